```python
import jax, jax.numpy as jnp
from jax import lax
import numpy as np

D_MODEL = 2048
BATCH = 4
SEQ = 4096
DEPTH = 2

GRID_W = 64
CTX_LEN = 256
EPS = 1e-6
GLA_HEADS = 4
GLA_DK = 128
GLA_DV = 256
GLA_RANK = 16
GLA_TAU = 16.0
GLA_CHUNK = 64
SG_GROUPS = 4
SG_CH = 128
SG_CHUNK = 128
FT_GROUPS = 4
FT_CH = 128
N_BRANCH = 3
N_EXPERTS = 32
TOP_K = 4
D_EXPERT = 1024
SWIGLU_LIMIT = 7.0
SWIGLU_ALPHA = 1.702
MOE_BLOCK = 256

W_GLA_Q = GLA_HEADS * GLA_DK
W_GLA_V = GLA_HEADS * GLA_DV
W_SG = SG_GROUPS * SG_CH
W_FT = FT_GROUPS * FT_CH
SPLITS = (W_GLA_Q, W_GLA_Q, W_GLA_V, W_GLA_V, 2 * GLA_RANK, W_SG, W_SG, W_FT, N_BRANCH * D_MODEL)
D_IN = 2 * W_GLA_Q + 2 * W_GLA_V + 2 * GLA_RANK + 2 * W_SG + W_FT + N_BRANCH * D_MODEL

kernel_name = "hybrid_gla_gmlp_fnet_moe_dit"


def rmsnorm(x, g):
    xf = x.astype(jnp.float32)
    y = xf * lax.rsqrt(jnp.mean(xf * xf, axis=-1, keepdims=True) + EPS)
    return (y * g.astype(jnp.float32)).astype(x.dtype)


def modulate(h, shift, scale):
    return h * (1 + scale) + shift


def grid_pos_embed(n, d):
    rows = n // GRID_W
    row = jnp.broadcast_to(jnp.arange(rows, dtype=jnp.float32)[:, None], (rows, GRID_W)).reshape(-1)
    col = jnp.broadcast_to(jnp.arange(GRID_W, dtype=jnp.float32)[None, :], (rows, GRID_W)).reshape(-1)
    quarter = d // 4
    omega = 1.0 / (10000.0 ** (jnp.arange(quarter, dtype=jnp.float32) / quarter))

    def enc(p):
        a = p[:, None] * omega
        return jnp.concatenate([jnp.sin(a), jnp.cos(a)], axis=-1)

    return jnp.concatenate([enc(row), enc(col)], axis=-1)


def project(h, w_in):
    z = h @ w_in
    idx = np.cumsum(SPLITS)[:-1].tolist()
    return tuple(jnp.split(z, idx, axis=-1))


def gla_scan(q, k, v, log_a, s0):
    b, l, h, _ = q.shape
    dv = v.shape[-1]
    n = l // GLA_CHUNK

    def chunks(t):
        return t.reshape(b, n, GLA_CHUNK, h, t.shape[-1]).transpose(1, 0, 3, 2, 4)

    mask = jnp.tril(jnp.ones((GLA_CHUNK, GLA_CHUNK), dtype=bool))[..., None]

    def step(s, inp):
        qc, kc, vc, lac = inp
        cum = jnp.cumsum(lac, axis=2)
        last = cum[:, :, -1:, :]
        o_inter = jnp.einsum('bhcd,bhde->bhce', qc * jnp.exp(cum), s)
        diff = cum[:, :, :, None, :] - cum[:, :, None, :, :]
        decay = jnp.exp(jnp.where(mask, diff, -jnp.inf))
        scores = jnp.einsum('bhid,bhijd->bhij', qc, decay * kc[:, :, None, :, :])
        o = o_inter + jnp.einsum('bhij,bhje->bhie', scores, vc)
        s_new = jnp.exp(last[:, :, 0, :])[..., None] * s + jnp.einsum(
            'bhcd,bhce->bhde', kc * jnp.exp(last - cum), vc)
        return s_new, o

    s_fin, o = lax.scan(step, s0, (chunks(q), chunks(k), chunks(v), chunks(log_a)))
    o = o.transpose(1, 0, 3, 2, 4).reshape(b, l, h, dv)
    return o, s_fin


def gla_mix(parts, w_a2, b_a, s_f, s_b):
    q, k, v, _, a_lr, _, _, _, _ = parts
    b, l, _ = q.shape
    f32 = jnp.float32
    q = (q.astype(f32) * GLA_DK ** -0.5).reshape(b, l, GLA_HEADS, GLA_DK)
    k = k.astype(f32).reshape(b, l, GLA_HEADS, GLA_DK)
    v = v.astype(f32).reshape(b, l, GLA_HEADS, GLA_DV)
    a = a_lr.astype(f32).reshape(b, l, 2, GLA_RANK)
    log_a = jax.nn.log_sigmoid(
        jnp.einsum('blzr,zrk->blzk', a, w_a2.astype(f32)) + b_a.astype(f32)) / GLA_TAU
    log_a = log_a.reshape(b, l, 2, GLA_HEADS, GLA_DK)
    o_f, s_f = gla_scan(q, k, v, log_a[:, :, 0], s_f)
    rev = lambda t: jnp.flip(t, axis=1)
    o_b, s_b = gla_scan(rev(q), rev(k), rev(v), rev(log_a[:, :, 1]), s_b)
    return o_f + rev(o_b), s_f, s_b


def spatial_gating(su, sv, w_s, b_s):
    b, l, _ = su.shape
    n = l // SG_CHUNK
    u = jax.nn.gelu(su)
    v = jax.nn.gelu(sv).reshape(b, n, SG_CHUNK, SG_GROUPS, SG_CH).astype(jnp.float32)
    mu = jnp.mean(v, axis=-1, keepdims=True)
    var = jnp.mean(jnp.square(v - mu), axis=-1, keepdims=True)
    v = ((v - mu) * lax.rsqrt(var + EPS)).astype(su.dtype)
    v = jnp.einsum('gij,bnjgc->bnigc', w_s, v) + b_s.T[:, :, None]
    return u * v.reshape(b, l, W_SG)


def fourier_mix(f):
    b, l, _ = f.shape
    fg = f.astype(jnp.float32).reshape(b, l, FT_GROUPS, FT_CH)
    y = jnp.fft.fftn(fg, axes=(1, 3), norm='ortho').real
    return y.reshape(b, l, W_FT).astype(f.dtype)


def merge_out(parts, o, gla_norm, w_s, b_s, p_gla, p_sg, p_ft, w_out):
    _, _, _, r, _, su, sv, f, gates = parts
    b, l, _ = r.shape
    dt = r.dtype
    o = o * lax.rsqrt(jnp.mean(o * o, axis=-1, keepdims=True) + EPS)
    y_gla = (o.reshape(b, l, W_GLA_V) * gla_norm.astype(jnp.float32)).astype(dt) * jax.nn.silu(r)
    y_sg = spatial_gating(su, sv, w_s, b_s)
    y_ft = fourier_mix(f)
    g_gla, g_sg, g_ft = jnp.split(jax.nn.sigmoid(gates), N_BRANCH, axis=-1)
    merged = g_gla * (y_gla @ p_gla) + g_sg * (y_sg @ p_sg) + g_ft * (y_ft @ p_ft)
    return merged @ w_out


def moe(xt, w_router, b_router, w_gu, b_gu, w_down, b_down):
    t, d = xt.shape
    logits = (xt @ w_router).astype(jnp.float32) + b_router.astype(jnp.float32)
    top_val, top_idx = lax.top_k(logits, TOP_K)
    probs = jax.nn.softmax(top_val, axis=-1)
    m = t * TOP_K
    flat_e = top_idx.reshape(m)
    flat_tok = jnp.broadcast_to(jnp.arange(t, dtype=jnp.int32)[:, None], (t, TOP_K)).reshape(m)
    flat_p = probs.reshape(m)
    order = jnp.argsort(flat_e)
    se = flat_e[order]
    counts = jnp.zeros((N_EXPERTS,), jnp.int32).at[flat_e].add(1)
    pad_counts = (counts + MOE_BLOCK - 1) // MOE_BLOCK * MOE_BLOCK
    pad_end = jnp.cumsum(pad_counts)
    pad_start = pad_end - pad_counts
    start = jnp.cumsum(counts) - counts
    dest = pad_start[se] + jnp.arange(m, dtype=jnp.int32) - start[se]
    nb = (m + MOE_BLOCK - 1) // MOE_BLOCK + N_EXPERTS
    rows = nb * MOE_BLOCK
    row_tok = jnp.full((rows,), t, jnp.int32).at[dest].set(flat_tok[order])
    row_p = jnp.zeros((rows,), jnp.float32).at[dest].set(flat_p[order])
    blk_e = jnp.minimum(
        jnp.searchsorted(pad_end, jnp.arange(nb, dtype=jnp.int32) * MOE_BLOCK, side='right'),
        N_EXPERTS - 1)
    x_pad = jnp.concatenate([xt, jnp.zeros((1, d), xt.dtype)], axis=0)

    def expert_block(args):
        tok, e = args
        xb = x_pad[tok]
        gu = xb @ w_gu[e] + b_gu[e]
        gate, up = jnp.split(gu, 2, axis=-1)
        gate = jnp.minimum(gate, SWIGLU_LIMIT)
        up = jnp.clip(up, -SWIGLU_LIMIT, SWIGLU_LIMIT)
        hdn = (up + 1) * (gate * jax.nn.sigmoid(SWIGLU_ALPHA * gate))
        return hdn @ w_down[e] + b_down[e]

    yb = lax.map(expert_block, (row_tok.reshape(nb, MOE_BLOCK), blk_e))
    y = jnp.zeros((t + 1, d), jnp.float32).at[row_tok].add(
        yb.reshape(rows, d).astype(jnp.float32) * row_p[:, None])
    return y[:t].astype(xt.dtype)


def setup_inputs(seed: int = 0) -> dict:
    key = jax.random.key(seed)
    ks = iter(jax.random.split(key, 25))
    f32 = jnp.float32
    D = D_MODEL

    def nrm(shape, scale):
        return jax.random.normal(next(ks), shape, f32) * scale

    return {
        "x": nrm((BATCH, SEQ, D), 1.0),
        "c": nrm((BATCH, D), 1.0),
        "ctx": nrm((BATCH, CTX_LEN, D), 1.0),
        "c_ctx": nrm((D,), 1.0),
        "w_mod": nrm((DEPTH, D, 6 * D), D ** -0.5),
        "b_mod": nrm((DEPTH, 6 * D), 0.01),
        "norm1": 1.0 + nrm((DEPTH, D), 0.01),
        "w_in": nrm((DEPTH, D, D_IN), D ** -0.5),
        "w_a2": nrm((DEPTH, 2, GLA_RANK, W_GLA_Q), GLA_RANK ** -0.5),
        "b_a": nrm((DEPTH, 2, W_GLA_Q), 0.1),
        "gla_norm": 1.0 + nrm((DEPTH, W_GLA_V), 0.01),
        "w_s": nrm((DEPTH, SG_GROUPS, SG_CHUNK, SG_CHUNK), SG_CHUNK ** -0.5),
        "b_s": 1.0 + nrm((DEPTH, SG_GROUPS, SG_CHUNK), 0.01),
        "p_gla": nrm((DEPTH, W_GLA_V, D), W_GLA_V ** -0.5),
        "p_sg": nrm((DEPTH, W_SG, D), W_SG ** -0.5),
        "p_ft": nrm((DEPTH, W_FT, D), W_FT ** -0.5),
        "w_out": nrm((DEPTH, D, D), D ** -0.5),
        "norm2": 1.0 + nrm((DEPTH, D), 0.01),
        "w_router": nrm((DEPTH, D, N_EXPERTS), D ** -0.5),
        "b_router": nrm((DEPTH, N_EXPERTS), 0.01),
        "w_gu": nrm((DEPTH, N_EXPERTS, D, 2 * D_EXPERT), D ** -0.5),
        "b_gu": nrm((DEPTH, N_EXPERTS, 2 * D_EXPERT), 0.01),
        "w_down": nrm((DEPTH, N_EXPERTS, D_EXPERT, D), D_EXPERT ** -0.5),
        "b_down": nrm((DEPTH, N_EXPERTS, D), 0.01),
        "norm_f": 1.0 + nrm((D,), 0.01),
    }


def reference(x, c, ctx, c_ctx, w_mod, b_mod, norm1, w_in, w_a2, b_a, gla_norm, w_s, b_s,
              p_gla, p_sg, p_ft, w_out, norm2, w_router, b_router, w_gu, b_gu, w_down, b_down,
              norm_f):
    b, l, d = x.shape
    n_ctx_tok = ctx.shape[0] * ctx.shape[1]
    x = x + grid_pos_embed(l, d).astype(x.dtype)[None]
    cond_lat = jax.nn.silu(c)
    cond_ctx = jax.nn.silu(c_ctx)
    zero_state = jnp.zeros((b, GLA_HEADS, GLA_DK, GLA_DV), jnp.float32)
    for i in range(DEPTH):
        last = i == DEPTH - 1
        mod_lat = (cond_lat @ w_mod[i] + b_mod[i])[:, None, :]
        mod_ctx = (cond_ctx @ w_mod[i] + b_mod[i])[None, None, :]
        sh1, sc1, g1, sh2, sc2, g2 = jnp.split(mod_lat, 6, axis=-1)
        csh1, csc1, cg1, csh2, csc2, cg2 = jnp.split(mod_ctx, 6, axis=-1)
        layer_mix = (gla_norm[i], w_s[i], b_s[i], p_gla[i], p_sg[i], p_ft[i], w_out[i])

        zc = project(modulate(rmsnorm(ctx, norm1[i]), csh1, csc1), w_in[i])
        o_c, s_f, s_b = gla_mix(zc, w_a2[i], b_a[i], zero_state, zero_state)
        zl = project(modulate(rmsnorm(x, norm1[i]), sh1, sc1), w_in[i])
        o_l, _, _ = gla_mix(zl, w_a2[i], b_a[i], s_f, s_b)
        x = x + g1 * merge_out(zl, o_l, *layer_mix)
        if not last:
            ctx = ctx + cg1 * merge_out(zc, o_c, *layer_mix)

        hl2 = modulate(rmsnorm(x, norm2[i]), sh2, sc2).reshape(-1, d)
        moe_p = (w_router[i], b_router[i], w_gu[i], b_gu[i], w_down[i], b_down[i])
        if last:
            x = x + g2 * moe(hl2, *moe_p).reshape(x.shape)
        else:
            hc2 = modulate(rmsnorm(ctx, norm2[i]), csh2, csc2).reshape(-1, d)
            y = moe(jnp.concatenate([hc2, hl2], axis=0), *moe_p)
            ctx = ctx + cg2 * y[:n_ctx_tok].reshape(ctx.shape)
            x = x + g2 * y[n_ctx_tok:].reshape(x.shape)
    return rmsnorm(x, norm_f)
```

```python
import functools
import math

import jax
import jax.numpy as jnp
from jax import lax
from jax.experimental import pallas as pl
from jax.experimental.pallas import tpu as pltpu

F32 = jnp.float32
BF16 = jnp.bfloat16

EPS = 1e-6
GRID_W = 64
GLA_HEADS = 4
GLA_DK = 128
GLA_DV = 256
GLA_RANK = 16
GLA_TAU = 16.0
GLA_CHUNK = 64
GLA_SUB = 16
SG_GROUPS = 4
SG_CH = 128
SG_CHUNK = 128
FT_GROUPS = 4
FT_CH = 128
N_EXPERTS = 32
TOP_K = 4
SWIGLU_LIMIT = 7.0
SWIGLU_ALPHA = 1.702
MOE_BLOCK = 256

LANES = 128
ROW_TILE = 256
VMEM_LIMIT = 56 * 1024 * 1024

W_Q = GLA_HEADS * GLA_DK
W_V = GLA_HEADS * GLA_DV
W_SG = SG_GROUPS * SG_CH
W_FT = FT_GROUPS * FT_CH


def _cparams(*sem):
    return pltpu.CompilerParams(dimension_semantics=sem, vmem_limit_bytes=VMEM_LIMIT)


def _sigmoid(x):
    return 1.0 / (1.0 + jnp.exp(-x))


def _gelu_tanh(x):
    return 0.5 * x * (1.0 + jnp.tanh(math.sqrt(2.0 / math.pi) * (x + 0.044715 * (x * x * x))))


def _dot(a, b):
    return jnp.dot(a, b, preferred_element_type=F32)


def _mod_row(tile, n_ctx_tiles, tiles_per_batch, ctx_row):
    return jnp.where(tile < n_ctx_tiles, ctx_row, (tile - n_ctx_tiles) // tiles_per_batch)


def _mod_kernel(c_ref, w_ref, b_ref, o_ref):
    c = c_ref[...]
    s = (c * _sigmoid(c)).astype(BF16)
    o_ref[0] = _dot(s, w_ref[0].astype(BF16)) + b_ref[0]


def _mod_vectors(cond, w_mod, b_mod):
    depth, d, n = w_mod.shape
    tn = 1024
    return pl.pallas_call(
        _mod_kernel,
        grid=(depth, n // tn),
        in_specs=[
            pl.BlockSpec((8, d), lambda l, j: (0, 0)),
            pl.BlockSpec((1, d, tn), lambda l, j: (l, 0, j)),
            pl.BlockSpec((1, 1, tn), lambda l, j: (l, 0, j)),
        ],
        out_specs=pl.BlockSpec((1, 8, tn), lambda l, j: (l, 0, j)),
        out_shape=jax.ShapeDtypeStruct((depth, 8, n), F32),
        compiler_params=_cparams("parallel", "parallel"),
        name="mod_vectors",
    )(cond, w_mod, b_mod.reshape(depth, 1, n))


def _proj_kernel(x_ref, sh_ref, sc_ref, g_ref, w_ref, wa_ref, wa2_ref, ba_ref,
                 z_ref, la_ref, a_scr, *, n_ctx_tiles, tiles_per_batch, ctx_row):
    i = pl.program_id(0)
    j = pl.program_id(1)

    @pl.when(j == 0)
    def _():
        row = _mod_row(i, n_ctx_tiles, tiles_per_batch, ctx_row)
        scale = 1.0 + sc_ref[pl.ds(row, 1), :]
        shift = sh_ref[pl.ds(row, 1), :]

        def body(r, carry):
            rows = pl.ds(pl.multiple_of(r * ROW_TILE, ROW_TILE), ROW_TILE)
            x = x_ref[rows, :]
            y = x * lax.rsqrt(jnp.mean(x * x, axis=-1, keepdims=True) + EPS) * g_ref[...]
            hb = (y * scale + shift).astype(BF16)
            a_scr[rows, :] = hb
            a_lr = _dot(hb, wa_ref[...])
            pre = _dot(a_lr.astype(BF16), wa2_ref[...]) + ba_ref[...]
            la_ref[rows, :] = -(jnp.maximum(-pre, 0.0) + jnp.log1p(jnp.exp(-jnp.abs(pre)))) * (1.0 / GLA_TAU)
            return carry

        lax.fori_loop(0, x_ref.shape[0] // ROW_TILE, body, 0)

    z_ref[...] = _dot(a_scr[...], w_ref[...])


def _project(x, mod, g, w_main, w_a, w_a2, b_a, *, tm, tn, n_ctx_tiles, tiles_per_batch, ctx_row):
    t, d = x.shape
    n = w_main.shape[1]
    kern = functools.partial(_proj_kernel, n_ctx_tiles=n_ctx_tiles,
                             tiles_per_batch=tiles_per_batch, ctx_row=ctx_row)
    return pl.pallas_call(
        kern,
        grid=(t // tm, n // tn),
        in_specs=[
            pl.BlockSpec((tm, d), lambda i, j: (i, 0)),
            pl.BlockSpec((8, d), lambda i, j: (0, 0)),
            pl.BlockSpec((8, d), lambda i, j: (0, 1)),
            pl.BlockSpec((1, d), lambda i, j: (0, 0)),
            pl.BlockSpec((d, tn), lambda i, j: (0, j)),
            pl.BlockSpec((d, LANES), lambda i, j: (0, 0)),
            pl.BlockSpec((LANES, 2 * W_Q), lambda i, j: (0, 0)),
            pl.BlockSpec((1, 2 * W_Q), lambda i, j: (0, 0)),
        ],
        out_specs=[
            pl.BlockSpec((tm, tn), lambda i, j: (i, j)),
            pl.BlockSpec((tm, 2 * W_Q), lambda i, j: (i, 0)),
        ],
        out_shape=[
            jax.ShapeDtypeStruct((t, n), F32),
            jax.ShapeDtypeStruct((t, 2 * W_Q), F32),
        ],
        scratch_shapes=[pltpu.VMEM((tm, d), BF16)],
        compiler_params=_cparams("parallel", "arbitrary"),
        name="norm_mod_project",
    )(x, mod, mod, g, w_main, w_a, w_a2, b_a)


def _gla_chunk(q_ref, k_ref, v_ref, la_ref, o_ref, s_ref, cum_ref, r0, rev):
    c, sub = GLA_CHUNK, GLA_SUB
    nsub = c // sub
    rows = pl.ds(r0, c)
    q = q_ref[rows, :] * (GLA_DK ** -0.5)
    k = k_ref[rows, :]
    v = v_ref[rows, :].astype(BF16)
    la = la_ref[rows, :]

    ri = lax.broadcasted_iota(jnp.int32, (c, c), 0)
    ci = lax.broadcasted_iota(jnp.int32, (c, c), 1)
    before = (ci >= ri) if rev else (ci <= ri)
    tri = jnp.where(before, 1.0, 0.0).astype(BF16)
    la_hi = la.astype(BF16)
    rem = la - la_hi.astype(F32)
    la_mid = rem.astype(BF16)
    la_lo = (rem - la_mid.astype(F32)).astype(BF16)
    cum = _dot(tri, la_hi) + _dot(tri, la_mid) + _dot(tri, la_lo)
    cum_ref[...] = cum
    last = 0 if rev else c - 1
    total = cum[last:last + 1, :]

    s = s_ref[...]
    o = _dot((q * jnp.exp(cum)).astype(BF16), s.astype(BF16))

    s_rows = []
    for a in range(nsub):
        has_earlier = (a < nsub - 1) if rev else (a > 0)
        if not has_earlier:
            s_rows.append(jnp.zeros((sub, c), F32))
            continue
        first = a * sub + (sub - 1 if rev else 0)
        c_a = cum[first:first + 1, :]
        qt = q[a * sub:(a + 1) * sub, :] * jnp.exp(cum[a * sub:(a + 1) * sub, :] - c_a)
        kt = k * jnp.exp(jnp.minimum(c_a - cum, 0.0))
        s_rows.append(lax.dot_general(qt.astype(BF16), kt.astype(BF16),
                                      (((1,), (1,)), ((), ())), preferred_element_type=F32))
    s_off = jnp.concatenate(s_rows, axis=0)

    diag = jnp.zeros((c, c), F32)
    for jj in range(sub):
        cj = jnp.concatenate(
            [jnp.broadcast_to(cum_ref[a * sub + jj:a * sub + jj + 1, :], (sub, GLA_DK)) for a in range(nsub)], axis=0)
        kj = jnp.concatenate(
            [jnp.broadcast_to(k_ref[pl.ds(r0 + a * sub + jj, 1), :], (sub, GLA_DK)) for a in range(nsub)], axis=0)
        e = jnp.exp(jnp.minimum(cum - cj, 0.0))
        col = jnp.sum(q * e * kj, axis=-1, keepdims=True)
        diag = jnp.where((ci % sub) == jj, col, diag)

    p = jnp.where((ri // sub) == (ci // sub), diag, s_off)
    p = jnp.where(before, p, 0.0)
    o_ref[rows, :] = o + _dot(p.astype(BF16), v)

    k_out = (k * jnp.exp(total - cum)).astype(BF16)
    upd = lax.dot_general(k_out, v, (((0,), (0,)), ((), ())), preferred_element_type=F32)
    dmat = jnp.broadcast_to(jnp.exp(total), (GLA_DK, GLA_DK)).T
    for half in range(GLA_DV // GLA_DK):
        cols = slice(half * GLA_DK, (half + 1) * GLA_DK)
        s_ref[:, cols] = s[:, cols] * dmat + upd[:, cols]


def _gla_kernel(qf, kf, vf, laf, qb, kb, vb, lab, of_ref, ob_ref, sf, sb, cumf, cumb):
    t = pl.program_id(2)

    @pl.when(t == 0)
    def _():
        sf[...] = jnp.zeros_like(sf)
        sb[...] = jnp.zeros_like(sb)

    nchunk = ROW_TILE // GLA_CHUNK

    def body(c, carry):
        r_f = pl.multiple_of(c * GLA_CHUNK, GLA_CHUNK)
        r_b = pl.multiple_of((nchunk - 1 - c) * GLA_CHUNK, GLA_CHUNK)
        _gla_chunk(qf, kf, vf, laf, of_ref, sf, cumf, r_f, False)
        _gla_chunk(qb, kb, vb, lab, ob_ref, sb, cumb, r_b, True)
        return carry

    lax.fori_loop(0, nchunk, body, 0)


def _gla(z, la, *, batch, lc, l, col_q):
    t_rows = z.shape[0]
    nctx = lc // ROW_TILE
    nlat = l // ROW_TILE
    steps = nctx + nlat
    qb0 = col_q // GLA_DK
    kb0 = (col_q + W_Q) // GLA_DK
    vb0 = (col_q + 2 * W_Q) // GLA_DV

    def fwd_tile(b, t):
        return jnp.where(t < nctx, b * nctx + t, batch * nctx + b * nlat + (t - nctx))

    def bwd_tile(b, t):
        return jnp.where(t < nctx, b * nctx + (nctx - 1 - t), batch * nctx + b * nlat + (steps - 1 - t))

    def spec(width, tile_fn, col0):
        return pl.BlockSpec((ROW_TILE, width), lambda b, h, t: (tile_fn(b, t), col0 + h))

    o_shape = jax.ShapeDtypeStruct((t_rows, W_V), F32)
    return pl.pallas_call(
        _gla_kernel,
        grid=(batch, GLA_HEADS, steps),
        in_specs=[
            spec(GLA_DK, fwd_tile, qb0), spec(GLA_DK, fwd_tile, kb0), spec(GLA_DV, fwd_tile, vb0),
            spec(GLA_DK, fwd_tile, 0),
            spec(GLA_DK, bwd_tile, qb0), spec(GLA_DK, bwd_tile, kb0), spec(GLA_DV, bwd_tile, vb0),
            spec(GLA_DK, bwd_tile, GLA_HEADS),
        ],
        out_specs=[spec(GLA_DV, fwd_tile, 0), spec(GLA_DV, bwd_tile, 0)],
        out_shape=[o_shape, o_shape],
        scratch_shapes=[pltpu.VMEM((GLA_DK, GLA_DV), F32), pltpu.VMEM((GLA_DK, GLA_DV), F32),
                        pltpu.VMEM((GLA_CHUNK, GLA_DK), F32), pltpu.VMEM((GLA_CHUNK, GLA_DK), F32)],
        compiler_params=_cparams("parallel", "parallel", "arbitrary"),
        name="gla_scan",
    )(z, z, z, la, z, z, z, la)


def _dft_tables(n):
    idx = jnp.arange(n, dtype=jnp.int32)
    ang = ((idx[:, None] * idx[None, :]) % n).astype(F32) * (2.0 * math.pi / n)
    scale = n ** -0.5
    return (jnp.cos(ang) * scale).astype(BF16), (jnp.sin(ang) * scale).astype(BF16)


def _ft_chan_kernel(f_ref, cs_ref, xc_ref, xs_ref):
    for g in range(FT_GROUPS):
        cols = slice(g * FT_CH, (g + 1) * FT_CH)
        y = _dot(f_ref[:, cols].astype(BF16), cs_ref[...])
        xc_ref[:, cols] = y[:, :FT_CH].astype(BF16)
        xs_ref[:, cols] = y[:, FT_CH:].astype(BF16)


def _ft_channel(z, cs, *, col_f):
    t_rows = z.shape[0]
    tm = 1024 if t_rows % 1024 == 0 else ROW_TILE
    shape = jax.ShapeDtypeStruct((t_rows, W_FT), BF16)
    return pl.pallas_call(
        _ft_chan_kernel,
        grid=(t_rows // tm,),
        in_specs=[pl.BlockSpec((tm, W_FT), lambda i: (i, col_f // W_FT)),
                  pl.BlockSpec((FT_CH, 2 * FT_CH), lambda i: (0, 0))],
        out_specs=[pl.BlockSpec((tm, W_FT), lambda i: (i, 0))] * 2,
        out_shape=[shape, shape],
        compiler_params=_cparams("parallel"),
        name="fourier_channels",
    )(z, cs)


def _ft_pos_kernel(c_ref, s_ref, xc_ref, xs_ref, o_ref):
    o_ref[...] = _dot(c_ref[...], xc_ref[...]) - _dot(s_ref[...], xs_ref[...])


def _ft_positions(cos_t, sin_t, xc, xs, *, batch, seq):
    nt = seq // ROW_TILE
    x_spec = pl.BlockSpec((seq, W_FT), lambda b, j: (b, 0))
    t_spec = pl.BlockSpec((ROW_TILE, seq), lambda b, j: (j, 0))
    return pl.pallas_call(
        _ft_pos_kernel,
        grid=(batch, nt),
        in_specs=[t_spec, t_spec, x_spec, x_spec],
        out_specs=pl.BlockSpec((ROW_TILE, W_FT), lambda b, j: (b * nt + j, 0)),
        out_shape=jax.ShapeDtypeStruct((batch * seq, W_FT), F32),
        compiler_params=_cparams("parallel", "parallel"),
        name="fourier_positions",
    )(cos_t, sin_t, xc, xs)


def _branch_kernel(gg_ref, gs_ref, gf_ref, of_ref, ob_ref, r_ref, su_ref, sv_ref, yft_ref,
                   gn_ref, ws_ref, bs_ref, pg_ref, ps_ref, pf_ref, m_ref):
    o = of_ref[...] + ob_ref[...]
    heads = []
    for h in range(GLA_HEADS):
        oh = o[:, h * GLA_DV:(h + 1) * GLA_DV]
        heads.append(oh * lax.rsqrt(jnp.mean(oh * oh, axis=-1, keepdims=True) + EPS))
    r = r_ref[...]
    y_gla = (jnp.concatenate(heads, axis=-1) * gn_ref[...]) * (r * _sigmoid(r))
    acc = _sigmoid(gg_ref[...]) * _dot(y_gla.astype(BF16), pg_ref[...])

    u = _gelu_tanh(su_ref[...])
    v = _gelu_tanh(sv_ref[...])
    groups = []
    for g in range(SG_GROUPS):
        vg = v[:, g * SG_CH:(g + 1) * SG_CH]
        mu = jnp.mean(vg, axis=-1, keepdims=True)
        var = jnp.mean(jnp.square(vg - mu), axis=-1, keepdims=True)
        vn = ((vg - mu) * lax.rsqrt(var + EPS)).astype(BF16)
        chunks = []
        for c in range(ROW_TILE // SG_CHUNK):
            chunks.append(_dot(ws_ref[g], vn[c * SG_CHUNK:(c + 1) * SG_CHUNK, :]) + bs_ref[g])
        groups.append(jnp.concatenate(chunks, axis=0))
    y_sg = u * jnp.concatenate(groups, axis=-1)
    acc = acc + _sigmoid(gs_ref[...]) * _dot(y_sg.astype(BF16), ps_ref[...])

    acc = acc + _sigmoid(gf_ref[...]) * _dot(yft_ref[...].astype(BF16), pf_ref[...])
    m_ref[...] = acc.astype(BF16)


def _branches(z, o_f, o_b, y_ft, gla_norm, w_s, b_s, p_gla, p_sg, p_ft, *, cols):
    t_rows = z.shape[0]
    d = p_gla.shape[1]
    tm = ROW_TILE

    def zspec(width, col0):
        return pl.BlockSpec((tm, width), lambda i: (i, col0 // width))

    def rows(width):
        return pl.BlockSpec((tm, width), lambda i: (i, 0))

    def const(shape):
        return pl.BlockSpec(shape, lambda i: (0,) * len(shape))

    return pl.pallas_call(
        _branch_kernel,
        grid=(t_rows // tm,),
        in_specs=[
            zspec(d, cols["gates"]), zspec(d, cols["gates"] + d), zspec(d, cols["gates"] + 2 * d),
            rows(W_V), rows(W_V), zspec(W_V, cols["r"]), zspec(W_SG, cols["su"]), zspec(W_SG, cols["sv"]),
            rows(W_FT),
            const((1, W_V)), const((SG_GROUPS, SG_CHUNK, SG_CHUNK)), const((SG_GROUPS, SG_CHUNK, SG_CH)),
            const((W_V, d)), const((W_SG, d)), const((W_FT, d)),
        ],
        out_specs=rows(d),
        out_shape=jax.ShapeDtypeStruct((t_rows, d), BF16),
        compiler_params=_cparams("parallel"),
        name="branch_merge",
    )(z, z, z, o_f, o_b, z, z, z, y_ft, gla_norm, w_s, b_s, p_gla, p_sg, p_ft)


def _outproj_kernel(m_ref, x_ref, w_ref, g1_ref, n2_ref, sh_ref, sc_ref, wr_ref, br_ref,
                    xo_ref, h_ref, lg_ref, *, n_ctx_tiles, tiles_per_batch, ctx_row):
    row = _mod_row(pl.program_id(0), n_ctx_tiles, tiles_per_batch, ctx_row)
    x = x_ref[...] + g1_ref[pl.ds(row, 1), :] * _dot(m_ref[...], w_ref[...])
    xo_ref[...] = x
    y = x * lax.rsqrt(jnp.mean(x * x, axis=-1, keepdims=True) + EPS) * n2_ref[...]
    h = y * (1.0 + sc_ref[pl.ds(row, 1), :]) + sh_ref[pl.ds(row, 1), :]
    h_ref[...] = h.astype(BF16)
    lg_ref[...] = jnp.dot(h, wr_ref[...], preferred_element_type=F32,
                          precision=lax.Precision.HIGHEST) + br_ref[...]


def _out_project(merged, x, w_out, mod, norm2, w_router, b_router, *, n_ctx_tiles, tiles_per_batch, ctx_row):
    t_rows, d = x.shape
    tm = ROW_TILE
    kern = functools.partial(_outproj_kernel, n_ctx_tiles=n_ctx_tiles,
                             tiles_per_batch=tiles_per_batch, ctx_row=ctx_row)

    def rows(width):
        return pl.BlockSpec((tm, width), lambda i: (i, 0))

    def modspec(k):
        return pl.BlockSpec((8, d), lambda i: (0, k))

    return pl.pallas_call(
        kern,
        grid=(t_rows // tm,),
        in_specs=[rows(d), rows(d), pl.BlockSpec((d, d), lambda i: (0, 0)),
                  modspec(2), pl.BlockSpec((1, d), lambda i: (0, 0)), modspec(3), modspec(4),
                  pl.BlockSpec((d, LANES), lambda i: (0, 0)), pl.BlockSpec((1, LANES), lambda i: (0, 0))],
        out_specs=[rows(d), rows(d), rows(LANES)],
        out_shape=[jax.ShapeDtypeStruct((t_rows, d), F32), jax.ShapeDtypeStruct((t_rows, d), BF16),
                   jax.ShapeDtypeStruct((t_rows, LANES), F32)],
        compiler_params=_cparams("parallel"),
        name="out_project",
    )(merged, x, w_out, mod, norm2, mod, mod, w_router, b_router)


def _expert_kernel(be_ref, nu_ref, x_ref, wgu_ref, bgu_ref, wd_ref, bd_ref, y_ref):
    @pl.when(pl.program_id(0) < nu_ref[0])
    def _():
        gu = _dot(x_ref[...], wgu_ref[0]) + bgu_ref[0]
        de = gu.shape[1] // 2
        gate = jnp.minimum(gu[:, :de], SWIGLU_LIMIT)
        up = jnp.clip(gu[:, de:], -SWIGLU_LIMIT, SWIGLU_LIMIT)
        hdn = (up + 1.0) * (gate * _sigmoid(SWIGLU_ALPHA * gate))
        y_ref[...] = _dot(hdn.astype(BF16), wd_ref[0]) + bd_ref[0]


def _experts(xs, blk_e, n_used, w_gu, b_gu, w_down, b_down):
    rows_total, d = xs.shape
    nb = rows_total // MOE_BLOCK
    ne, _, n_gu = w_gu.shape
    de = w_down.shape[1]

    def blk(i, be, nu):
        return jnp.minimum(i, nu[0] - 1)

    grid_spec = pltpu.PrefetchScalarGridSpec(
        num_scalar_prefetch=2,
        grid=(nb,),
        in_specs=[
            pl.BlockSpec((MOE_BLOCK, d), lambda i, be, nu: (blk(i, be, nu), 0)),
            pl.BlockSpec((1, d, n_gu), lambda i, be, nu: (be[blk(i, be, nu)], 0, 0)),
            pl.BlockSpec((1, 1, n_gu), lambda i, be, nu: (be[blk(i, be, nu)], 0, 0)),
            pl.BlockSpec((1, de, d), lambda i, be, nu: (be[blk(i, be, nu)], 0, 0)),
            pl.BlockSpec((1, 1, d), lambda i, be, nu: (be[blk(i, be, nu)], 0, 0)),
        ],
        out_specs=pl.BlockSpec((MOE_BLOCK, d), lambda i, be, nu: (blk(i, be, nu), 0)),
    )
    return pl.pallas_call(
        _expert_kernel,
        grid_spec=grid_spec,
        out_shape=jax.ShapeDtypeStruct((rows_total, d), F32),
        compiler_params=_cparams("arbitrary"),
        name="moe_experts",
    )(blk_e, n_used, xs, w_gu, b_gu.reshape(ne, 1, n_gu), w_down, b_down.reshape(ne, 1, d))


def _route(logits, n_tok):
    top_val, top_idx = lax.top_k(logits, TOP_K)
    probs = jax.nn.softmax(top_val, axis=-1)
    m = n_tok * TOP_K
    flat_e = top_idx.reshape(m).astype(jnp.int32)
    onehot = (flat_e[:, None] == jnp.arange(N_EXPERTS, dtype=jnp.int32)[None, :]).astype(jnp.int32)
    csum = jnp.cumsum(onehot, axis=0)
    counts = csum[-1]
    rank = jnp.sum(csum * onehot, axis=1) - 1
    pad_counts = (counts + MOE_BLOCK - 1) // MOE_BLOCK * MOE_BLOCK
    pad_end = jnp.cumsum(pad_counts)
    pad_start = pad_end - pad_counts
    pos = pad_start[flat_e] + rank
    nb = (m + MOE_BLOCK - 1) // MOE_BLOCK + N_EXPERTS
    flat_tok = jnp.arange(m, dtype=jnp.int32) // TOP_K
    row_tok = jnp.zeros((nb * MOE_BLOCK,), jnp.int32).at[pos].set(flat_tok)
    blk_e = jnp.minimum(
        jnp.searchsorted(pad_end, jnp.arange(nb, dtype=jnp.int32) * MOE_BLOCK, side="right"),
        N_EXPERTS - 1).astype(jnp.int32)
    n_used = (pad_end[-1] // MOE_BLOCK).astype(jnp.int32).reshape(1)
    return probs, pos.reshape(n_tok, TOP_K), row_tok, blk_e, n_used


def _moe(h2, logits, w_gu, b_gu, w_down, b_down):
    n_tok = h2.shape[0]
    probs, pos, row_tok, blk_e, n_used = _route(logits[:, :N_EXPERTS], n_tok)
    xs = jnp.take(h2, row_tok, axis=0)
    yb = _experts(xs, blk_e, n_used, w_gu, b_gu, w_down, b_down)
    y = jnp.take(yb, pos.reshape(-1), axis=0).reshape(n_tok, TOP_K, -1)
    return jnp.sum(y * probs[:, :, None], axis=1)


def _final_kernel(x_ref, g_ref, o_ref):
    x = x_ref[...]
    o_ref[...] = x * lax.rsqrt(jnp.mean(x * x, axis=-1, keepdims=True) + EPS) * g_ref[...]


def _final_norm(x, g, *, row0, n_rows):
    d = x.shape[1]
    tm = ROW_TILE
    return pl.pallas_call(
        _final_kernel,
        grid=(n_rows // tm,),
        in_specs=[pl.BlockSpec((tm, d), lambda i: (row0 // tm + i, 0)),
                  pl.BlockSpec((1, d), lambda i: (0, 0))],
        out_specs=pl.BlockSpec((tm, d), lambda i: (i, 0)),
        out_shape=jax.ShapeDtypeStruct((n_rows, d), F32),
        compiler_params=_cparams("parallel"),
        name="final_norm",
    )(x, g)


def _grid_pos_embed(n, d):
    rows = n // GRID_W
    row = jnp.broadcast_to(jnp.arange(rows, dtype=F32)[:, None], (rows, GRID_W)).reshape(-1)
    col = jnp.broadcast_to(jnp.arange(GRID_W, dtype=F32)[None, :], (rows, GRID_W)).reshape(-1)
    quarter = d // 4
    omega = 1.0 / (10000.0 ** (jnp.arange(quarter, dtype=F32) / quarter))

    def enc(p):
        a = p[:, None] * omega
        return jnp.concatenate([jnp.sin(a), jnp.cos(a)], axis=-1)

    return jnp.concatenate([enc(row), enc(col)], axis=-1)


def kernel(x, c, ctx, c_ctx, w_mod, b_mod, norm1, w_in, w_a2, b_a, gla_norm, w_s, b_s, p_gla, p_sg, p_ft,
           w_out, norm2, w_router, b_router, w_gu, b_gu, w_down, b_down, norm_f):
    batch, l, d = x.shape
    lc = ctx.shape[1]
    depth = w_mod.shape[0]
    n_ctx_rows = batch * lc
    t_rows = n_ctx_rows + batch * l
    assert lc % ROW_TILE == 0 and l % ROW_TILE == 0 and batch < 8

    tm1 = math.gcd(math.gcd(n_ctx_rows, l), 1024)
    assert tm1 % ROW_TILE == 0
    proj_tiles = dict(n_ctx_tiles=n_ctx_rows // tm1, tiles_per_batch=l // tm1, ctx_row=batch)
    row_tiles = dict(n_ctx_tiles=n_ctx_rows // ROW_TILE, tiles_per_batch=l // ROW_TILE, ctx_row=batch)

    o_q = 0
    o_k = o_q + W_Q
    o_v = o_k + W_Q
    o_r = o_v + W_V
    o_a = o_r + W_V
    o_su = o_a + 2 * GLA_RANK
    o_sv = o_su + W_SG
    o_f = o_sv + W_SG
    o_g = o_f + W_FT
    n_gates = 3 * d
    cols = dict(gates=0, q=n_gates, r=n_gates + 2 * W_Q + W_V, su=n_gates + 2 * W_Q + 2 * W_V)
    cols["sv"] = cols["su"] + W_SG
    cols["f"] = cols["sv"] + W_SG
    n_main = cols["f"] + W_FT

    xt = jnp.concatenate([ctx.reshape(n_ctx_rows, d),
                          (x + _grid_pos_embed(l, d)[None]).reshape(batch * l, d)], axis=0)

    cond = jnp.zeros((8, d), F32).at[:batch].set(c).at[batch].set(c_ctx)
    mods = _mod_vectors(cond, w_mod, b_mod)

    cc, sc = _dft_tables(FT_CH)
    chan_tab = jnp.concatenate([cc, sc], axis=1)
    cos_l, sin_l = _dft_tables(l)
    cos_c, sin_c = _dft_tables(lc)

    for i in range(depth):
        last = i == depth - 1
        wi = w_in[i]
        w_main = jnp.concatenate(
            [wi[:, o_g:], wi[:, o_q:o_a], wi[:, o_su:o_g]], axis=1).astype(BF16)
        w_a = jnp.pad(wi[:, o_a:o_su], ((0, 0), (0, LANES - 2 * GLA_RANK))).astype(BF16)
        wa2 = jnp.zeros((LANES, 2 * W_Q), F32)
        wa2 = wa2.at[:GLA_RANK, :W_Q].set(w_a2[i, 0]).at[GLA_RANK:2 * GLA_RANK, W_Q:].set(w_a2[i, 1])
        mod = mods[i]

        z, la = _project(xt, mod, norm1[i][None], w_main, w_a, wa2.astype(BF16), b_a[i].reshape(1, 2 * W_Q),
                         tm=tm1, tn=768, **proj_tiles)
        assert z.shape[1] == n_main

        o_fw, o_bw = _gla(z, la, batch=batch, lc=lc, l=l, col_q=cols["q"])

        xc, xs = _ft_channel(z, chan_tab, col_f=cols["f"])
        y_ft = jnp.concatenate([
            _ft_positions(cos_c, sin_c, xc[:n_ctx_rows], xs[:n_ctx_rows], batch=batch, seq=lc),
            _ft_positions(cos_l, sin_l, xc[n_ctx_rows:], xs[n_ctx_rows:], batch=batch, seq=l),
        ], axis=0)

        bs_b = jnp.broadcast_to(b_s[i][:, :, None], (SG_GROUPS, SG_CHUNK, SG_CH))
        merged = _branches(z, o_fw, o_bw, y_ft, gla_norm[i][None], w_s[i].astype(BF16), bs_b,
                           p_gla[i].astype(BF16), p_sg[i].astype(BF16), p_ft[i].astype(BF16), cols=cols)

        w_r = jnp.pad(w_router[i], ((0, 0), (0, LANES - N_EXPERTS)))
        b_r = jnp.pad(b_router[i], (0, LANES - N_EXPERTS)).reshape(1, LANES)
        xt, h2, logits = _out_project(merged, xt, w_out[i].astype(BF16), mod, norm2[i][None], w_r, b_r,
                                      **row_tiles)

        wgu = w_gu[i].astype(BF16)
        wdn = w_down[i].astype(BF16)
        row0 = n_ctx_rows if last else 0
        y = _moe(h2[row0:], logits[row0:], wgu, b_gu[i], wdn, b_down[i])
        g2 = mod[:, 5 * d:6 * d]
        g_rows = jnp.concatenate([jnp.broadcast_to(g2[batch][None], (n_ctx_rows, d)),
                                  jnp.repeat(g2[:batch], l, axis=0)], axis=0)
        xt = xt.at[row0:].add(g_rows[row0:] * y)

    out = _final_norm(xt, norm_f[None], row0=n_ctx_rows, n_rows=batch * l)
    return out.reshape(batch, l, d)
```

```python
import functools
import math

import jax
import jax.numpy as jnp
from jax import lax
from jax.experimental import pallas as pl
from jax.experimental.pallas import tpu as pltpu

F32 = jnp.float32
BF16 = jnp.bfloat16

EPS = 1e-6
GRID_W = 64
GLA_HEADS = 4
GLA_DK = 128
GLA_DV = 256
GLA_RANK = 16
GLA_TAU = 16.0
GLA_CHUNK = 64
GLA_SUB = 16
GLA_STAGES = 3
SG_GROUPS = 4
SG_CH = 128
SG_CHUNK = 128
FT_GROUPS = 4
FT_CH = 128
N_EXPERTS = 32
TOP_K = 4
SWIGLU_LIMIT = 7.0
SWIGLU_ALPHA = 1.702
MOE_BLOCK = 256

LOG2E = math.log2(math.e)
LANES = 128
ROW_TILE = 256
VMEM_LIMIT = 56 * 1024 * 1024

W_Q = GLA_HEADS * GLA_DK
W_V = GLA_HEADS * GLA_DV
W_SG = SG_GROUPS * SG_CH
W_FT = FT_GROUPS * FT_CH


def _cparams(*sem):
    return pltpu.CompilerParams(dimension_semantics=sem, vmem_limit_bytes=VMEM_LIMIT)


def _sigmoid(x):
    return 1.0 / (1.0 + jnp.exp(-x))


def _gelu_tanh(x):
    return 0.5 * x * (1.0 + jnp.tanh(math.sqrt(2.0 / math.pi) * (x + 0.044715 * (x * x * x))))


def _dot(a, b):
    return jnp.dot(a, b, preferred_element_type=F32)


def _mod_row(tile, n_ctx_tiles, tiles_per_batch, ctx_row):
    return jnp.where(tile < n_ctx_tiles, ctx_row, (tile - n_ctx_tiles) // tiles_per_batch)


def _mod_kernel(c_ref, w_ref, b_ref, o_ref):
    c = c_ref[...]
    s = (c * _sigmoid(c)).astype(BF16)
    o_ref[0] = _dot(s, w_ref[0].astype(BF16)) + b_ref[0]


def _mod_vectors(cond, w_mod, b_mod):
    depth, d, n = w_mod.shape
    tn = 1024
    return pl.pallas_call(
        _mod_kernel,
        grid=(depth, n // tn),
        in_specs=[
            pl.BlockSpec((8, d), lambda l, j: (0, 0)),
            pl.BlockSpec((1, d, tn), lambda l, j: (l, 0, j)),
            pl.BlockSpec((1, 1, tn), lambda l, j: (l, 0, j)),
        ],
        out_specs=pl.BlockSpec((1, 8, tn), lambda l, j: (l, 0, j)),
        out_shape=jax.ShapeDtypeStruct((depth, 8, n), F32),
        compiler_params=_cparams("parallel", "parallel"),
        name="mod_vectors",
    )(cond, w_mod, b_mod.reshape(depth, 1, n))


def _proj_kernel(x_ref, sh_ref, sc_ref, g_ref, w_ref, wa_ref, wa2_ref, ba_ref,
                 z_ref, la_ref, a_scr, *, n_ctx_tiles, tiles_per_batch, ctx_row):
    i = pl.program_id(0)
    j = pl.program_id(1)

    @pl.when(j == 0)
    def _():
        row = _mod_row(i, n_ctx_tiles, tiles_per_batch, ctx_row)
        scale = 1.0 + sc_ref[pl.ds(row, 1), :]
        shift = sh_ref[pl.ds(row, 1), :]

        def body(r, carry):
            rows = pl.ds(pl.multiple_of(r * ROW_TILE, ROW_TILE), ROW_TILE)
            x = x_ref[rows, :]
            y = x * lax.rsqrt(jnp.mean(x * x, axis=-1, keepdims=True) + EPS) * g_ref[...]
            hb = (y * scale + shift).astype(BF16)
            a_scr[rows, :] = hb
            a_lr = _dot(hb, wa_ref[...])
            pre = _dot(a_lr.astype(BF16), wa2_ref[...]) + ba_ref[...]
            la_ref[rows, :] = -(jnp.maximum(-pre, 0.0) + jnp.log1p(jnp.exp(-jnp.abs(pre)))) * (1.0 / GLA_TAU)
            return carry

        lax.fori_loop(0, x_ref.shape[0] // ROW_TILE, body, 0)

    z_ref[...] = _dot(a_scr[...], w_ref[...])


def _project(x, mod, g, w_main, w_a, w_a2, b_a, *, tm, tn, n_ctx_tiles, tiles_per_batch, ctx_row):
    t, d = x.shape
    n = w_main.shape[1]
    kern = functools.partial(_proj_kernel, n_ctx_tiles=n_ctx_tiles,
                             tiles_per_batch=tiles_per_batch, ctx_row=ctx_row)
    return pl.pallas_call(
        kern,
        grid=(t // tm, n // tn),
        in_specs=[
            pl.BlockSpec((tm, d), lambda i, j: (i, 0)),
            pl.BlockSpec((8, d), lambda i, j: (0, 0)),
            pl.BlockSpec((8, d), lambda i, j: (0, 1)),
            pl.BlockSpec((1, d), lambda i, j: (0, 0)),
            pl.BlockSpec((d, tn), lambda i, j: (0, j)),
            pl.BlockSpec((d, LANES), lambda i, j: (0, 0)),
            pl.BlockSpec((LANES, 2 * W_Q), lambda i, j: (0, 0)),
            pl.BlockSpec((1, 2 * W_Q), lambda i, j: (0, 0)),
        ],
        out_specs=[
            pl.BlockSpec((tm, tn), lambda i, j: (i, j)),
            pl.BlockSpec((tm, 2 * W_Q), lambda i, j: (i, 0)),
        ],
        out_shape=[
            jax.ShapeDtypeStruct((t, n), F32),
            jax.ShapeDtypeStruct((t, 2 * W_Q), F32),
        ],
        scratch_shapes=[pltpu.VMEM((tm, d), BF16)],
        compiler_params=_cparams("parallel", "arbitrary"),
        name="norm_mod_project",
    )(x, mod, mod, g, w_main, w_a, w_a2, b_a)


def _gla_pairs(rev):
    c, sub = GLA_CHUNK, GLA_SUB
    half = c // 2
    if rev:
        return ((0, half, half, c, half - 1),
                (0, sub, sub, half, sub - 1),
                (half, half + sub, half + sub, c, half + sub - 1))
    return ((half, c, 0, half, half),
            (sub, half, 0, sub, sub),
            (half + sub, c, half, half + sub, half + sub))


def _place_rows(x, r0, n):
    parts = []
    if r0:
        parts.append(jnp.zeros((r0, x.shape[1]), x.dtype))
    parts.append(x)
    rest = n - r0 - x.shape[0]
    if rest:
        parts.append(jnp.zeros((rest, x.shape[1]), x.dtype))
    return jnp.concatenate(parts, axis=0)


def _gla_chunk(q_ref, k_ref, v_ref, la_ref, e_ref, o_ref, s_ref, cum_ref, kc_ref, r0, h, rev):
    c, sub = GLA_CHUNK, GLA_SUB
    nsub = c // sub
    rows = pl.ds(r0, c)
    kcols = slice(h * GLA_DK, (h + 1) * GLA_DK)
    vcols = slice(h * GLA_DV, (h + 1) * GLA_DV)
    q = q_ref[rows, kcols] * (GLA_DK ** -0.5)
    k = k_ref[rows, kcols]
    v = v_ref[rows, vcols].astype(BF16)
    la = la_ref[rows, kcols]

    ri = lax.broadcasted_iota(jnp.int32, (c, c), 0)
    ci = lax.broadcasted_iota(jnp.int32, (c, c), 1)
    before = (ci >= ri) if rev else (ci <= ri)
    tri = jnp.where(before, 1.0, 0.0).astype(BF16)
    la_hi = la.astype(BF16)
    rem = la - la_hi.astype(F32)
    la_mid = rem.astype(BF16)
    la_lo = (rem - la_mid.astype(F32)).astype(BF16)
    cum = (_dot(tri, la_hi) + _dot(tri, la_mid) + _dot(tri, la_lo)) * LOG2E
    cum_ref[...] = cum
    kc_ref[...] = k
    yield
    last = 0 if rev else c - 1
    total = cum[last:last + 1, :]

    s = s_ref[...]
    o = _dot((q * jnp.exp2(cum)).astype(BF16), s.astype(BF16))

    qparts, kparts = [], []
    for l0, l1, e0, e1, ref in _gla_pairs(rev):
        cr = cum[ref:ref + 1, :]
        qparts.append(_place_rows(q[l0:l1] * jnp.exp2(cum[l0:l1] - cr), l0, c))
        kparts.append(_place_rows(k[e0:e1] * jnp.exp2(cr - cum[e0:e1]), e0, c))
    s_off = lax.dot_general(jnp.concatenate(qparts, axis=1).astype(BF16),
                            jnp.concatenate(kparts, axis=1).astype(BF16),
                            (((1,), (1,)), ((), ())), preferred_element_type=F32)

    parts = []
    for jj in range(sub):
        cj = jnp.concatenate(
            [jnp.broadcast_to(cum_ref[a * sub + jj:a * sub + jj + 1, :], (sub, GLA_DK)) for a in range(nsub)],
            axis=0)
        kj = jnp.concatenate(
            [jnp.broadcast_to(kc_ref[a * sub + jj:a * sub + jj + 1, :], (sub, GLA_DK)) for a in range(nsub)],
            axis=0)
        parts.append((q * jnp.exp2(jnp.minimum(cum - cj, 0.0)) * kj).astype(BF16))
    diag = _dot(jnp.concatenate(parts, axis=1), e_ref[...])
    yield

    lag = (ci - ri) if rev else (ri - ci)
    room = (sub - 1 - ri % sub) if rev else (ri % sub)
    in_diag = lag.astype(jnp.uint32) <= room.astype(jnp.uint32)
    p = jnp.where(in_diag, diag, 0.0) + s_off
    o_ref[rows, vcols] = o + _dot(p.astype(BF16), v)

    k_out = (k * jnp.exp2(total - cum)).astype(BF16)
    upd = lax.dot_general(k_out, v, (((0,), (0,)), ((), ())), preferred_element_type=F32)
    dmat = jnp.broadcast_to(jnp.exp2(total), (GLA_DK, GLA_DK)).T
    for half in range(GLA_DV // GLA_DK):
        cols = slice(half * GLA_DK, (half + 1) * GLA_DK)
        s_ref[:, cols] = s[:, cols] * dmat + upd[:, cols]


def _gla_kernel(qf, kf, vf, laf, qb, kb, vb, lab, e_ref, of_ref, ob_ref, *scratch):
    n = 2 * GLA_HEADS
    states, cums, keys = scratch[:n], scratch[n:2 * n], scratch[2 * n:]

    @pl.when(pl.program_id(1) == 0)
    def _():
        for s_ref in states:
            s_ref[...] = jnp.zeros_like(s_ref)

    nchunk = ROW_TILE // GLA_CHUNK

    def body(c, carry):
        r_f = pl.multiple_of(c * GLA_CHUNK, GLA_CHUNK)
        r_b = pl.multiple_of((nchunk - 1 - c) * GLA_CHUNK, GLA_CHUNK)
        units = []
        for h in range(GLA_HEADS):
            f, b = h, GLA_HEADS + h
            units.append(_gla_chunk(qf, kf, vf, laf, e_ref, of_ref, states[f], cums[f], keys[f], r_f, h, False))
            units.append(_gla_chunk(qb, kb, vb, lab, e_ref, ob_ref, states[b], cums[b], keys[b], r_b, h, True))
        for _ in range(GLA_STAGES):
            for u in units:
                next(u, None)
        return carry

    lax.fori_loop(0, nchunk, body, 0)


def _gla(z, la, *, batch, lc, l, col_q):
    t_rows = z.shape[0]
    nctx = lc // ROW_TILE
    nlat = l // ROW_TILE
    steps = nctx + nlat
    qb0 = col_q // W_Q
    kb0 = (col_q + W_Q) // W_Q
    vb0 = (col_q + 2 * W_Q) // W_V

    def fwd_tile(b, t):
        return jnp.where(t < nctx, b * nctx + t, batch * nctx + b * nlat + (t - nctx))

    def bwd_tile(b, t):
        return jnp.where(t < nctx, b * nctx + (nctx - 1 - t), batch * nctx + b * nlat + (steps - 1 - t))

    def spec(width, tile_fn, col):
        return pl.BlockSpec((ROW_TILE, width), lambda b, t: (tile_fn(b, t), col))

    kk = jnp.arange(GLA_SUB * GLA_DK, dtype=jnp.int32)[:, None] // GLA_DK
    cc = jnp.arange(GLA_CHUNK, dtype=jnp.int32)[None, :] % GLA_SUB
    e_mat = (kk == cc).astype(BF16)

    o_shape = jax.ShapeDtypeStruct((t_rows, W_V), F32)
    n_units = 2 * GLA_HEADS
    state = pltpu.VMEM((GLA_DK, GLA_DV), F32)
    chunk = pltpu.VMEM((GLA_CHUNK, GLA_DK), F32)
    return pl.pallas_call(
        _gla_kernel,
        grid=(batch, steps),
        in_specs=[
            spec(W_Q, fwd_tile, qb0), spec(W_Q, fwd_tile, kb0), spec(W_V, fwd_tile, vb0), spec(W_Q, fwd_tile, 0),
            spec(W_Q, bwd_tile, qb0), spec(W_Q, bwd_tile, kb0), spec(W_V, bwd_tile, vb0), spec(W_Q, bwd_tile, 1),
            pl.BlockSpec((GLA_SUB * GLA_DK, GLA_CHUNK), lambda b, t: (0, 0)),
        ],
        out_specs=[spec(W_V, fwd_tile, 0), spec(W_V, bwd_tile, 0)],
        out_shape=[o_shape, o_shape],
        scratch_shapes=[state] * n_units + [chunk] * (2 * n_units),
        compiler_params=_cparams("parallel", "arbitrary"),
        name="gla_scan",
    )(z, z, z, la, z, z, z, la, e_mat)


def _dft_tables(n):
    idx = jnp.arange(n, dtype=jnp.int32)
    ang = ((idx[:, None] * idx[None, :]) % n).astype(F32) * (2.0 * math.pi / n)
    scale = n ** -0.5
    return (jnp.cos(ang) * scale).astype(BF16), (jnp.sin(ang) * scale).astype(BF16)


def _ft_chan_kernel(f_ref, cs_ref, xc_ref, xs_ref):
    for g in range(FT_GROUPS):
        cols = slice(g * FT_CH, (g + 1) * FT_CH)
        y = _dot(f_ref[:, cols].astype(BF16), cs_ref[...])
        xc_ref[:, cols] = y[:, :FT_CH].astype(BF16)
        xs_ref[:, cols] = y[:, FT_CH:].astype(BF16)


def _ft_channel(z, cs, *, col_f):
    t_rows = z.shape[0]
    tm = 1024 if t_rows % 1024 == 0 else ROW_TILE
    shape = jax.ShapeDtypeStruct((t_rows, W_FT), BF16)
    return pl.pallas_call(
        _ft_chan_kernel,
        grid=(t_rows // tm,),
        in_specs=[pl.BlockSpec((tm, W_FT), lambda i: (i, col_f // W_FT)),
                  pl.BlockSpec((FT_CH, 2 * FT_CH), lambda i: (0, 0))],
        out_specs=[pl.BlockSpec((tm, W_FT), lambda i: (i, 0))] * 2,
        out_shape=[shape, shape],
        compiler_params=_cparams("parallel"),
        name="fourier_channels",
    )(z, cs)


def _ft_pos_kernel(c_ref, s_ref, xc_ref, xs_ref, o_ref):
    o_ref[...] = _dot(c_ref[...], xc_ref[...]) - _dot(s_ref[...], xs_ref[...])


def _ft_positions(cos_t, sin_t, xc, xs, *, batch, seq):
    nt = seq // ROW_TILE
    x_spec = pl.BlockSpec((seq, W_FT), lambda b, j: (b, 0))
    t_spec = pl.BlockSpec((ROW_TILE, seq), lambda b, j: (j, 0))
    return pl.pallas_call(
        _ft_pos_kernel,
        grid=(batch, nt),
        in_specs=[t_spec, t_spec, x_spec, x_spec],
        out_specs=pl.BlockSpec((ROW_TILE, W_FT), lambda b, j: (b * nt + j, 0)),
        out_shape=jax.ShapeDtypeStruct((batch * seq, W_FT), F32),
        compiler_params=_cparams("parallel", "parallel"),
        name="fourier_positions",
    )(cos_t, sin_t, xc, xs)


def _branch_kernel(gg_ref, gs_ref, gf_ref, of_ref, ob_ref, r_ref, su_ref, sv_ref, yft_ref,
                   gn_ref, ws_ref, bs_ref, pg_ref, ps_ref, pf_ref, m_ref):
    o = of_ref[...] + ob_ref[...]
    heads = []
    for h in range(GLA_HEADS):
        oh = o[:, h * GLA_DV:(h + 1) * GLA_DV]
        heads.append(oh * lax.rsqrt(jnp.mean(oh * oh, axis=-1, keepdims=True) + EPS))
    r = r_ref[...]
    y_gla = (jnp.concatenate(heads, axis=-1) * gn_ref[...]) * (r * _sigmoid(r))
    acc = _sigmoid(gg_ref[...]) * _dot(y_gla.astype(BF16), pg_ref[...])

    u = _gelu_tanh(su_ref[...])
    v = _gelu_tanh(sv_ref[...])
    groups = []
    for g in range(SG_GROUPS):
        vg = v[:, g * SG_CH:(g + 1) * SG_CH]
        mu = jnp.mean(vg, axis=-1, keepdims=True)
        var = jnp.mean(jnp.square(vg - mu), axis=-1, keepdims=True)
        vn = ((vg - mu) * lax.rsqrt(var + EPS)).astype(BF16)
        chunks = []
        for c in range(ROW_TILE // SG_CHUNK):
            chunks.append(_dot(ws_ref[g], vn[c * SG_CHUNK:(c + 1) * SG_CHUNK, :]) + bs_ref[g])
        groups.append(jnp.concatenate(chunks, axis=0))
    y_sg = u * jnp.concatenate(groups, axis=-1)
    acc = acc + _sigmoid(gs_ref[...]) * _dot(y_sg.astype(BF16), ps_ref[...])

    acc = acc + _sigmoid(gf_ref[...]) * _dot(yft_ref[...].astype(BF16), pf_ref[...])
    m_ref[...] = acc.astype(BF16)


def _branches(z, o_f, o_b, y_ft, gla_norm, w_s, b_s, p_gla, p_sg, p_ft, *, cols):
    t_rows = z.shape[0]
    d = p_gla.shape[1]
    tm = ROW_TILE

    def zspec(width, col0):
        return pl.BlockSpec((tm, width), lambda i: (i, col0 // width))

    def rows(width):
        return pl.BlockSpec((tm, width), lambda i: (i, 0))

    def const(shape):
        return pl.BlockSpec(shape, lambda i: (0,) * len(shape))

    return pl.pallas_call(
        _branch_kernel,
        grid=(t_rows // tm,),
        in_specs=[
            zspec(d, cols["gates"]), zspec(d, cols["gates"] + d), zspec(d, cols["gates"] + 2 * d),
            rows(W_V), rows(W_V), zspec(W_V, cols["r"]), zspec(W_SG, cols["su"]), zspec(W_SG, cols["sv"]),
            rows(W_FT),
            const((1, W_V)), const((SG_GROUPS, SG_CHUNK, SG_CHUNK)), const((SG_GROUPS, SG_CHUNK, SG_CH)),
            const((W_V, d)), const((W_SG, d)), const((W_FT, d)),
        ],
        out_specs=rows(d),
        out_shape=jax.ShapeDtypeStruct((t_rows, d), BF16),
        compiler_params=_cparams("parallel"),
        name="branch_merge",
    )(z, z, z, o_f, o_b, z, z, z, y_ft, gla_norm, w_s, b_s, p_gla, p_sg, p_ft)


def _outproj_kernel(m_ref, x_ref, w_ref, g1_ref, n2_ref, sh_ref, sc_ref, wr_ref, br_ref,
                    xo_ref, h_ref, lg_ref, *, n_ctx_tiles, tiles_per_batch, ctx_row):
    row = _mod_row(pl.program_id(0), n_ctx_tiles, tiles_per_batch, ctx_row)
    x = x_ref[...] + g1_ref[pl.ds(row, 1), :] * _dot(m_ref[...], w_ref[...])
    xo_ref[...] = x
    y = x * lax.rsqrt(jnp.mean(x * x, axis=-1, keepdims=True) + EPS) * n2_ref[...]
    h = y * (1.0 + sc_ref[pl.ds(row, 1), :]) + sh_ref[pl.ds(row, 1), :]
    h_hi = h.astype(BF16)
    h_ref[...] = h_hi
    h_lo = (h - h_hi.astype(F32)).astype(BF16)
    w = wr_ref[...]
    w_hi = w.astype(BF16)
    w_lo = (w - w_hi.astype(F32)).astype(BF16)
    lg_ref[...] = _dot(h_hi, w_hi) + (_dot(h_hi, w_lo) + _dot(h_lo, w_hi)) + br_ref[...]


def _out_project(merged, x, w_out, mod, norm2, w_router, b_router, *, n_ctx_tiles, tiles_per_batch, ctx_row):
    t_rows, d = x.shape
    tm = ROW_TILE
    kern = functools.partial(_outproj_kernel, n_ctx_tiles=n_ctx_tiles,
                             tiles_per_batch=tiles_per_batch, ctx_row=ctx_row)

    def rows(width):
        return pl.BlockSpec((tm, width), lambda i: (i, 0))

    def modspec(k):
        return pl.BlockSpec((8, d), lambda i: (0, k))

    return pl.pallas_call(
        kern,
        grid=(t_rows // tm,),
        in_specs=[rows(d), rows(d), pl.BlockSpec((d, d), lambda i: (0, 0)),
                  modspec(2), pl.BlockSpec((1, d), lambda i: (0, 0)), modspec(3), modspec(4),
                  pl.BlockSpec((d, LANES), lambda i: (0, 0)), pl.BlockSpec((1, LANES), lambda i: (0, 0))],
        out_specs=[rows(d), rows(d), rows(LANES)],
        out_shape=[jax.ShapeDtypeStruct((t_rows, d), F32), jax.ShapeDtypeStruct((t_rows, d), BF16),
                   jax.ShapeDtypeStruct((t_rows, LANES), F32)],
        compiler_params=_cparams("parallel"),
        name="out_project",
    )(merged, x, w_out, mod, norm2, mod, mod, w_router, b_router)


def _expert_kernel(be_ref, nu_ref, x_ref, wgu_ref, bgu_ref, wd_ref, bd_ref, y_ref):
    @pl.when(pl.program_id(0) < nu_ref[0])
    def _():
        gu = _dot(x_ref[...], wgu_ref[0]) + bgu_ref[0]
        de = gu.shape[1] // 2
        gate = jnp.minimum(gu[:, :de], SWIGLU_LIMIT)
        up = jnp.clip(gu[:, de:], -SWIGLU_LIMIT, SWIGLU_LIMIT)
        hdn = (up + 1.0) * (gate * _sigmoid(SWIGLU_ALPHA * gate))
        y_ref[...] = _dot(hdn.astype(BF16), wd_ref[0]) + bd_ref[0]


def _experts(xs, blk_e, n_used, w_gu, b_gu, w_down, b_down):
    rows_total, d = xs.shape
    nb = rows_total // MOE_BLOCK
    ne, _, n_gu = w_gu.shape
    de = w_down.shape[1]

    def blk(i, be, nu):
        return jnp.minimum(i, nu[0] - 1)

    grid_spec = pltpu.PrefetchScalarGridSpec(
        num_scalar_prefetch=2,
        grid=(nb,),
        in_specs=[
            pl.BlockSpec((MOE_BLOCK, d), lambda i, be, nu: (blk(i, be, nu), 0)),
            pl.BlockSpec((1, d, n_gu), lambda i, be, nu: (be[blk(i, be, nu)], 0, 0)),
            pl.BlockSpec((1, 1, n_gu), lambda i, be, nu: (be[blk(i, be, nu)], 0, 0)),
            pl.BlockSpec((1, de, d), lambda i, be, nu: (be[blk(i, be, nu)], 0, 0)),
            pl.BlockSpec((1, 1, d), lambda i, be, nu: (be[blk(i, be, nu)], 0, 0)),
        ],
        out_specs=pl.BlockSpec((MOE_BLOCK, d), lambda i, be, nu: (blk(i, be, nu), 0)),
    )
    return pl.pallas_call(
        _expert_kernel,
        grid_spec=grid_spec,
        out_shape=jax.ShapeDtypeStruct((rows_total, d), F32),
        compiler_params=_cparams("arbitrary"),
        name="moe_experts",
    )(blk_e, n_used, xs, w_gu, b_gu.reshape(ne, 1, n_gu), w_down, b_down.reshape(ne, 1, d))


def _route(logits, n_tok):
    top_val, top_idx = lax.top_k(logits, TOP_K)
    probs = jax.nn.softmax(top_val, axis=-1)
    m = n_tok * TOP_K
    flat_e = top_idx.reshape(m).astype(jnp.int32)
    onehot = (flat_e[:, None] == jnp.arange(N_EXPERTS, dtype=jnp.int32)[None, :]).astype(jnp.int32)
    csum = jnp.cumsum(onehot, axis=0)
    counts = csum[-1]
    rank = jnp.sum(csum * onehot, axis=1) - 1
    pad_counts = (counts + MOE_BLOCK - 1) // MOE_BLOCK * MOE_BLOCK
    pad_end = jnp.cumsum(pad_counts)
    pad_start = pad_end - pad_counts
    pos = pad_start[flat_e] + rank
    nb = (m + MOE_BLOCK - 1) // MOE_BLOCK + N_EXPERTS
    flat_tok = jnp.arange(m, dtype=jnp.int32) // TOP_K
    row_tok = jnp.zeros((nb * MOE_BLOCK,), jnp.int32).at[pos].set(flat_tok)
    blk_e = jnp.minimum(
        jnp.searchsorted(pad_end, jnp.arange(nb, dtype=jnp.int32) * MOE_BLOCK, side="right"),
        N_EXPERTS - 1).astype(jnp.int32)
    n_used = (pad_end[-1] // MOE_BLOCK).astype(jnp.int32).reshape(1)
    return probs, pos.reshape(n_tok, TOP_K), row_tok, blk_e, n_used


def _moe(h2, logits, w_gu, b_gu, w_down, b_down):
    n_tok = h2.shape[0]
    probs, pos, row_tok, blk_e, n_used = _route(logits[:, :N_EXPERTS], n_tok)
    xs = jnp.take(h2, row_tok, axis=0)
    yb = _experts(xs, blk_e, n_used, w_gu, b_gu, w_down, b_down)
    y = jnp.take(yb, pos.T.reshape(-1), axis=0).reshape(TOP_K, n_tok, -1)
    return y, probs


def _combine_kernel(x_ref, y0_ref, y1_ref, y2_ref, y3_ref, p_ref, g2_ref, nf_ref, o_ref,
                    *, tile0, final, n_ctx_tiles, tiles_per_batch, ctx_row):
    row = _mod_row(pl.program_id(0) + tile0, n_ctx_tiles, tiles_per_batch, ctx_row)
    p = p_ref[...]
    y = (p[:, 0:1] * y0_ref[0] + p[:, 1:2] * y1_ref[0]) + (p[:, 2:3] * y2_ref[0] + p[:, 3:4] * y3_ref[0])
    x = x_ref[...] + g2_ref[pl.ds(row, 1), :] * y
    if final:
        x = x * lax.rsqrt(jnp.mean(x * x, axis=-1, keepdims=True) + EPS) * nf_ref[...]
    o_ref[...] = x


def _combine(x, y, probs, mod, norm_f, *, tile0, final, n_ctx_tiles, tiles_per_batch, ctx_row):
    assert y.shape[0] == TOP_K == 4
    _, n_rows, d = y.shape
    tm = ROW_TILE
    kern = functools.partial(_combine_kernel, tile0=tile0, final=final, n_ctx_tiles=n_ctx_tiles,
                             tiles_per_batch=tiles_per_batch, ctx_row=ctx_row)

    def yspec(k):
        return pl.BlockSpec((1, tm, d), lambda i: (k, i, 0))

    return pl.pallas_call(
        kern,
        grid=(n_rows // tm,),
        in_specs=[pl.BlockSpec((tm, d), lambda i: (tile0 + i, 0)),
                  yspec(0), yspec(1), yspec(2), yspec(3),
                  pl.BlockSpec((tm, TOP_K), lambda i: (i, 0)),
                  pl.BlockSpec((8, d), lambda i: (0, 5)),
                  pl.BlockSpec((1, d), lambda i: (0, 0))],
        out_specs=pl.BlockSpec((tm, d), lambda i: (i, 0)),
        out_shape=jax.ShapeDtypeStruct((n_rows, d), F32),
        compiler_params=_cparams("parallel"),
        name="moe_combine",
    )(x, y, y, y, y, probs, mod, norm_f)


def _cast_kernel(x_ref, o_ref):
    o_ref[...] = x_ref[...].astype(BF16)


def _to_bf16(w):
    shape = w.shape
    k, n = shape[-2:]
    w3 = w.reshape(-1, k, n)
    tk = 1024
    out = pl.pallas_call(
        _cast_kernel,
        grid=(w3.shape[0], k // tk),
        in_specs=[pl.BlockSpec((1, tk, n), lambda e, j: (e, j, 0))],
        out_specs=pl.BlockSpec((1, tk, n), lambda e, j: (e, j, 0)),
        out_shape=jax.ShapeDtypeStruct(w3.shape, BF16),
        compiler_params=_cparams("parallel", "parallel"),
        name="cast_bf16",
    )(w3)
    return out.reshape(shape)


def _grid_pos_embed(n, d):
    rows = n // GRID_W
    row = jnp.broadcast_to(jnp.arange(rows, dtype=F32)[:, None], (rows, GRID_W)).reshape(-1)
    col = jnp.broadcast_to(jnp.arange(GRID_W, dtype=F32)[None, :], (rows, GRID_W)).reshape(-1)
    quarter = d // 4
    omega = 1.0 / (10000.0 ** (jnp.arange(quarter, dtype=F32) / quarter))

    def enc(p):
        a = p[:, None] * omega
        return jnp.concatenate([jnp.sin(a), jnp.cos(a)], axis=-1)

    return jnp.concatenate([enc(row), enc(col)], axis=-1)


def kernel(x, c, ctx, c_ctx, w_mod, b_mod, norm1, w_in, w_a2, b_a, gla_norm, w_s, b_s, p_gla, p_sg, p_ft,
           w_out, norm2, w_router, b_router, w_gu, b_gu, w_down, b_down, norm_f):
    batch, l, d = x.shape
    lc = ctx.shape[1]
    depth = w_mod.shape[0]
    n_ctx_rows = batch * lc
    t_rows = n_ctx_rows + batch * l
    assert lc % ROW_TILE == 0 and l % ROW_TILE == 0 and batch < 8

    tm1 = math.gcd(math.gcd(n_ctx_rows, l), 1024)
    assert tm1 % ROW_TILE == 0
    proj_tiles = dict(n_ctx_tiles=n_ctx_rows // tm1, tiles_per_batch=l // tm1, ctx_row=batch)
    row_tiles = dict(n_ctx_tiles=n_ctx_rows // ROW_TILE, tiles_per_batch=l // ROW_TILE, ctx_row=batch)

    o_q = 0
    o_k = o_q + W_Q
    o_v = o_k + W_Q
    o_r = o_v + W_V
    o_a = o_r + W_V
    o_su = o_a + 2 * GLA_RANK
    o_sv = o_su + W_SG
    o_f = o_sv + W_SG
    o_g = o_f + W_FT
    n_gates = 3 * d
    cols = dict(gates=0, q=n_gates, r=n_gates + 2 * W_Q + W_V, su=n_gates + 2 * W_Q + 2 * W_V)
    cols["sv"] = cols["su"] + W_SG
    cols["f"] = cols["sv"] + W_SG
    n_main = cols["f"] + W_FT

    xt = jnp.concatenate([ctx.reshape(n_ctx_rows, d),
                          (x + _grid_pos_embed(l, d)[None]).reshape(batch * l, d)], axis=0)

    cond = jnp.zeros((8, d), F32).at[:batch].set(c).at[batch].set(c_ctx)
    mods = _mod_vectors(cond, w_mod, b_mod)

    cc, sc = _dft_tables(FT_CH)
    chan_tab = jnp.concatenate([cc, sc], axis=1)
    cos_l, sin_l = _dft_tables(l)
    cos_c, sin_c = _dft_tables(lc)

    wgu_all = _to_bf16(w_gu)
    wdn_all = _to_bf16(w_down)

    for i in range(depth):
        last = i == depth - 1
        wi = w_in[i]
        w_main = jnp.concatenate(
            [wi[:, o_g:], wi[:, o_q:o_a], wi[:, o_su:o_g]], axis=1).astype(BF16)
        w_a = jnp.pad(wi[:, o_a:o_su], ((0, 0), (0, LANES - 2 * GLA_RANK))).astype(BF16)
        wa2 = jnp.zeros((LANES, 2 * W_Q), F32)
        wa2 = wa2.at[:GLA_RANK, :W_Q].set(w_a2[i, 0]).at[GLA_RANK:2 * GLA_RANK, W_Q:].set(w_a2[i, 1])
        mod = mods[i]

        z, la = _project(xt, mod, norm1[i][None], w_main, w_a, wa2.astype(BF16), b_a[i].reshape(1, 2 * W_Q),
                         tm=tm1, tn=768, **proj_tiles)
        assert z.shape[1] == n_main

        o_fw, o_bw = _gla(z, la, batch=batch, lc=lc, l=l, col_q=cols["q"])

        xc, xs = _ft_channel(z, chan_tab, col_f=cols["f"])
        y_ft = jnp.concatenate([
            _ft_positions(cos_c, sin_c, xc[:n_ctx_rows], xs[:n_ctx_rows], batch=batch, seq=lc),
            _ft_positions(cos_l, sin_l, xc[n_ctx_rows:], xs[n_ctx_rows:], batch=batch, seq=l),
        ], axis=0)

        bs_b = jnp.broadcast_to(b_s[i][:, :, None], (SG_GROUPS, SG_CHUNK, SG_CH))
        merged = _branches(z, o_fw, o_bw, y_ft, gla_norm[i][None], w_s[i].astype(BF16), bs_b,
                           p_gla[i].astype(BF16), p_sg[i].astype(BF16), p_ft[i].astype(BF16), cols=cols)

        w_r = jnp.pad(w_router[i], ((0, 0), (0, LANES - N_EXPERTS)))
        b_r = jnp.pad(b_router[i], (0, LANES - N_EXPERTS)).reshape(1, LANES)
        xt, h2, logits = _out_project(merged, xt, w_out[i].astype(BF16), mod, norm2[i][None], w_r, b_r,
                                      **row_tiles)

        row0 = n_ctx_rows if last else 0
        y, probs = _moe(h2[row0:], logits[row0:], wgu_all[i], b_gu[i], wdn_all[i], b_down[i])
        xt = _combine(xt, y, probs, mod, norm_f[None], tile0=row0 // ROW_TILE, final=last, **row_tiles)

    return xt.reshape(batch, l, d)
```

```python
import functools
import math

import jax
import jax.numpy as jnp
from jax import lax
from jax.experimental import pallas as pl
from jax.experimental.pallas import tpu as pltpu

F32 = jnp.float32
BF16 = jnp.bfloat16

EPS = 1e-6
GRID_W = 64
GLA_HEADS = 4
GLA_DK = 128
GLA_DV = 256
GLA_RANK = 16
GLA_TAU = 16.0
GLA_CHUNK = 64
GLA_SUB = 16
GLA_STAGES = 3
SG_GROUPS = 4
SG_CH = 128
SG_CHUNK = 128
FT_GROUPS = 4
FT_CH = 128
N_EXPERTS = 32
TOP_K = 4
SWIGLU_LIMIT = 7.0
SWIGLU_ALPHA = 1.702
MOE_BLOCK = 256

LOG2E = math.log2(math.e)
LANES = 128
ROW_TILE = 256
VMEM_LIMIT = 56 * 1024 * 1024

W_Q = GLA_HEADS * GLA_DK
W_V = GLA_HEADS * GLA_DV
W_SG = SG_GROUPS * SG_CH
W_FT = FT_GROUPS * FT_CH


def _cparams(*sem):
    return pltpu.CompilerParams(dimension_semantics=sem, vmem_limit_bytes=VMEM_LIMIT)


def _sigmoid(x):
    return 1.0 / (1.0 + jnp.exp(-x))


def _gelu_tanh(x):
    return 0.5 * x * (1.0 + jnp.tanh(math.sqrt(2.0 / math.pi) * (x + 0.044715 * (x * x * x))))


def _dot(a, b):
    return jnp.dot(a, b, preferred_element_type=F32)


def _mod_row(tile, n_ctx_tiles, tiles_per_batch, ctx_row):
    return jnp.where(tile < n_ctx_tiles, ctx_row, (tile - n_ctx_tiles) // tiles_per_batch)


def _mod_kernel(c_ref, w_ref, b_ref, o_ref):
    c = c_ref[...]
    s = (c * _sigmoid(c)).astype(BF16)
    o_ref[0] = _dot(s, w_ref[0].astype(BF16)) + b_ref[0]


def _mod_vectors(cond, w_mod, b_mod):
    depth, d, n = w_mod.shape
    tn = 1024
    return pl.pallas_call(
        _mod_kernel,
        grid=(depth, n // tn),
        in_specs=[
            pl.BlockSpec((8, d), lambda l, j: (0, 0)),
            pl.BlockSpec((1, d, tn), lambda l, j: (l, 0, j)),
            pl.BlockSpec((1, 1, tn), lambda l, j: (l, 0, j)),
        ],
        out_specs=pl.BlockSpec((1, 8, tn), lambda l, j: (l, 0, j)),
        out_shape=jax.ShapeDtypeStruct((depth, 8, n), F32),
        compiler_params=_cparams("parallel", "parallel"),
        name="mod_vectors",
    )(cond, w_mod, b_mod.reshape(depth, 1, n))


def _proj_kernel(x_ref, sh_ref, sc_ref, g_ref, w_ref, wa_ref, wa2_ref, ba_ref,
                 z_ref, la_ref, a_scr, *, n_ctx_tiles, tiles_per_batch, ctx_row):
    i = pl.program_id(0)
    j = pl.program_id(1)

    @pl.when(j == 0)
    def _():
        row = _mod_row(i, n_ctx_tiles, tiles_per_batch, ctx_row)
        scale = 1.0 + sc_ref[pl.ds(row, 1), :]
        shift = sh_ref[pl.ds(row, 1), :]

        def body(r, carry):
            rows = pl.ds(pl.multiple_of(r * ROW_TILE, ROW_TILE), ROW_TILE)
            x = x_ref[rows, :]
            y = x * lax.rsqrt(jnp.mean(x * x, axis=-1, keepdims=True) + EPS) * g_ref[...]
            hb = (y * scale + shift).astype(BF16)
            a_scr[rows, :] = hb
            a_lr = _dot(hb, wa_ref[...])
            pre = _dot(a_lr.astype(BF16), wa2_ref[...]) + ba_ref[...]
            la_ref[rows, :] = -(jnp.maximum(-pre, 0.0) + jnp.log1p(jnp.exp(-jnp.abs(pre)))) * (1.0 / GLA_TAU)
            return carry

        lax.fori_loop(0, x_ref.shape[0] // ROW_TILE, body, 0)

    z_ref[...] = _dot(a_scr[...], w_ref[...])


def _project(x, mod, g, w_main, w_a, w_a2, b_a, *, tm, tn, n_ctx_tiles, tiles_per_batch, ctx_row):
    t, d = x.shape
    n = w_main.shape[1]
    kern = functools.partial(_proj_kernel, n_ctx_tiles=n_ctx_tiles,
                             tiles_per_batch=tiles_per_batch, ctx_row=ctx_row)
    return pl.pallas_call(
        kern,
        grid=(t // tm, n // tn),
        in_specs=[
            pl.BlockSpec((tm, d), lambda i, j: (i, 0)),
            pl.BlockSpec((8, d), lambda i, j: (0, 0)),
            pl.BlockSpec((8, d), lambda i, j: (0, 1)),
            pl.BlockSpec((1, d), lambda i, j: (0, 0)),
            pl.BlockSpec((d, tn), lambda i, j: (0, j)),
            pl.BlockSpec((d, LANES), lambda i, j: (0, 0)),
            pl.BlockSpec((LANES, 2 * W_Q), lambda i, j: (0, 0)),
            pl.BlockSpec((1, 2 * W_Q), lambda i, j: (0, 0)),
        ],
        out_specs=[
            pl.BlockSpec((tm, tn), lambda i, j: (i, j)),
            pl.BlockSpec((tm, 2 * W_Q), lambda i, j: (i, 0)),
        ],
        out_shape=[
            jax.ShapeDtypeStruct((t, n), F32),
            jax.ShapeDtypeStruct((t, 2 * W_Q), F32),
        ],
        scratch_shapes=[pltpu.VMEM((tm, d), BF16)],
        compiler_params=_cparams("parallel", "arbitrary"),
        name="norm_mod_project",
    )(x, mod, mod, g, w_main, w_a, w_a2, b_a)


def _gla_pairs(rev):
    c, sub = GLA_CHUNK, GLA_SUB
    half = c // 2
    if rev:
        return ((0, half, half, c, half - 1),
                (0, sub, sub, half, sub - 1),
                (half, half + sub, half + sub, c, half + sub - 1))
    return ((half, c, 0, half, half),
            (sub, half, 0, sub, sub),
            (half + sub, c, half, half + sub, half + sub))


def _place_rows(x, r0, n):
    parts = []
    if r0:
        parts.append(jnp.zeros((r0, x.shape[1]), x.dtype))
    parts.append(x)
    rest = n - r0 - x.shape[0]
    if rest:
        parts.append(jnp.zeros((rest, x.shape[1]), x.dtype))
    return jnp.concatenate(parts, axis=0)


def _gla_chunk(q_ref, k_ref, v_ref, la_ref, e_ref, o_ref, s_ref, cum_ref, kc_ref, r0, h, rev):
    c, sub = GLA_CHUNK, GLA_SUB
    nsub = c // sub
    rows = pl.ds(r0, c)
    kcols = slice(h * GLA_DK, (h + 1) * GLA_DK)
    vcols = slice(h * GLA_DV, (h + 1) * GLA_DV)
    q = q_ref[rows, kcols] * (GLA_DK ** -0.5)
    k = k_ref[rows, kcols]
    v = v_ref[rows, vcols].astype(BF16)
    la = la_ref[rows, kcols]

    ri = lax.broadcasted_iota(jnp.int32, (c, c), 0)
    ci = lax.broadcasted_iota(jnp.int32, (c, c), 1)
    before = (ci >= ri) if rev else (ci <= ri)
    tri = jnp.where(before, 1.0, 0.0).astype(BF16)
    la_hi = la.astype(BF16)
    rem = la - la_hi.astype(F32)
    la_mid = rem.astype(BF16)
    la_lo = (rem - la_mid.astype(F32)).astype(BF16)
    cum = (_dot(tri, la_hi) + _dot(tri, la_mid) + _dot(tri, la_lo)) * LOG2E
    cum_ref[...] = cum
    kc_ref[...] = k
    yield
    last = 0 if rev else c - 1
    total = cum[last:last + 1, :]

    s = s_ref[...]
    o = _dot((q * jnp.exp2(cum)).astype(BF16), s.astype(BF16))

    qparts, kparts = [], []
    for l0, l1, e0, e1, ref in _gla_pairs(rev):
        cr = cum[ref:ref + 1, :]
        qparts.append(_place_rows(q[l0:l1] * jnp.exp2(cum[l0:l1] - cr), l0, c))
        kparts.append(_place_rows(k[e0:e1] * jnp.exp2(cr - cum[e0:e1]), e0, c))
    s_off = lax.dot_general(jnp.concatenate(qparts, axis=1).astype(BF16),
                            jnp.concatenate(kparts, axis=1).astype(BF16),
                            (((1,), (1,)), ((), ())), preferred_element_type=F32)

    parts = []
    for jj in range(sub):
        cj = jnp.concatenate(
            [jnp.broadcast_to(cum_ref[a * sub + jj:a * sub + jj + 1, :], (sub, GLA_DK)) for a in range(nsub)],
            axis=0)
        kj = jnp.concatenate(
            [jnp.broadcast_to(kc_ref[a * sub + jj:a * sub + jj + 1, :], (sub, GLA_DK)) for a in range(nsub)],
            axis=0)
        parts.append((q * jnp.exp2(jnp.minimum(cum - cj, 0.0)) * kj).astype(BF16))
    diag = _dot(jnp.concatenate(parts, axis=1), e_ref[...])
    yield

    lag = (ci - ri) if rev else (ri - ci)
    room = (sub - 1 - ri % sub) if rev else (ri % sub)
    in_diag = lag.astype(jnp.uint32) <= room.astype(jnp.uint32)
    p = jnp.where(in_diag, diag, 0.0) + s_off
    o_ref[rows, vcols] = o + _dot(p.astype(BF16), v)

    k_out = (k * jnp.exp2(total - cum)).astype(BF16)
    upd = lax.dot_general(k_out, v, (((0,), (0,)), ((), ())), preferred_element_type=F32)
    dmat = jnp.broadcast_to(jnp.exp2(total), (GLA_DK, GLA_DK)).T
    for half in range(GLA_DV // GLA_DK):
        cols = slice(half * GLA_DK, (half + 1) * GLA_DK)
        s_ref[:, cols] = s[:, cols] * dmat + upd[:, cols]


def _gla_kernel(qf, kf, vf, laf, qb, kb, vb, lab, e_ref, of_ref, ob_ref, *scratch):
    n = 2 * GLA_HEADS
    states, cums, keys = scratch[:n], scratch[n:2 * n], scratch[2 * n:]

    @pl.when(pl.program_id(1) == 0)
    def _():
        for s_ref in states:
            s_ref[...] = jnp.zeros_like(s_ref)

    nchunk = ROW_TILE // GLA_CHUNK

    def body(c, carry):
        r_f = pl.multiple_of(c * GLA_CHUNK, GLA_CHUNK)
        r_b = pl.multiple_of((nchunk - 1 - c) * GLA_CHUNK, GLA_CHUNK)
        units = []
        for h in range(GLA_HEADS):
            f, b = h, GLA_HEADS + h
            units.append(_gla_chunk(qf, kf, vf, laf, e_ref, of_ref, states[f], cums[f], keys[f], r_f, h, False))
            units.append(_gla_chunk(qb, kb, vb, lab, e_ref, ob_ref, states[b], cums[b], keys[b], r_b, h, True))
        for _ in range(GLA_STAGES):
            for u in units:
                next(u, None)
        return carry

    lax.fori_loop(0, nchunk, body, 0)


def _gla(z, la, *, batch, lc, l, col_q):
    t_rows = z.shape[0]
    nctx = lc // ROW_TILE
    nlat = l // ROW_TILE
    steps = nctx + nlat
    qb0 = col_q // W_Q
    kb0 = (col_q + W_Q) // W_Q
    vb0 = (col_q + 2 * W_Q) // W_V

    def fwd_tile(b, t):
        return jnp.where(t < nctx, b * nctx + t, batch * nctx + b * nlat + (t - nctx))

    def bwd_tile(b, t):
        return jnp.where(t < nctx, b * nctx + (nctx - 1 - t), batch * nctx + b * nlat + (steps - 1 - t))

    def spec(width, tile_fn, col):
        return pl.BlockSpec((ROW_TILE, width), lambda b, t: (tile_fn(b, t), col))

    kk = jnp.arange(GLA_SUB * GLA_DK, dtype=jnp.int32)[:, None] // GLA_DK
    cc = jnp.arange(GLA_CHUNK, dtype=jnp.int32)[None, :] % GLA_SUB
    e_mat = (kk == cc).astype(BF16)

    o_shape = jax.ShapeDtypeStruct((t_rows, W_V), F32)
    n_units = 2 * GLA_HEADS
    state = pltpu.VMEM((GLA_DK, GLA_DV), F32)
    chunk = pltpu.VMEM((GLA_CHUNK, GLA_DK), F32)
    return pl.pallas_call(
        _gla_kernel,
        grid=(batch, steps),
        in_specs=[
            spec(W_Q, fwd_tile, qb0), spec(W_Q, fwd_tile, kb0), spec(W_V, fwd_tile, vb0), spec(W_Q, fwd_tile, 0),
            spec(W_Q, bwd_tile, qb0), spec(W_Q, bwd_tile, kb0), spec(W_V, bwd_tile, vb0), spec(W_Q, bwd_tile, 1),
            pl.BlockSpec((GLA_SUB * GLA_DK, GLA_CHUNK), lambda b, t: (0, 0)),
        ],
        out_specs=[spec(W_V, fwd_tile, 0), spec(W_V, bwd_tile, 0)],
        out_shape=[o_shape, o_shape],
        scratch_shapes=[state] * n_units + [chunk] * (2 * n_units),
        compiler_params=_cparams("parallel", "arbitrary"),
        name="gla_scan",
    )(z, z, z, la, z, z, z, la, e_mat)


def _dft_tables(n):
    idx = jnp.arange(n, dtype=jnp.int32)
    ang = ((idx[:, None] * idx[None, :]) % n).astype(F32) * (2.0 * math.pi / n)
    scale = n ** -0.5
    return (jnp.cos(ang) * scale).astype(BF16), (jnp.sin(ang) * scale).astype(BF16)


def _ft_chan_kernel(f_ref, cs_ref, xc_ref, xs_ref):
    for g in range(FT_GROUPS):
        cols = slice(g * FT_CH, (g + 1) * FT_CH)
        y = _dot(f_ref[:, cols].astype(BF16), cs_ref[...])
        xc_ref[:, cols] = y[:, :FT_CH].astype(BF16)
        xs_ref[:, cols] = y[:, FT_CH:].astype(BF16)


def _ft_channel(z, cs, *, col_f):
    t_rows = z.shape[0]
    tm = 1024 if t_rows % 1024 == 0 else ROW_TILE
    shape = jax.ShapeDtypeStruct((t_rows, W_FT), BF16)
    return pl.pallas_call(
        _ft_chan_kernel,
        grid=(t_rows // tm,),
        in_specs=[pl.BlockSpec((tm, W_FT), lambda i: (i, col_f // W_FT)),
                  pl.BlockSpec((FT_CH, 2 * FT_CH), lambda i: (0, 0))],
        out_specs=[pl.BlockSpec((tm, W_FT), lambda i: (i, 0))] * 2,
        out_shape=[shape, shape],
        compiler_params=_cparams("parallel"),
        name="fourier_channels",
    )(z, cs)


def _ft_pos_kernel(c_ref, s_ref, xc_ref, xs_ref, o_ref):
    o_ref[...] = _dot(c_ref[...], xc_ref[...]) - _dot(s_ref[...], xs_ref[...])


def _ft_positions(cos_t, sin_t, xc, xs, *, batch, seq):
    nt = seq // ROW_TILE
    x_spec = pl.BlockSpec((seq, W_FT), lambda b, j: (b, 0))
    t_spec = pl.BlockSpec((ROW_TILE, seq), lambda b, j: (j, 0))
    return pl.pallas_call(
        _ft_pos_kernel,
        grid=(batch, nt),
        in_specs=[t_spec, t_spec, x_spec, x_spec],
        out_specs=pl.BlockSpec((ROW_TILE, W_FT), lambda b, j: (b * nt + j, 0)),
        out_shape=jax.ShapeDtypeStruct((batch * seq, W_FT), F32),
        compiler_params=_cparams("parallel", "parallel"),
        name="fourier_positions",
    )(cos_t, sin_t, xc, xs)


def _branch_kernel(gg_ref, gs_ref, gf_ref, of_ref, ob_ref, r_ref, su_ref, sv_ref, yft_ref,
                   gn_ref, ws_ref, bs_ref, pg_ref, ps_ref, pf_ref, m_ref):
    o = of_ref[...] + ob_ref[...]
    heads = []
    for h in range(GLA_HEADS):
        oh = o[:, h * GLA_DV:(h + 1) * GLA_DV]
        heads.append(oh * lax.rsqrt(jnp.mean(oh * oh, axis=-1, keepdims=True) + EPS))
    r = r_ref[...]
    y_gla = (jnp.concatenate(heads, axis=-1) * gn_ref[...]) * (r * _sigmoid(r))
    acc = _sigmoid(gg_ref[...]) * _dot(y_gla.astype(BF16), pg_ref[...])

    u = _gelu_tanh(su_ref[...])
    v = _gelu_tanh(sv_ref[...])
    groups = []
    for g in range(SG_GROUPS):
        vg = v[:, g * SG_CH:(g + 1) * SG_CH]
        mu = jnp.mean(vg, axis=-1, keepdims=True)
        var = jnp.mean(jnp.square(vg - mu), axis=-1, keepdims=True)
        vn = ((vg - mu) * lax.rsqrt(var + EPS)).astype(BF16)
        chunks = []
        for c in range(ROW_TILE // SG_CHUNK):
            chunks.append(_dot(ws_ref[g], vn[c * SG_CHUNK:(c + 1) * SG_CHUNK, :]) + bs_ref[g])
        groups.append(jnp.concatenate(chunks, axis=0))
    y_sg = u * jnp.concatenate(groups, axis=-1)
    acc = acc + _sigmoid(gs_ref[...]) * _dot(y_sg.astype(BF16), ps_ref[...])

    acc = acc + _sigmoid(gf_ref[...]) * _dot(yft_ref[...].astype(BF16), pf_ref[...])
    m_ref[...] = acc.astype(BF16)


def _branches(z, o_f, o_b, y_ft, gla_norm, w_s, b_s, p_gla, p_sg, p_ft, *, cols):
    t_rows = z.shape[0]
    d = p_gla.shape[1]
    tm = ROW_TILE

    def zspec(width, col0):
        return pl.BlockSpec((tm, width), lambda i: (i, col0 // width))

    def rows(width):
        return pl.BlockSpec((tm, width), lambda i: (i, 0))

    def const(shape):
        return pl.BlockSpec(shape, lambda i: (0,) * len(shape))

    return pl.pallas_call(
        _branch_kernel,
        grid=(t_rows // tm,),
        in_specs=[
            zspec(d, cols["gates"]), zspec(d, cols["gates"] + d), zspec(d, cols["gates"] + 2 * d),
            rows(W_V), rows(W_V), zspec(W_V, cols["r"]), zspec(W_SG, cols["su"]), zspec(W_SG, cols["sv"]),
            rows(W_FT),
            const((1, W_V)), const((SG_GROUPS, SG_CHUNK, SG_CHUNK)), const((SG_GROUPS, SG_CHUNK, SG_CH)),
            const((W_V, d)), const((W_SG, d)), const((W_FT, d)),
        ],
        out_specs=rows(d),
        out_shape=jax.ShapeDtypeStruct((t_rows, d), BF16),
        compiler_params=_cparams("parallel"),
        name="branch_merge",
    )(z, z, z, o_f, o_b, z, z, z, y_ft, gla_norm, w_s, b_s, p_gla, p_sg, p_ft)


def _outproj_kernel(m_ref, x_ref, w_ref, g1_ref, n2_ref, sh_ref, sc_ref, wr_ref, br_ref,
                    xo_ref, h_ref, lg_ref, *, n_ctx_tiles, tiles_per_batch, ctx_row):
    row = _mod_row(pl.program_id(0), n_ctx_tiles, tiles_per_batch, ctx_row)
    x = x_ref[...] + g1_ref[pl.ds(row, 1), :] * _dot(m_ref[...], w_ref[...])
    xo_ref[...] = x
    y = x * lax.rsqrt(jnp.mean(x * x, axis=-1, keepdims=True) + EPS) * n2_ref[...]
    h = y * (1.0 + sc_ref[pl.ds(row, 1), :]) + sh_ref[pl.ds(row, 1), :]
    h_hi = h.astype(BF16)
    h_ref[...] = h_hi
    h_lo = (h - h_hi.astype(F32)).astype(BF16)
    w = wr_ref[...]
    w_hi = w.astype(BF16)
    w_lo = (w - w_hi.astype(F32)).astype(BF16)
    lg_ref[...] = _dot(h_hi, w_hi) + (_dot(h_hi, w_lo) + _dot(h_lo, w_hi)) + br_ref[...]


def _out_project(merged, x, w_out, mod, norm2, w_router, b_router, *, n_ctx_tiles, tiles_per_batch, ctx_row):
    t_rows, d = x.shape
    tm = ROW_TILE
    kern = functools.partial(_outproj_kernel, n_ctx_tiles=n_ctx_tiles,
                             tiles_per_batch=tiles_per_batch, ctx_row=ctx_row)

    def rows(width):
        return pl.BlockSpec((tm, width), lambda i: (i, 0))

    def modspec(k):
        return pl.BlockSpec((8, d), lambda i: (0, k))

    return pl.pallas_call(
        kern,
        grid=(t_rows // tm,),
        in_specs=[rows(d), rows(d), pl.BlockSpec((d, d), lambda i: (0, 0)),
                  modspec(2), pl.BlockSpec((1, d), lambda i: (0, 0)), modspec(3), modspec(4),
                  pl.BlockSpec((d, LANES), lambda i: (0, 0)), pl.BlockSpec((1, LANES), lambda i: (0, 0))],
        out_specs=[rows(d), rows(d), rows(LANES)],
        out_shape=[jax.ShapeDtypeStruct((t_rows, d), F32), jax.ShapeDtypeStruct((t_rows, d), BF16),
                   jax.ShapeDtypeStruct((t_rows, LANES), F32)],
        compiler_params=_cparams("parallel"),
        name="out_project",
    )(merged, x, w_out, mod, norm2, mod, mod, w_router, b_router)


def _expert_kernel(be_ref, nu_ref, x_ref, wgu_ref, bgu_ref, wd_ref, bd_ref, y_ref):
    @pl.when(pl.program_id(0) < nu_ref[0])
    def _():
        gu = _dot(x_ref[...], wgu_ref[0, 0]) + bgu_ref[0, 0]
        de = gu.shape[1] // 2
        gate = jnp.minimum(gu[:, :de], SWIGLU_LIMIT)
        up = jnp.clip(gu[:, de:], -SWIGLU_LIMIT, SWIGLU_LIMIT)
        hdn = (up + 1.0) * (gate * _sigmoid(SWIGLU_ALPHA * gate))
        y_ref[...] = _dot(hdn.astype(BF16), wd_ref[0, 0]) + bd_ref[0, 0]


def _experts(xs, blk_e, n_used, layer, w_gu, b_gu, w_down, b_down):
    rows_total, d = xs.shape
    nb = rows_total // MOE_BLOCK
    depth, ne, _, n_gu = w_gu.shape
    de = w_down.shape[2]

    def blk(i, be, nu):
        return jnp.minimum(i, nu[0] - 1)

    def wspec(k, n):
        return pl.BlockSpec((1, 1, k, n), lambda i, be, nu: (layer, be[blk(i, be, nu)], 0, 0))

    grid_spec = pltpu.PrefetchScalarGridSpec(
        num_scalar_prefetch=2,
        grid=(nb,),
        in_specs=[
            pl.BlockSpec((MOE_BLOCK, d), lambda i, be, nu: (blk(i, be, nu), 0)),
            wspec(d, n_gu), wspec(1, n_gu), wspec(de, d), wspec(1, d),
        ],
        out_specs=pl.BlockSpec((MOE_BLOCK, d), lambda i, be, nu: (blk(i, be, nu), 0)),
    )
    return pl.pallas_call(
        _expert_kernel,
        grid_spec=grid_spec,
        out_shape=jax.ShapeDtypeStruct((rows_total, d), F32),
        compiler_params=_cparams("arbitrary"),
        name="moe_experts",
    )(blk_e, n_used, xs, w_gu, b_gu.reshape(depth, ne, 1, n_gu), w_down, b_down.reshape(depth, ne, 1, d))


def _route(logits, n_tok):
    top_val, top_idx = lax.top_k(logits, TOP_K)
    probs = jax.nn.softmax(top_val, axis=-1)
    m = n_tok * TOP_K
    flat_e = top_idx.reshape(m).astype(jnp.int32)
    onehot = (flat_e[:, None] == jnp.arange(N_EXPERTS, dtype=jnp.int32)[None, :]).astype(jnp.int32)
    csum = jnp.cumsum(onehot, axis=0)
    counts = csum[-1]
    rank = jnp.sum(csum * onehot, axis=1) - 1
    pad_counts = (counts + MOE_BLOCK - 1) // MOE_BLOCK * MOE_BLOCK
    pad_end = jnp.cumsum(pad_counts)
    pad_start = pad_end - pad_counts
    pos = pad_start[flat_e] + rank
    nb = (m + MOE_BLOCK - 1) // MOE_BLOCK + N_EXPERTS
    idx_bits = (m - 1).bit_length()
    assert (2 * N_EXPERTS + 2) << idx_bits < 2 ** 31
    experts = jnp.arange(N_EXPERTS, dtype=jnp.int32)
    real_keys = ((2 * flat_e) << idx_bits) | jnp.arange(m, dtype=jnp.int32)
    n_fill = nb * MOE_BLOCK - m
    fill_e = jnp.arange(n_fill, dtype=jnp.int32) // (n_fill // N_EXPERTS)
    fill_i = jnp.arange(n_fill, dtype=jnp.int32) % (n_fill // N_EXPERTS)
    assert n_fill % N_EXPERTS == 0 and n_fill // N_EXPERTS >= MOE_BLOCK - 1
    fill_keys = jnp.where(fill_i < (pad_counts - counts)[fill_e],
                          (2 * fill_e + 1) << idx_bits, (2 * N_EXPERTS + 1) << idx_bits)
    keys = lax.sort(jnp.concatenate([real_keys, fill_keys]))
    is_real = ((keys >> idx_bits) & 1) == 0
    row_tok = jnp.where(is_real, (keys & ((1 << idx_bits) - 1)) // TOP_K, 0)
    blk_start = jnp.arange(nb, dtype=jnp.int32) * MOE_BLOCK
    blk_e = jnp.minimum(jnp.sum((pad_end[None, :] <= blk_start[:, None]).astype(jnp.int32), axis=1),
                        N_EXPERTS - 1)
    n_used = (pad_end[-1] // MOE_BLOCK).astype(jnp.int32).reshape(1)
    return probs, pos.reshape(n_tok, TOP_K), row_tok, blk_e, n_used


def _rows(x, idx):
    return x.at[idx].get(mode="promise_in_bounds")


def _moe(h2, logits, layer, w_gu, b_gu, w_down, b_down):
    n_tok = h2.shape[0]
    probs, pos, row_tok, blk_e, n_used = _route(logits[:, :N_EXPERTS], n_tok)
    xs = _rows(h2, row_tok)
    yb = _experts(xs, blk_e, n_used, layer, w_gu, b_gu, w_down, b_down)
    y = _rows(yb, pos.T.reshape(-1)).reshape(TOP_K, n_tok, -1)
    return y, probs


def _combine_kernel(x_ref, y0_ref, y1_ref, y2_ref, y3_ref, p_ref, g2_ref, nf_ref, o_ref,
                    *, tile0, final, n_ctx_tiles, tiles_per_batch, ctx_row):
    row = _mod_row(pl.program_id(0) + tile0, n_ctx_tiles, tiles_per_batch, ctx_row)
    p = p_ref[...]
    y = (p[:, 0:1] * y0_ref[0] + p[:, 1:2] * y1_ref[0]) + (p[:, 2:3] * y2_ref[0] + p[:, 3:4] * y3_ref[0])
    x = x_ref[...] + g2_ref[pl.ds(row, 1), :] * y
    if final:
        x = x * lax.rsqrt(jnp.mean(x * x, axis=-1, keepdims=True) + EPS) * nf_ref[...]
    o_ref[...] = x


def _combine(x, y, probs, mod, norm_f, *, tile0, final, n_ctx_tiles, tiles_per_batch, ctx_row):
    assert y.shape[0] == TOP_K == 4
    _, n_rows, d = y.shape
    tm = ROW_TILE
    kern = functools.partial(_combine_kernel, tile0=tile0, final=final, n_ctx_tiles=n_ctx_tiles,
                             tiles_per_batch=tiles_per_batch, ctx_row=ctx_row)

    def yspec(k):
        return pl.BlockSpec((1, tm, d), lambda i: (k, i, 0))

    return pl.pallas_call(
        kern,
        grid=(n_rows // tm,),
        in_specs=[pl.BlockSpec((tm, d), lambda i: (tile0 + i, 0)),
                  yspec(0), yspec(1), yspec(2), yspec(3),
                  pl.BlockSpec((tm, TOP_K), lambda i: (i, 0)),
                  pl.BlockSpec((8, d), lambda i: (0, 5)),
                  pl.BlockSpec((1, d), lambda i: (0, 0))],
        out_specs=pl.BlockSpec((tm, d), lambda i: (i, 0)),
        out_shape=jax.ShapeDtypeStruct((n_rows, d), F32),
        compiler_params=_cparams("parallel"),
        name="moe_combine",
    )(x, y, y, y, y, probs, mod, norm_f)


def _cast_kernel(x_ref, o_ref):
    o_ref[...] = x_ref[...].astype(BF16)


def _to_bf16(w):
    shape = w.shape
    k, n = shape[-2:]
    w3 = w.reshape(-1, k, n)
    tk = 1024
    out = pl.pallas_call(
        _cast_kernel,
        grid=(w3.shape[0], k // tk),
        in_specs=[pl.BlockSpec((1, tk, n), lambda e, j: (e, j, 0))],
        out_specs=pl.BlockSpec((1, tk, n), lambda e, j: (e, j, 0)),
        out_shape=jax.ShapeDtypeStruct(w3.shape, BF16),
        compiler_params=_cparams("parallel", "parallel"),
        name="cast_bf16",
    )(w3)
    return out.reshape(shape)


def _grid_pos_embed(n, d):
    rows = n // GRID_W
    row = jnp.broadcast_to(jnp.arange(rows, dtype=F32)[:, None], (rows, GRID_W)).reshape(-1)
    col = jnp.broadcast_to(jnp.arange(GRID_W, dtype=F32)[None, :], (rows, GRID_W)).reshape(-1)
    quarter = d // 4
    omega = 1.0 / (10000.0 ** (jnp.arange(quarter, dtype=F32) / quarter))

    def enc(p):
        a = p[:, None] * omega
        return jnp.concatenate([jnp.sin(a), jnp.cos(a)], axis=-1)

    return jnp.concatenate([enc(row), enc(col)], axis=-1)


def kernel(x, c, ctx, c_ctx, w_mod, b_mod, norm1, w_in, w_a2, b_a, gla_norm, w_s, b_s, p_gla, p_sg, p_ft,
           w_out, norm2, w_router, b_router, w_gu, b_gu, w_down, b_down, norm_f):
    batch, l, d = x.shape
    lc = ctx.shape[1]
    depth = w_mod.shape[0]
    n_ctx_rows = batch * lc
    t_rows = n_ctx_rows + batch * l
    assert lc % ROW_TILE == 0 and l % ROW_TILE == 0 and batch < 8

    tm1 = math.gcd(math.gcd(n_ctx_rows, l), 1024)
    assert tm1 % ROW_TILE == 0
    proj_tiles = dict(n_ctx_tiles=n_ctx_rows // tm1, tiles_per_batch=l // tm1, ctx_row=batch)
    row_tiles = dict(n_ctx_tiles=n_ctx_rows // ROW_TILE, tiles_per_batch=l // ROW_TILE, ctx_row=batch)

    o_q = 0
    o_k = o_q + W_Q
    o_v = o_k + W_Q
    o_r = o_v + W_V
    o_a = o_r + W_V
    o_su = o_a + 2 * GLA_RANK
    o_sv = o_su + W_SG
    o_f = o_sv + W_SG
    o_g = o_f + W_FT
    n_gates = 3 * d
    cols = dict(gates=0, q=n_gates, r=n_gates + 2 * W_Q + W_V, su=n_gates + 2 * W_Q + 2 * W_V)
    cols["sv"] = cols["su"] + W_SG
    cols["f"] = cols["sv"] + W_SG
    n_main = cols["f"] + W_FT

    xt = jnp.concatenate([ctx.reshape(n_ctx_rows, d),
                          (x + _grid_pos_embed(l, d)[None]).reshape(batch * l, d)], axis=0)

    cond = jnp.zeros((8, d), F32).at[:batch].set(c).at[batch].set(c_ctx)
    mods = _mod_vectors(cond, w_mod, b_mod)

    cc, sc = _dft_tables(FT_CH)
    chan_tab = jnp.concatenate([cc, sc], axis=1)
    cos_l, sin_l = _dft_tables(l)
    cos_c, sin_c = _dft_tables(lc)

    wgu_all = _to_bf16(w_gu)
    wdn_all = _to_bf16(w_down)

    for i in range(depth):
        last = i == depth - 1
        wi = w_in[i]
        w_main = jnp.concatenate(
            [wi[:, o_g:], wi[:, o_q:o_a], wi[:, o_su:o_g]], axis=1).astype(BF16)
        w_a = jnp.pad(wi[:, o_a:o_su], ((0, 0), (0, LANES - 2 * GLA_RANK))).astype(BF16)
        wa2 = jnp.zeros((LANES, 2 * W_Q), F32)
        wa2 = wa2.at[:GLA_RANK, :W_Q].set(w_a2[i, 0]).at[GLA_RANK:2 * GLA_RANK, W_Q:].set(w_a2[i, 1])
        mod = mods[i]

        z, la = _project(xt, mod, norm1[i][None], w_main, w_a, wa2.astype(BF16), b_a[i].reshape(1, 2 * W_Q),
                         tm=tm1, tn=768, **proj_tiles)
        assert z.shape[1] == n_main

        o_fw, o_bw = _gla(z, la, batch=batch, lc=lc, l=l, col_q=cols["q"])

        xc, xs = _ft_channel(z, chan_tab, col_f=cols["f"])
        y_ft = jnp.concatenate([
            _ft_positions(cos_c, sin_c, xc[:n_ctx_rows], xs[:n_ctx_rows], batch=batch, seq=lc),
            _ft_positions(cos_l, sin_l, xc[n_ctx_rows:], xs[n_ctx_rows:], batch=batch, seq=l),
        ], axis=0)

        bs_b = jnp.broadcast_to(b_s[i][:, :, None], (SG_GROUPS, SG_CHUNK, SG_CH))
        merged = _branches(z, o_fw, o_bw, y_ft, gla_norm[i][None], w_s[i].astype(BF16), bs_b,
                           p_gla[i].astype(BF16), p_sg[i].astype(BF16), p_ft[i].astype(BF16), cols=cols)

        w_r = jnp.pad(w_router[i], ((0, 0), (0, LANES - N_EXPERTS)))
        b_r = jnp.pad(b_router[i], (0, LANES - N_EXPERTS)).reshape(1, LANES)
        xt, h2, logits = _out_project(merged, xt, w_out[i].astype(BF16), mod, norm2[i][None], w_r, b_r,
                                      **row_tiles)

        row0 = n_ctx_rows if last else 0
        y, probs = _moe(h2[row0:], logits[row0:], i, wgu_all, b_gu, wdn_all, b_down)
        xt = _combine(xt, y, probs, mod, norm_f[None], tile0=row0 // ROW_TILE, final=last, **row_tiles)

    return xt.reshape(batch, l, d)
```

```python
import functools
import math

import jax
import jax.numpy as jnp
from jax import lax
from jax.experimental import pallas as pl
from jax.experimental.pallas import tpu as pltpu

F32 = jnp.float32
BF16 = jnp.bfloat16

EPS = 1e-6
GRID_W = 64
GLA_HEADS = 4
GLA_DK = 128
GLA_DV = 256
GLA_RANK = 16
GLA_TAU = 16.0
GLA_CHUNK = 64
GLA_SUB = 16
GLA_STAGES = 3
SG_GROUPS = 4
SG_CH = 128
SG_CHUNK = 128
FT_GROUPS = 4
FT_CH = 128
N_EXPERTS = 32
TOP_K = 4
SWIGLU_LIMIT = 7.0
SWIGLU_ALPHA = 1.702
MOE_BLOCK = 256
W_CHUNK = 512

LOG2E = math.log2(math.e)
LANES = 128
ROW_TILE = 256
VMEM_LIMIT = 56 * 1024 * 1024

W_Q = GLA_HEADS * GLA_DK
W_V = GLA_HEADS * GLA_DV
W_SG = SG_GROUPS * SG_CH
W_FT = FT_GROUPS * FT_CH


def _cparams(*sem):
    return pltpu.CompilerParams(dimension_semantics=sem, vmem_limit_bytes=VMEM_LIMIT)


def _sigmoid(x):
    return 1.0 / (1.0 + jnp.exp(-x))


def _gelu_tanh(x):
    return 0.5 * x * (1.0 + jnp.tanh(math.sqrt(2.0 / math.pi) * (x + 0.044715 * (x * x * x))))


def _dot(a, b):
    return jnp.dot(a, b, preferred_element_type=F32)


def _pack_bf16_pairs(x):
    bits = lax.bitcast_convert_type(x.astype(F32), jnp.uint32)
    half = x.shape[1] // 2
    return (bits[:, :half] >> 16) | (bits[:, half:] & jnp.uint32(0xFFFF0000))


def _unpack_bf16_pairs(u):
    lo = lax.bitcast_convert_type(u << 16, F32)
    hi = lax.bitcast_convert_type(u & jnp.uint32(0xFFFF0000), F32)
    return jnp.concatenate([lo, hi], axis=1).astype(BF16)


def _mod_row(tile, n_ctx_tiles, tiles_per_batch, ctx_row):
    return jnp.where(tile < n_ctx_tiles, ctx_row, (tile - n_ctx_tiles) // tiles_per_batch)


def _mod_kernel(c_ref, w_ref, b_ref, o_ref):
    c = c_ref[...]
    s = (c * _sigmoid(c)).astype(BF16)
    o_ref[0] = _dot(s, w_ref[0].astype(BF16)) + b_ref[0]


def _mod_vectors(cond, w_mod, b_mod):
    depth, d, n = w_mod.shape
    tn = 1024
    return pl.pallas_call(
        _mod_kernel,
        grid=(depth, n // tn),
        in_specs=[
            pl.BlockSpec((8, d), lambda l, j: (0, 0)),
            pl.BlockSpec((1, d, tn), lambda l, j: (l, 0, j)),
            pl.BlockSpec((1, 1, tn), lambda l, j: (l, 0, j)),
        ],
        out_specs=pl.BlockSpec((1, 8, tn), lambda l, j: (l, 0, j)),
        out_shape=jax.ShapeDtypeStruct((depth, 8, n), F32),
        compiler_params=_cparams("parallel", "parallel"),
        name="mod_vectors",
    )(cond, w_mod, b_mod.reshape(depth, 1, n))


def _proj_kernel(x_ref, sh_ref, sc_ref, g_ref, w_ref, wa_ref, wa2_ref, ba_ref,
                 z_ref, la_ref, a_scr, *, n_ctx_tiles, tiles_per_batch, ctx_row):
    i = pl.program_id(0)
    j = pl.program_id(1)

    @pl.when(j == 0)
    def _():
        row = _mod_row(i, n_ctx_tiles, tiles_per_batch, ctx_row)
        scale = 1.0 + sc_ref[pl.ds(row, 1), :]
        shift = sh_ref[pl.ds(row, 1), :]

        def body(r, carry):
            rows = pl.ds(pl.multiple_of(r * ROW_TILE, ROW_TILE), ROW_TILE)
            x = x_ref[rows, :]
            y = x * lax.rsqrt(jnp.mean(x * x, axis=-1, keepdims=True) + EPS) * g_ref[...]
            hb = (y * scale + shift).astype(BF16)
            a_scr[rows, :] = hb
            a_lr = _dot(hb, wa_ref[...])
            pre = _dot(a_lr.astype(BF16), wa2_ref[...]) + ba_ref[...]
            la_ref[rows, :] = -(jnp.maximum(-pre, 0.0) + jnp.log1p(jnp.exp(-jnp.abs(pre)))) * (1.0 / GLA_TAU)
            return carry

        lax.fori_loop(0, x_ref.shape[0] // ROW_TILE, body, 0)

    z_ref[...] = _dot(a_scr[...], w_ref[...])


def _project(x, mod, g, w_main, w_a, w_a2, b_a, *, tm, tn, n_ctx_tiles, tiles_per_batch, ctx_row):
    t, d = x.shape
    n = w_main.shape[1]
    kern = functools.partial(_proj_kernel, n_ctx_tiles=n_ctx_tiles,
                             tiles_per_batch=tiles_per_batch, ctx_row=ctx_row)
    return pl.pallas_call(
        kern,
        grid=(t // tm, n // tn),
        in_specs=[
            pl.BlockSpec((tm, d), lambda i, j: (i, 0)),
            pl.BlockSpec((8, d), lambda i, j: (0, 0)),
            pl.BlockSpec((8, d), lambda i, j: (0, 1)),
            pl.BlockSpec((1, d), lambda i, j: (0, 0)),
            pl.BlockSpec((d, tn), lambda i, j: (0, j)),
            pl.BlockSpec((d, LANES), lambda i, j: (0, 0)),
            pl.BlockSpec((LANES, 2 * W_Q), lambda i, j: (0, 0)),
            pl.BlockSpec((1, 2 * W_Q), lambda i, j: (0, 0)),
        ],
        out_specs=[
            pl.BlockSpec((tm, tn), lambda i, j: (i, j)),
            pl.BlockSpec((tm, 2 * W_Q), lambda i, j: (i, 0)),
        ],
        out_shape=[
            jax.ShapeDtypeStruct((t, n), F32),
            jax.ShapeDtypeStruct((t, 2 * W_Q), F32),
        ],
        scratch_shapes=[pltpu.VMEM((tm, d), BF16)],
        compiler_params=_cparams("parallel", "arbitrary"),
        name="norm_mod_project",
    )(x, mod, mod, g, w_main, w_a, w_a2, b_a)


def _gla_pairs(rev):
    c, sub = GLA_CHUNK, GLA_SUB
    half = c // 2
    if rev:
        return ((0, half, half, c, half - 1),
                (0, sub, sub, half, sub - 1),
                (half, half + sub, half + sub, c, half + sub - 1))
    return ((half, c, 0, half, half),
            (sub, half, 0, sub, sub),
            (half + sub, c, half, half + sub, half + sub))


def _place_rows(x, r0, n):
    parts = []
    if r0:
        parts.append(jnp.zeros((r0, x.shape[1]), x.dtype))
    parts.append(x)
    rest = n - r0 - x.shape[0]
    if rest:
        parts.append(jnp.zeros((rest, x.shape[1]), x.dtype))
    return jnp.concatenate(parts, axis=0)


def _gla_chunk(q_ref, k_ref, v_ref, la_ref, e_ref, o_ref, s_ref, cum_ref, kc_ref, r0, h, rev):
    c, sub = GLA_CHUNK, GLA_SUB
    nsub = c // sub
    rows = pl.ds(r0, c)
    kcols = slice(h * GLA_DK, (h + 1) * GLA_DK)
    vcols = slice(h * GLA_DV, (h + 1) * GLA_DV)
    q = q_ref[rows, kcols] * (GLA_DK ** -0.5)
    k = k_ref[rows, kcols]
    v = v_ref[rows, vcols].astype(BF16)
    la = la_ref[rows, kcols]

    ri = lax.broadcasted_iota(jnp.int32, (c, c), 0)
    ci = lax.broadcasted_iota(jnp.int32, (c, c), 1)
    before = (ci >= ri) if rev else (ci <= ri)
    tri = jnp.where(before, 1.0, 0.0).astype(BF16)
    la_hi = la.astype(BF16)
    rem = la - la_hi.astype(F32)
    la_mid = rem.astype(BF16)
    la_lo = (rem - la_mid.astype(F32)).astype(BF16)
    cum = (_dot(tri, la_hi) + _dot(tri, la_mid) + _dot(tri, la_lo)) * LOG2E
    cum_ref[...] = cum
    kc_ref[...] = k
    yield
    last = 0 if rev else c - 1
    total = cum[last:last + 1, :]

    s = s_ref[...]
    o = _dot((q * jnp.exp2(cum)).astype(BF16), s.astype(BF16))

    qparts, kparts = [], []
    for l0, l1, e0, e1, ref in _gla_pairs(rev):
        cr = cum[ref:ref + 1, :]
        qparts.append(_place_rows(q[l0:l1] * jnp.exp2(cum[l0:l1] - cr), l0, c))
        kparts.append(_place_rows(k[e0:e1] * jnp.exp2(cr - cum[e0:e1]), e0, c))
    s_off = lax.dot_general(jnp.concatenate(qparts, axis=1).astype(BF16),
                            jnp.concatenate(kparts, axis=1).astype(BF16),
                            (((1,), (1,)), ((), ())), preferred_element_type=F32)

    parts = []
    for jj in range(sub):
        cj = jnp.concatenate(
            [jnp.broadcast_to(cum_ref[a * sub + jj:a * sub + jj + 1, :], (sub, GLA_DK)) for a in range(nsub)],
            axis=0)
        kj = jnp.concatenate(
            [jnp.broadcast_to(kc_ref[a * sub + jj:a * sub + jj + 1, :], (sub, GLA_DK)) for a in range(nsub)],
            axis=0)
        parts.append((q * jnp.exp2(jnp.minimum(cum - cj, 0.0)) * kj).astype(BF16))
    diag = _dot(jnp.concatenate(parts, axis=1), e_ref[...])
    yield

    lag = (ci - ri) if rev else (ri - ci)
    room = (sub - 1 - ri % sub) if rev else (ri % sub)
    in_diag = lag.astype(jnp.uint32) <= room.astype(jnp.uint32)
    p = jnp.where(in_diag, diag, 0.0) + s_off
    o_ref[rows, vcols] = o + _dot(p.astype(BF16), v)

    k_out = (k * jnp.exp2(total - cum)).astype(BF16)
    upd = lax.dot_general(k_out, v, (((0,), (0,)), ((), ())), preferred_element_type=F32)
    dmat = jnp.broadcast_to(jnp.exp2(total), (GLA_DK, GLA_DK)).T
    for half in range(GLA_DV // GLA_DK):
        cols = slice(half * GLA_DK, (half + 1) * GLA_DK)
        s_ref[:, cols] = s[:, cols] * dmat + upd[:, cols]


def _gla_kernel(qf, kf, vf, laf, qb, kb, vb, lab, e_ref, of_ref, ob_ref, *scratch):
    n = 2 * GLA_HEADS
    states, cums, keys = scratch[:n], scratch[n:2 * n], scratch[2 * n:]

    @pl.when(pl.program_id(1) == 0)
    def _():
        for s_ref in states:
            s_ref[...] = jnp.zeros_like(s_ref)

    nchunk = ROW_TILE // GLA_CHUNK

    def body(c, carry):
        r_f = pl.multiple_of(c * GLA_CHUNK, GLA_CHUNK)
        r_b = pl.multiple_of((nchunk - 1 - c) * GLA_CHUNK, GLA_CHUNK)
        units = []
        for h in range(GLA_HEADS):
            f, b = h, GLA_HEADS + h
            units.append(_gla_chunk(qf, kf, vf, laf, e_ref, of_ref, states[f], cums[f], keys[f], r_f, h, False))
            units.append(_gla_chunk(qb, kb, vb, lab, e_ref, ob_ref, states[b], cums[b], keys[b], r_b, h, True))
        for _ in range(GLA_STAGES):
            for u in units:
                next(u, None)
        return carry

    lax.fori_loop(0, nchunk, body, 0)


def _gla(z, la, *, batch, lc, l, col_q):
    t_rows = z.shape[0]
    nctx = lc // ROW_TILE
    nlat = l // ROW_TILE
    steps = nctx + nlat
    qb0 = col_q // W_Q
    kb0 = (col_q + W_Q) // W_Q
    vb0 = (col_q + 2 * W_Q) // W_V

    def fwd_tile(b, t):
        return jnp.where(t < nctx, b * nctx + t, batch * nctx + b * nlat + (t - nctx))

    def bwd_tile(b, t):
        return jnp.where(t < nctx, b * nctx + (nctx - 1 - t), batch * nctx + b * nlat + (steps - 1 - t))

    def spec(width, tile_fn, col):
        return pl.BlockSpec((ROW_TILE, width), lambda b, t: (tile_fn(b, t), col))

    kk = jnp.arange(GLA_SUB * GLA_DK, dtype=jnp.int32)[:, None] // GLA_DK
    cc = jnp.arange(GLA_CHUNK, dtype=jnp.int32)[None, :] % GLA_SUB
    e_mat = (kk == cc).astype(BF16)

    o_shape = jax.ShapeDtypeStruct((t_rows, W_V), F32)
    n_units = 2 * GLA_HEADS
    state = pltpu.VMEM((GLA_DK, GLA_DV), F32)
    chunk = pltpu.VMEM((GLA_CHUNK, GLA_DK), F32)
    return pl.pallas_call(
        _gla_kernel,
        grid=(batch, steps),
        in_specs=[
            spec(W_Q, fwd_tile, qb0), spec(W_Q, fwd_tile, kb0), spec(W_V, fwd_tile, vb0), spec(W_Q, fwd_tile, 0),
            spec(W_Q, bwd_tile, qb0), spec(W_Q, bwd_tile, kb0), spec(W_V, bwd_tile, vb0), spec(W_Q, bwd_tile, 1),
            pl.BlockSpec((GLA_SUB * GLA_DK, GLA_CHUNK), lambda b, t: (0, 0)),
        ],
        out_specs=[spec(W_V, fwd_tile, 0), spec(W_V, bwd_tile, 0)],
        out_shape=[o_shape, o_shape],
        scratch_shapes=[state] * n_units + [chunk] * (2 * n_units),
        compiler_params=_cparams("parallel", "arbitrary"),
        name="gla_scan",
    )(z, z, z, la, z, z, z, la, e_mat)


def _dft_tables(n):
    idx = jnp.arange(n, dtype=jnp.int32)
    ang = ((idx[:, None] * idx[None, :]) % n).astype(F32) * (2.0 * math.pi / n)
    scale = n ** -0.5
    return (jnp.cos(ang) * scale).astype(BF16), (jnp.sin(ang) * scale).astype(BF16)


def _ft_chan_kernel(f_ref, cs_ref, xc_ref, xs_ref):
    for g in range(FT_GROUPS):
        cols = slice(g * FT_CH, (g + 1) * FT_CH)
        y = _dot(f_ref[:, cols].astype(BF16), cs_ref[...])
        xc_ref[:, cols] = y[:, :FT_CH].astype(BF16)
        xs_ref[:, cols] = y[:, FT_CH:].astype(BF16)


def _ft_channel(z, cs, *, col_f):
    t_rows = z.shape[0]
    tm = 1024 if t_rows % 1024 == 0 else ROW_TILE
    shape = jax.ShapeDtypeStruct((t_rows, W_FT), BF16)
    return pl.pallas_call(
        _ft_chan_kernel,
        grid=(t_rows // tm,),
        in_specs=[pl.BlockSpec((tm, W_FT), lambda i: (i, col_f // W_FT)),
                  pl.BlockSpec((FT_CH, 2 * FT_CH), lambda i: (0, 0))],
        out_specs=[pl.BlockSpec((tm, W_FT), lambda i: (i, 0))] * 2,
        out_shape=[shape, shape],
        compiler_params=_cparams("parallel"),
        name="fourier_channels",
    )(z, cs)


def _ft_pos_kernel(c_ref, s_ref, xc_ref, xs_ref, o_ref):
    o_ref[...] = _dot(c_ref[...], xc_ref[...]) - _dot(s_ref[...], xs_ref[...])


def _ft_positions(cos_t, sin_t, xc, xs, *, batch, seq):
    nt = seq // ROW_TILE
    x_spec = pl.BlockSpec((seq, W_FT), lambda b, j: (b, 0))
    t_spec = pl.BlockSpec((ROW_TILE, seq), lambda b, j: (j, 0))
    return pl.pallas_call(
        _ft_pos_kernel,
        grid=(batch, nt),
        in_specs=[t_spec, t_spec, x_spec, x_spec],
        out_specs=pl.BlockSpec((ROW_TILE, W_FT), lambda b, j: (b * nt + j, 0)),
        out_shape=jax.ShapeDtypeStruct((batch * seq, W_FT), F32),
        compiler_params=_cparams("parallel", "parallel"),
        name="fourier_positions",
    )(cos_t, sin_t, xc, xs)


def _branch_kernel(gg_ref, gs_ref, gf_ref, of_ref, ob_ref, r_ref, su_ref, sv_ref, yft_ref,
                   gn_ref, ws_ref, bs_ref, pg_ref, ps_ref, pf_ref, m_ref):
    o = of_ref[...] + ob_ref[...]
    heads = []
    for h in range(GLA_HEADS):
        oh = o[:, h * GLA_DV:(h + 1) * GLA_DV]
        heads.append(oh * lax.rsqrt(jnp.mean(oh * oh, axis=-1, keepdims=True) + EPS))
    r = r_ref[...]
    y_gla = (jnp.concatenate(heads, axis=-1) * gn_ref[...]) * (r * _sigmoid(r))
    acc = _sigmoid(gg_ref[...]) * _dot(y_gla.astype(BF16), pg_ref[...])

    u = _gelu_tanh(su_ref[...])
    v = _gelu_tanh(sv_ref[...])
    groups = []
    for g in range(SG_GROUPS):
        vg = v[:, g * SG_CH:(g + 1) * SG_CH]
        mu = jnp.mean(vg, axis=-1, keepdims=True)
        var = jnp.mean(jnp.square(vg - mu), axis=-1, keepdims=True)
        vn = ((vg - mu) * lax.rsqrt(var + EPS)).astype(BF16)
        chunks = []
        for c in range(ROW_TILE // SG_CHUNK):
            chunks.append(_dot(ws_ref[g], vn[c * SG_CHUNK:(c + 1) * SG_CHUNK, :]) + bs_ref[g])
        groups.append(jnp.concatenate(chunks, axis=0))
    y_sg = u * jnp.concatenate(groups, axis=-1)
    acc = acc + _sigmoid(gs_ref[...]) * _dot(y_sg.astype(BF16), ps_ref[...])

    acc = acc + _sigmoid(gf_ref[...]) * _dot(yft_ref[...].astype(BF16), pf_ref[...])
    m_ref[...] = acc.astype(BF16)


def _branches(z, o_f, o_b, y_ft, gla_norm, w_s, b_s, p_gla, p_sg, p_ft, *, cols):
    t_rows = z.shape[0]
    d = p_gla.shape[1]
    tm = ROW_TILE

    def zspec(width, col0):
        return pl.BlockSpec((tm, width), lambda i: (i, col0 // width))

    def rows(width):
        return pl.BlockSpec((tm, width), lambda i: (i, 0))

    def const(shape):
        return pl.BlockSpec(shape, lambda i: (0,) * len(shape))

    return pl.pallas_call(
        _branch_kernel,
        grid=(t_rows // tm,),
        in_specs=[
            zspec(d, cols["gates"]), zspec(d, cols["gates"] + d), zspec(d, cols["gates"] + 2 * d),
            rows(W_V), rows(W_V), zspec(W_V, cols["r"]), zspec(W_SG, cols["su"]), zspec(W_SG, cols["sv"]),
            rows(W_FT),
            const((1, W_V)), const((SG_GROUPS, SG_CHUNK, SG_CHUNK)), const((SG_GROUPS, SG_CHUNK, SG_CH)),
            const((W_V, d)), const((W_SG, d)), const((W_FT, d)),
        ],
        out_specs=rows(d),
        out_shape=jax.ShapeDtypeStruct((t_rows, d), BF16),
        compiler_params=_cparams("parallel"),
        name="branch_merge",
    )(z, z, z, o_f, o_b, z, z, z, y_ft, gla_norm, w_s, b_s, p_gla, p_sg, p_ft)


def _outproj_kernel(m_ref, x_ref, w_ref, g1_ref, n2_ref, sh_ref, sc_ref, wr_ref, br_ref,
                    xo_ref, h_ref, lg_ref, *, n_ctx_tiles, tiles_per_batch, ctx_row):
    row = _mod_row(pl.program_id(0), n_ctx_tiles, tiles_per_batch, ctx_row)
    x = x_ref[...] + g1_ref[pl.ds(row, 1), :] * _dot(m_ref[...], w_ref[...])
    xo_ref[...] = x
    y = x * lax.rsqrt(jnp.mean(x * x, axis=-1, keepdims=True) + EPS) * n2_ref[...]
    h = y * (1.0 + sc_ref[pl.ds(row, 1), :]) + sh_ref[pl.ds(row, 1), :]
    h_hi = h.astype(BF16)
    h_ref[...] = _pack_bf16_pairs(h_hi)
    h_lo = (h - h_hi.astype(F32)).astype(BF16)
    w = wr_ref[...]
    w_hi = w.astype(BF16)
    w_lo = (w - w_hi.astype(F32)).astype(BF16)
    lg_ref[...] = _dot(h_hi, w_hi) + (_dot(h_hi, w_lo) + _dot(h_lo, w_hi)) + br_ref[...]


def _out_project(merged, x, w_out, mod, norm2, w_router, b_router, *, n_ctx_tiles, tiles_per_batch, ctx_row):
    t_rows, d = x.shape
    tm = ROW_TILE
    kern = functools.partial(_outproj_kernel, n_ctx_tiles=n_ctx_tiles,
                             tiles_per_batch=tiles_per_batch, ctx_row=ctx_row)

    def rows(width):
        return pl.BlockSpec((tm, width), lambda i: (i, 0))

    def modspec(k):
        return pl.BlockSpec((8, d), lambda i: (0, k))

    return pl.pallas_call(
        kern,
        grid=(t_rows // tm,),
        in_specs=[rows(d), rows(d), pl.BlockSpec((d, d), lambda i: (0, 0)),
                  modspec(2), pl.BlockSpec((1, d), lambda i: (0, 0)), modspec(3), modspec(4),
                  pl.BlockSpec((d, LANES), lambda i: (0, 0)), pl.BlockSpec((1, LANES), lambda i: (0, 0))],
        out_specs=[rows(d), rows(d // 2), rows(LANES)],
        out_shape=[jax.ShapeDtypeStruct((t_rows, d), F32), jax.ShapeDtypeStruct((t_rows, d // 2), jnp.uint32),
                   jax.ShapeDtypeStruct((t_rows, LANES), F32)],
        compiler_params=_cparams("parallel"),
        name="out_project",
    )(merged, x, w_out, mod, norm2, mod, mod, w_router, b_router)


def _expert_kernel(be_ref, jb_ref, nblk_ref, ord_ref, nxt_ref, nu_ref,
                   x_ref, wgu_hbm, bgu_ref, wd_hbm, bd_ref, y_ref,
                   wgu_bf, wd_bf, stage, sems, *, layer):
    i = pl.program_id(0)
    n_gu_chunks = wgu_bf.shape[1] // W_CHUNK
    n_chunks = n_gu_chunks + wd_bf.shape[1] // W_CHUNK

    def chunk_copy(e, q):
        if q < n_gu_chunks:
            src = wgu_hbm.at[layer, e, pl.ds(q * W_CHUNK, W_CHUNK), :]
        else:
            src = wd_hbm.at[layer, e, pl.ds((q - n_gu_chunks) * W_CHUNK, W_CHUNK), :]
        return pltpu.make_async_copy(src, stage.at[q % 2], sems.at[q % 2])

    def land(e, q, slot):
        chunk_copy(e, q).wait()
        w = stage[q % 2].astype(BF16)
        if q < n_gu_chunks:
            wgu_bf[slot, pl.ds(q * W_CHUNK, W_CHUNK), :] = w
        else:
            wd_bf[slot, pl.ds((q - n_gu_chunks) * W_CHUNK, W_CHUNK), :] = w
        if q + 2 < n_chunks:
            chunk_copy(e, q + 2).start()

    active = i < nu_ref[0]
    e = be_ref[i]
    j = jb_ref[i]
    n = nblk_ref[i]
    slot = ord_ref[i] % 2
    nxt = nxt_ref[i]

    @pl.when(i == 0)
    def _():
        chunk_copy(e, 0).start()
        chunk_copy(e, 1).start()
        for q in range(n_chunks):
            land(e, q, slot)

    prefetch = jnp.logical_and(active, nxt >= 0)

    @pl.when(jnp.logical_and(prefetch, j == 0))
    def _():
        chunk_copy(nxt, 0).start()
        chunk_copy(nxt, 1).start()

    @pl.when(active)
    def _():
        gu = _dot(_unpack_bf16_pairs(x_ref[...]), wgu_bf[slot]) + bgu_ref[0, 0]
        de = gu.shape[1] // 2
        gate = jnp.minimum(gu[:, :de], SWIGLU_LIMIT)
        up = jnp.clip(gu[:, de:], -SWIGLU_LIMIT, SWIGLU_LIMIT)
        hdn = (up + 1.0) * (gate * _sigmoid(SWIGLU_ALPHA * gate))
        y_ref[...] = _dot(hdn.astype(BF16), wd_bf[slot]) + bd_ref[0, 0]

    @pl.when(jnp.logical_not(active))
    def _():
        y_ref[...] = jnp.zeros_like(y_ref)

    lo = (n_chunks * j) // n
    hi = (n_chunks * (j + 1)) // n
    for q in range(n_chunks):
        @pl.when(jnp.logical_and(prefetch, jnp.logical_and(lo <= q, q < hi)))
        def _(q=q):
            land(nxt, q, 1 - slot)


def _experts(xs, sched, layer, w_gu, b_gu, w_down, b_down):
    rows_total = xs.shape[0]
    nb = rows_total // MOE_BLOCK
    depth, ne, d, n_gu = w_gu.shape
    assert xs.shape[1] * 2 == d and xs.dtype == jnp.uint32
    de = w_down.shape[2]
    assert n_gu == d and d % W_CHUNK == 0 and de % W_CHUNK == 0

    def blk(i, *s):
        return jnp.minimum(i, s[-1][0] - 1)

    def bspec(n):
        return pl.BlockSpec((1, 1, 1, n), lambda i, *s: (layer, s[0][blk(i, *s)], 0, 0))

    grid_spec = pltpu.PrefetchScalarGridSpec(
        num_scalar_prefetch=len(sched),
        grid=(nb,),
        in_specs=[
            pl.BlockSpec((MOE_BLOCK, d // 2), lambda i, *s: (blk(i, *s), 0)),
            pl.BlockSpec(memory_space=pl.ANY), bspec(n_gu),
            pl.BlockSpec(memory_space=pl.ANY), bspec(d),
        ],
        out_specs=pl.BlockSpec((MOE_BLOCK, d), lambda i, *s: (i, 0)),
        scratch_shapes=[pltpu.VMEM((2, d, n_gu), BF16), pltpu.VMEM((2, de, d), BF16),
                        pltpu.VMEM((2, W_CHUNK, d), F32), pltpu.SemaphoreType.DMA((2,))],
    )
    return pl.pallas_call(
        functools.partial(_expert_kernel, layer=layer),
        grid_spec=grid_spec,
        out_shape=jax.ShapeDtypeStruct((rows_total, d), F32),
        compiler_params=_cparams("arbitrary"),
        name="moe_experts",
    )(*sched, xs, w_gu, b_gu.reshape(depth, ne, 1, n_gu), w_down, b_down.reshape(depth, ne, 1, d))


def _route(logits, n_tok):
    top_val, top_idx = lax.top_k(logits, TOP_K)
    probs = jax.nn.softmax(top_val, axis=-1)
    m = n_tok * TOP_K
    flat_e = top_idx.reshape(m).astype(jnp.int32)
    onehot = (flat_e[:, None] == jnp.arange(N_EXPERTS, dtype=jnp.int32)[None, :]).astype(jnp.int32)
    csum = jnp.cumsum(onehot, axis=0)
    counts = csum[-1]
    rank = jnp.sum(csum * onehot, axis=1) - 1
    pad_counts = (counts + MOE_BLOCK - 1) // MOE_BLOCK * MOE_BLOCK
    pad_end = jnp.cumsum(pad_counts)
    pad_start = pad_end - pad_counts
    pos = pad_start[flat_e] + rank
    nb = (m + MOE_BLOCK - 1) // MOE_BLOCK + N_EXPERTS
    idx_bits = (m - 1).bit_length()
    assert (2 * N_EXPERTS + 2) << idx_bits < 2 ** 31
    experts = jnp.arange(N_EXPERTS, dtype=jnp.int32)
    real_keys = ((2 * flat_e) << idx_bits) | jnp.arange(m, dtype=jnp.int32)
    n_fill = nb * MOE_BLOCK - m
    fill_e = jnp.arange(n_fill, dtype=jnp.int32) // (n_fill // N_EXPERTS)
    fill_i = jnp.arange(n_fill, dtype=jnp.int32) % (n_fill // N_EXPERTS)
    assert n_fill % N_EXPERTS == 0 and n_fill // N_EXPERTS >= MOE_BLOCK - 1
    fill_keys = jnp.where(fill_i < (pad_counts - counts)[fill_e],
                          (2 * fill_e + 1) << idx_bits, (2 * N_EXPERTS + 1) << idx_bits)
    keys = lax.sort(jnp.concatenate([real_keys, fill_keys]))
    is_real = ((keys >> idx_bits) & 1) == 0
    row_tok = jnp.where(is_real, (keys & ((1 << idx_bits) - 1)) // TOP_K, 0)
    blk_start = jnp.arange(nb, dtype=jnp.int32) * MOE_BLOCK
    blk_e = jnp.minimum(jnp.sum((pad_end[None, :] <= blk_start[:, None]).astype(jnp.int32), axis=1),
                        N_EXPERTS - 1)
    n_used = (pad_end[-1] // MOE_BLOCK).astype(jnp.int32).reshape(1)
    has_blocks = pad_counts > 0
    order_e = jnp.cumsum(has_blocks.astype(jnp.int32)) - 1
    later = jnp.logical_and(has_blocks[None, :], experts[None, :] > experts[:, None])
    next_e = jnp.min(jnp.where(later, experts[None, :], N_EXPERTS), axis=1)
    next_e = jnp.where(next_e == N_EXPERTS, -1, next_e)
    blk_j = (blk_start - pad_start[blk_e]) // MOE_BLOCK
    blk_n = jnp.maximum(pad_counts[blk_e] // MOE_BLOCK, 1)
    sched = (blk_e, blk_j, blk_n, order_e[blk_e], next_e[blk_e], n_used)
    return probs, pos.reshape(n_tok, TOP_K), row_tok, sched


def _rows(x, idx):
    return x.at[idx].get(mode="promise_in_bounds")


def _moe(h2, logits, layer, w_gu, b_gu, w_down, b_down):
    n_tok = h2.shape[0]
    probs, pos, row_tok, sched = _route(logits[:, :N_EXPERTS], n_tok)
    xs = _rows(h2, row_tok)
    yb = _experts(xs, sched, layer, w_gu, b_gu, w_down, b_down)
    y = _rows(yb, pos.T.reshape(-1)).reshape(TOP_K, n_tok, -1)
    return y, probs


def _combine_kernel(x_ref, y0_ref, y1_ref, y2_ref, y3_ref, p_ref, g2_ref, nf_ref, o_ref,
                    *, tile0, final, n_ctx_tiles, tiles_per_batch, ctx_row):
    row = _mod_row(pl.program_id(0) + tile0, n_ctx_tiles, tiles_per_batch, ctx_row)
    p = p_ref[...]
    y = (p[:, 0:1] * y0_ref[0] + p[:, 1:2] * y1_ref[0]) + (p[:, 2:3] * y2_ref[0] + p[:, 3:4] * y3_ref[0])
    x = x_ref[...] + g2_ref[pl.ds(row, 1), :] * y
    if final:
        x = x * lax.rsqrt(jnp.mean(x * x, axis=-1, keepdims=True) + EPS) * nf_ref[...]
    o_ref[...] = x


def _combine(x, y, probs, mod, norm_f, *, tile0, final, n_ctx_tiles, tiles_per_batch, ctx_row):
    assert y.shape[0] == TOP_K == 4
    _, n_rows, d = y.shape
    tm = ROW_TILE
    kern = functools.partial(_combine_kernel, tile0=tile0, final=final, n_ctx_tiles=n_ctx_tiles,
                             tiles_per_batch=tiles_per_batch, ctx_row=ctx_row)

    def yspec(k):
        return pl.BlockSpec((1, tm, d), lambda i: (k, i, 0))

    return pl.pallas_call(
        kern,
        grid=(n_rows // tm,),
        in_specs=[pl.BlockSpec((tm, d), lambda i: (tile0 + i, 0)),
                  yspec(0), yspec(1), yspec(2), yspec(3),
                  pl.BlockSpec((tm, TOP_K), lambda i: (i, 0)),
                  pl.BlockSpec((8, d), lambda i: (0, 5)),
                  pl.BlockSpec((1, d), lambda i: (0, 0))],
        out_specs=pl.BlockSpec((tm, d), lambda i: (i, 0)),
        out_shape=jax.ShapeDtypeStruct((n_rows, d), F32),
        compiler_params=_cparams("parallel"),
        name="moe_combine",
    )(x, y, y, y, y, probs, mod, norm_f)


def _grid_pos_embed(n, d):
    rows = n // GRID_W
    row = jnp.broadcast_to(jnp.arange(rows, dtype=F32)[:, None], (rows, GRID_W)).reshape(-1)
    col = jnp.broadcast_to(jnp.arange(GRID_W, dtype=F32)[None, :], (rows, GRID_W)).reshape(-1)
    quarter = d // 4
    omega = 1.0 / (10000.0 ** (jnp.arange(quarter, dtype=F32) / quarter))

    def enc(p):
        a = p[:, None] * omega
        return jnp.concatenate([jnp.sin(a), jnp.cos(a)], axis=-1)

    return jnp.concatenate([enc(row), enc(col)], axis=-1)


def kernel(x, c, ctx, c_ctx, w_mod, b_mod, norm1, w_in, w_a2, b_a, gla_norm, w_s, b_s, p_gla, p_sg, p_ft,
           w_out, norm2, w_router, b_router, w_gu, b_gu, w_down, b_down, norm_f):
    batch, l, d = x.shape
    lc = ctx.shape[1]
    depth = w_mod.shape[0]
    n_ctx_rows = batch * lc
    t_rows = n_ctx_rows + batch * l
    assert lc % ROW_TILE == 0 and l % ROW_TILE == 0 and batch < 8

    tm1 = math.gcd(math.gcd(n_ctx_rows, l), 1024)
    assert tm1 % ROW_TILE == 0
    proj_tiles = dict(n_ctx_tiles=n_ctx_rows // tm1, tiles_per_batch=l // tm1, ctx_row=batch)
    row_tiles = dict(n_ctx_tiles=n_ctx_rows // ROW_TILE, tiles_per_batch=l // ROW_TILE, ctx_row=batch)

    o_q = 0
    o_k = o_q + W_Q
    o_v = o_k + W_Q
    o_r = o_v + W_V
    o_a = o_r + W_V
    o_su = o_a + 2 * GLA_RANK
    o_sv = o_su + W_SG
    o_f = o_sv + W_SG
    o_g = o_f + W_FT
    n_gates = 3 * d
    cols = dict(gates=0, q=n_gates, r=n_gates + 2 * W_Q + W_V, su=n_gates + 2 * W_Q + 2 * W_V)
    cols["sv"] = cols["su"] + W_SG
    cols["f"] = cols["sv"] + W_SG
    n_main = cols["f"] + W_FT

    xt = jnp.concatenate([ctx.reshape(n_ctx_rows, d),
                          (x + _grid_pos_embed(l, d)[None]).reshape(batch * l, d)], axis=0)

    cond = jnp.zeros((8, d), F32).at[:batch].set(c).at[batch].set(c_ctx)
    mods = _mod_vectors(cond, w_mod, b_mod)

    cc, sc = _dft_tables(FT_CH)
    chan_tab = jnp.concatenate([cc, sc], axis=1)
    cos_l, sin_l = _dft_tables(l)
    cos_c, sin_c = _dft_tables(lc)

    for i in range(depth):
        last = i == depth - 1
        wi = w_in[i]
        w_main = jnp.concatenate(
            [wi[:, o_g:], wi[:, o_q:o_a], wi[:, o_su:o_g]], axis=1).astype(BF16)
        w_a = jnp.pad(wi[:, o_a:o_su], ((0, 0), (0, LANES - 2 * GLA_RANK))).astype(BF16)
        wa2 = jnp.zeros((LANES, 2 * W_Q), F32)
        wa2 = wa2.at[:GLA_RANK, :W_Q].set(w_a2[i, 0]).at[GLA_RANK:2 * GLA_RANK, W_Q:].set(w_a2[i, 1])
        mod = mods[i]

        z, la = _project(xt, mod, norm1[i][None], w_main, w_a, wa2.astype(BF16), b_a[i].reshape(1, 2 * W_Q),
                         tm=tm1, tn=768, **proj_tiles)
        assert z.shape[1] == n_main

        o_fw, o_bw = _gla(z, la, batch=batch, lc=lc, l=l, col_q=cols["q"])

        xc, xs = _ft_channel(z, chan_tab, col_f=cols["f"])
        y_ft = jnp.concatenate([
            _ft_positions(cos_c, sin_c, xc[:n_ctx_rows], xs[:n_ctx_rows], batch=batch, seq=lc),
            _ft_positions(cos_l, sin_l, xc[n_ctx_rows:], xs[n_ctx_rows:], batch=batch, seq=l),
        ], axis=0)

        bs_b = jnp.broadcast_to(b_s[i][:, :, None], (SG_GROUPS, SG_CHUNK, SG_CH))
        merged = _branches(z, o_fw, o_bw, y_ft, gla_norm[i][None], w_s[i].astype(BF16), bs_b,
                           p_gla[i].astype(BF16), p_sg[i].astype(BF16), p_ft[i].astype(BF16), cols=cols)

        w_r = jnp.pad(w_router[i], ((0, 0), (0, LANES - N_EXPERTS)))
        b_r = jnp.pad(b_router[i], (0, LANES - N_EXPERTS)).reshape(1, LANES)
        xt, h2, logits = _out_project(merged, xt, w_out[i].astype(BF16), mod, norm2[i][None], w_r, b_r,
                                      **row_tiles)

        row0 = n_ctx_rows if last else 0
        y, probs = _moe(h2[row0:], logits[row0:], i, w_gu, b_gu, w_down, b_down)
        xt = _combine(xt, y, probs, mod, norm_f[None], tile0=row0 // ROW_TILE, final=last, **row_tiles)

    return xt.reshape(batch, l, d)
```

```python
import functools
import math

import jax
import jax.numpy as jnp
from jax import lax
from jax.experimental import pallas as pl
from jax.experimental.pallas import tpu as pltpu

F32 = jnp.float32
BF16 = jnp.bfloat16

EPS = 1e-6
GRID_W = 64
GLA_HEADS = 4
GLA_DK = 128
GLA_DV = 256
GLA_RANK = 16
GLA_TAU = 16.0
GLA_CHUNK = 64
GLA_SUB = 16
GLA_STAGES = 3
SG_GROUPS = 4
SG_CH = 128
SG_CHUNK = 128
FT_GROUPS = 4
FT_CH = 128
N_EXPERTS = 32
TOP_K = 4
SWIGLU_LIMIT = 7.0
SWIGLU_ALPHA = 1.702
MOE_BLOCK = 256
W_CHUNK = 512

LOG2E = math.log2(math.e)
LANES = 128
ROW_TILE = 256
VMEM_LIMIT = 56 * 1024 * 1024

W_Q = GLA_HEADS * GLA_DK
W_V = GLA_HEADS * GLA_DV
W_SG = SG_GROUPS * SG_CH
W_FT = FT_GROUPS * FT_CH


def _cparams(*sem):
    return pltpu.CompilerParams(dimension_semantics=sem, vmem_limit_bytes=VMEM_LIMIT)


def _sigmoid(x):
    return 1.0 / (1.0 + jnp.exp(-x))


def _gelu_tanh(x):
    return 0.5 * x * (1.0 + jnp.tanh(math.sqrt(2.0 / math.pi) * (x + 0.044715 * (x * x * x))))


def _dot(a, b):
    return jnp.dot(a, b, preferred_element_type=F32)


def _pack_bf16_pairs(x):
    bits = lax.bitcast_convert_type(x.astype(F32), jnp.uint32)
    half = x.shape[1] // 2
    return (bits[:, :half] >> 16) | (bits[:, half:] & jnp.uint32(0xFFFF0000))


def _unpack_bf16_pairs(u):
    lo = lax.bitcast_convert_type(u << 16, F32)
    hi = lax.bitcast_convert_type(u & jnp.uint32(0xFFFF0000), F32)
    return jnp.concatenate([lo, hi], axis=1).astype(BF16)


def _mod_row(tile, n_ctx_tiles, tiles_per_batch, ctx_row):
    return jnp.where(tile < n_ctx_tiles, ctx_row, (tile - n_ctx_tiles) // tiles_per_batch)


def _mod_kernel(c_ref, w_ref, b_ref, o_ref):
    c = c_ref[...]
    s = (c * _sigmoid(c)).astype(BF16)
    o_ref[0] = _dot(s, w_ref[0].astype(BF16)) + b_ref[0]


def _mod_vectors(cond, w_mod, b_mod):
    depth, d, n = w_mod.shape
    tn = 1024
    return pl.pallas_call(
        _mod_kernel,
        grid=(depth, n // tn),
        in_specs=[
            pl.BlockSpec((8, d), lambda l, j: (0, 0)),
            pl.BlockSpec((1, d, tn), lambda l, j: (l, 0, j)),
            pl.BlockSpec((1, 1, tn), lambda l, j: (l, 0, j)),
        ],
        out_specs=pl.BlockSpec((1, 8, tn), lambda l, j: (l, 0, j)),
        out_shape=jax.ShapeDtypeStruct((depth, 8, n), F32),
        compiler_params=_cparams("parallel", "parallel"),
        name="mod_vectors",
    )(cond, w_mod, b_mod.reshape(depth, 1, n))


def _proj_kernel(x_ref, sh_ref, sc_ref, g_ref, w_ref, wa_ref, wa2_ref, ba_ref,
                 z_ref, la_ref, a_scr, *, n_ctx_tiles, tiles_per_batch, ctx_row):
    i = pl.program_id(0)
    j = pl.program_id(1)

    @pl.when(j == 0)
    def _():
        row = _mod_row(i, n_ctx_tiles, tiles_per_batch, ctx_row)
        scale = 1.0 + sc_ref[pl.ds(row, 1), :]
        shift = sh_ref[pl.ds(row, 1), :]

        def body(r, carry):
            rows = pl.ds(pl.multiple_of(r * ROW_TILE, ROW_TILE), ROW_TILE)
            x = x_ref[rows, :]
            y = x * lax.rsqrt(jnp.mean(x * x, axis=-1, keepdims=True) + EPS) * g_ref[...]
            hb = (y * scale + shift).astype(BF16)
            a_scr[rows, :] = hb
            a_lr = _dot(hb, wa_ref[...])
            pre = _dot(a_lr.astype(BF16), wa2_ref[...]) + ba_ref[...]
            la_ref[rows, :] = -(jnp.maximum(-pre, 0.0) + jnp.log1p(jnp.exp(-jnp.abs(pre)))) * (1.0 / GLA_TAU)
            return carry

        lax.fori_loop(0, x_ref.shape[0] // ROW_TILE, body, 0)

    z_ref[...] = _dot(a_scr[...], w_ref[...]).astype(BF16)


def _project(x, mod, g, w_main, w_a, w_a2, b_a, *, tm, tn, n_ctx_tiles, tiles_per_batch, ctx_row):
    t, d = x.shape
    n = w_main.shape[1]
    kern = functools.partial(_proj_kernel, n_ctx_tiles=n_ctx_tiles,
                             tiles_per_batch=tiles_per_batch, ctx_row=ctx_row)
    return pl.pallas_call(
        kern,
        grid=(t // tm, n // tn),
        in_specs=[
            pl.BlockSpec((tm, d), lambda i, j: (i, 0)),
            pl.BlockSpec((8, d), lambda i, j: (0, 0)),
            pl.BlockSpec((8, d), lambda i, j: (0, 1)),
            pl.BlockSpec((1, d), lambda i, j: (0, 0)),
            pl.BlockSpec((d, tn), lambda i, j: (0, j)),
            pl.BlockSpec((d, LANES), lambda i, j: (0, 0)),
            pl.BlockSpec((LANES, 2 * W_Q), lambda i, j: (0, 0)),
            pl.BlockSpec((1, 2 * W_Q), lambda i, j: (0, 0)),
        ],
        out_specs=[
            pl.BlockSpec((tm, tn), lambda i, j: (i, j)),
            pl.BlockSpec((tm, 2 * W_Q), lambda i, j: (i, 0)),
        ],
        out_shape=[
            jax.ShapeDtypeStruct((t, n), BF16),
            jax.ShapeDtypeStruct((t, 2 * W_Q), F32),
        ],
        scratch_shapes=[pltpu.VMEM((tm, d), BF16)],
        compiler_params=_cparams("parallel", "arbitrary"),
        name="norm_mod_project",
    )(x, mod, mod, g, w_main, w_a, w_a2, b_a)


def _gla_pairs(rev):
    c, sub = GLA_CHUNK, GLA_SUB
    half = c // 2
    if rev:
        return ((0, half, half, c, half - 1),
                (0, sub, sub, half, sub - 1),
                (half, half + sub, half + sub, c, half + sub - 1))
    return ((half, c, 0, half, half),
            (sub, half, 0, sub, sub),
            (half + sub, c, half, half + sub, half + sub))


def _place_rows(x, r0, n):
    parts = []
    if r0:
        parts.append(jnp.zeros((r0, x.shape[1]), x.dtype))
    parts.append(x)
    rest = n - r0 - x.shape[0]
    if rest:
        parts.append(jnp.zeros((rest, x.shape[1]), x.dtype))
    return jnp.concatenate(parts, axis=0)


def _gla_chunk(q_ref, k_ref, v_ref, la_ref, e_ref, o_ref, s_ref, cum_ref, kc_ref, r0, h, rev):
    c, sub = GLA_CHUNK, GLA_SUB
    nsub = c // sub
    rows = pl.ds(r0, c)
    kcols = slice(h * GLA_DK, (h + 1) * GLA_DK)
    vcols = slice(h * GLA_DV, (h + 1) * GLA_DV)
    q = q_ref[rows, kcols].astype(F32) * (GLA_DK ** -0.5)
    k = k_ref[rows, kcols].astype(F32)
    v = v_ref[rows, vcols]
    la = la_ref[rows, kcols]

    ri = lax.broadcasted_iota(jnp.int32, (c, c), 0)
    ci = lax.broadcasted_iota(jnp.int32, (c, c), 1)
    before = (ci >= ri) if rev else (ci <= ri)
    tri = jnp.where(before, 1.0, 0.0).astype(BF16)
    la_hi = la.astype(BF16)
    rem = la - la_hi.astype(F32)
    la_mid = rem.astype(BF16)
    la_lo = (rem - la_mid.astype(F32)).astype(BF16)
    cum = (_dot(tri, la_hi) + _dot(tri, la_mid) + _dot(tri, la_lo)) * LOG2E
    cum_ref[...] = cum
    kc_ref[...] = k
    yield
    last = 0 if rev else c - 1
    total = cum[last:last + 1, :]

    s = s_ref[...]
    o = _dot((q * jnp.exp2(cum)).astype(BF16), s.astype(BF16))

    qparts, kparts = [], []
    for l0, l1, e0, e1, ref in _gla_pairs(rev):
        cr = cum[ref:ref + 1, :]
        qparts.append(_place_rows(q[l0:l1] * jnp.exp2(cum[l0:l1] - cr), l0, c))
        kparts.append(_place_rows(k[e0:e1] * jnp.exp2(cr - cum[e0:e1]), e0, c))
    s_off = lax.dot_general(jnp.concatenate(qparts, axis=1).astype(BF16),
                            jnp.concatenate(kparts, axis=1).astype(BF16),
                            (((1,), (1,)), ((), ())), preferred_element_type=F32)

    parts = []
    for jj in range(sub):
        cj = jnp.concatenate(
            [jnp.broadcast_to(cum_ref[a * sub + jj:a * sub + jj + 1, :], (sub, GLA_DK)) for a in range(nsub)],
            axis=0)
        kj = jnp.concatenate(
            [jnp.broadcast_to(kc_ref[a * sub + jj:a * sub + jj + 1, :], (sub, GLA_DK)) for a in range(nsub)],
            axis=0)
        parts.append((q * jnp.exp2(jnp.minimum(cum - cj, 0.0)) * kj).astype(BF16))
    diag = _dot(jnp.concatenate(parts, axis=1), e_ref[...])
    yield

    lag = (ci - ri) if rev else (ri - ci)
    room = (sub - 1 - ri % sub) if rev else (ri % sub)
    in_diag = lag.astype(jnp.uint32) <= room.astype(jnp.uint32)
    p = jnp.where(in_diag, diag, 0.0) + s_off
    o_ref[rows, vcols] = o + _dot(p.astype(BF16), v)

    k_out = (k * jnp.exp2(total - cum)).astype(BF16)
    upd = lax.dot_general(k_out, v, (((0,), (0,)), ((), ())), preferred_element_type=F32)
    dmat = jnp.broadcast_to(jnp.exp2(total), (GLA_DK, GLA_DK)).T
    for half in range(GLA_DV // GLA_DK):
        cols = slice(half * GLA_DK, (half + 1) * GLA_DK)
        s_ref[:, cols] = s[:, cols] * dmat + upd[:, cols]


def _gla_kernel(qf, kf, vf, laf, qb, kb, vb, lab, e_ref, of_ref, ob_ref, *scratch):
    n = 2 * GLA_HEADS
    states, cums, keys = scratch[:n], scratch[n:2 * n], scratch[2 * n:]

    @pl.when(pl.program_id(1) == 0)
    def _():
        for s_ref in states:
            s_ref[...] = jnp.zeros_like(s_ref)

    nchunk = ROW_TILE // GLA_CHUNK

    def body(c, carry):
        r_f = pl.multiple_of(c * GLA_CHUNK, GLA_CHUNK)
        r_b = pl.multiple_of((nchunk - 1 - c) * GLA_CHUNK, GLA_CHUNK)
        units = []
        for h in range(GLA_HEADS):
            f, b = h, GLA_HEADS + h
            units.append(_gla_chunk(qf, kf, vf, laf, e_ref, of_ref, states[f], cums[f], keys[f], r_f, h, False))
            units.append(_gla_chunk(qb, kb, vb, lab, e_ref, ob_ref, states[b], cums[b], keys[b], r_b, h, True))
        for _ in range(GLA_STAGES):
            for u in units:
                next(u, None)
        return carry

    lax.fori_loop(0, nchunk, body, 0)


def _gla(z, la, *, batch, lc, l, col_q):
    t_rows = z.shape[0]
    nctx = lc // ROW_TILE
    nlat = l // ROW_TILE
    steps = nctx + nlat
    qb0 = col_q // W_Q
    kb0 = (col_q + W_Q) // W_Q
    vb0 = (col_q + 2 * W_Q) // W_V

    def fwd_tile(b, t):
        return jnp.where(t < nctx, b * nctx + t, batch * nctx + b * nlat + (t - nctx))

    def bwd_tile(b, t):
        return jnp.where(t < nctx, b * nctx + (nctx - 1 - t), batch * nctx + b * nlat + (steps - 1 - t))

    def spec(width, tile_fn, col):
        return pl.BlockSpec((ROW_TILE, width), lambda b, t: (tile_fn(b, t), col))

    kk = jnp.arange(GLA_SUB * GLA_DK, dtype=jnp.int32)[:, None] // GLA_DK
    cc = jnp.arange(GLA_CHUNK, dtype=jnp.int32)[None, :] % GLA_SUB
    e_mat = (kk == cc).astype(BF16)

    o_shape = jax.ShapeDtypeStruct((t_rows, W_V), F32)
    n_units = 2 * GLA_HEADS
    state = pltpu.VMEM((GLA_DK, GLA_DV), F32)
    chunk = pltpu.VMEM((GLA_CHUNK, GLA_DK), F32)
    return pl.pallas_call(
        _gla_kernel,
        grid=(batch, steps),
        in_specs=[
            spec(W_Q, fwd_tile, qb0), spec(W_Q, fwd_tile, kb0), spec(W_V, fwd_tile, vb0), spec(W_Q, fwd_tile, 0),
            spec(W_Q, bwd_tile, qb0), spec(W_Q, bwd_tile, kb0), spec(W_V, bwd_tile, vb0), spec(W_Q, bwd_tile, 1),
            pl.BlockSpec((GLA_SUB * GLA_DK, GLA_CHUNK), lambda b, t: (0, 0)),
        ],
        out_specs=[spec(W_V, fwd_tile, 0), spec(W_V, bwd_tile, 0)],
        out_shape=[o_shape, o_shape],
        scratch_shapes=[state] * n_units + [chunk] * (2 * n_units),
        compiler_params=_cparams("parallel", "arbitrary"),
        name="gla_scan",
    )(z, z, z, la, z, z, z, la, e_mat)


def _dft_tables(n):
    scale = n ** -0.5
    p = jnp.arange(n, dtype=jnp.int32)[:, None]

    def trig(cols, period):
        ang = ((p * cols[None, :]) % period).astype(F32) * (2.0 * math.pi / period)
        return jnp.cos(ang), jnp.sin(ang)

    n2 = 1
    while n2 * n2 < n:
        n2 *= 2
    if n % n2 or n2 == 1:
        c, s = trig(jnp.arange(n, dtype=jnp.int32), n)
        return (c * scale).astype(BF16), (s * scale).astype(BF16)
    n1 = n // n2
    ca, sa = trig(jnp.arange(n1, dtype=jnp.int32), n1)
    cb, sb = trig(jnp.arange(n2, dtype=jnp.int32), n)
    ca, sa, cb, sb = ca[:, :, None], sa[:, :, None], cb[:, None, :] * scale, sb[:, None, :] * scale
    return ((ca * cb - sa * sb).reshape(n, n).astype(BF16), (sa * cb + ca * sb).reshape(n, n).astype(BF16))


def _ft_chan_kernel(f_ref, cs_ref, xc_ref, xs_ref):
    for g in range(FT_GROUPS):
        cols = slice(g * FT_CH, (g + 1) * FT_CH)
        y = _dot(f_ref[:, cols], cs_ref[...])
        xc_ref[:, cols] = y[:, :FT_CH].astype(BF16)
        xs_ref[:, cols] = y[:, FT_CH:].astype(BF16)


def _ft_channel(z, cs, *, col_f):
    t_rows = z.shape[0]
    tm = 1024 if t_rows % 1024 == 0 else ROW_TILE
    shape = jax.ShapeDtypeStruct((t_rows, W_FT), BF16)
    return pl.pallas_call(
        _ft_chan_kernel,
        grid=(t_rows // tm,),
        in_specs=[pl.BlockSpec((tm, W_FT), lambda i: (i, col_f // W_FT)),
                  pl.BlockSpec((FT_CH, 2 * FT_CH), lambda i: (0, 0))],
        out_specs=[pl.BlockSpec((tm, W_FT), lambda i: (i, 0))] * 2,
        out_shape=[shape, shape],
        compiler_params=_cparams("parallel"),
        name="fourier_channels",
    )(z, cs)


def _ft_pos_kernel(c_ref, s_ref, xc_ref, xs_ref, o_ref):
    o_ref[...] = _dot(c_ref[...], xc_ref[...]) - _dot(s_ref[...], xs_ref[...])


def _ft_positions(cos_t, sin_t, xc, xs, *, batch, seq):
    nt = seq // ROW_TILE
    x_spec = pl.BlockSpec((seq, W_FT), lambda b, j: (b, 0))
    t_spec = pl.BlockSpec((ROW_TILE, seq), lambda b, j: (j, 0))
    return pl.pallas_call(
        _ft_pos_kernel,
        grid=(batch, nt),
        in_specs=[t_spec, t_spec, x_spec, x_spec],
        out_specs=pl.BlockSpec((ROW_TILE, W_FT), lambda b, j: (b * nt + j, 0)),
        out_shape=jax.ShapeDtypeStruct((batch * seq, W_FT), F32),
        compiler_params=_cparams("parallel", "parallel"),
        name="fourier_positions",
    )(cos_t, sin_t, xc, xs)


def _branch_kernel(gg_ref, gs_ref, gf_ref, of_ref, ob_ref, r_ref, su_ref, sv_ref, yft_ref,
                   gn_ref, ws_ref, bs_ref, pg_ref, ps_ref, pf_ref, m_ref):
    o = of_ref[...] + ob_ref[...]
    heads = []
    for h in range(GLA_HEADS):
        oh = o[:, h * GLA_DV:(h + 1) * GLA_DV]
        heads.append(oh * lax.rsqrt(jnp.mean(oh * oh, axis=-1, keepdims=True) + EPS))
    r = r_ref[...].astype(F32)
    y_gla = (jnp.concatenate(heads, axis=-1) * gn_ref[...]) * (r * _sigmoid(r))
    acc = _sigmoid(gg_ref[...].astype(F32)) * _dot(y_gla.astype(BF16), pg_ref[...])

    u = _gelu_tanh(su_ref[...].astype(F32))
    v = _gelu_tanh(sv_ref[...].astype(F32))
    groups = []
    for g in range(SG_GROUPS):
        vg = v[:, g * SG_CH:(g + 1) * SG_CH]
        mu = jnp.mean(vg, axis=-1, keepdims=True)
        var = jnp.mean(jnp.square(vg - mu), axis=-1, keepdims=True)
        vn = ((vg - mu) * lax.rsqrt(var + EPS)).astype(BF16)
        chunks = []
        for c in range(v.shape[0] // SG_CHUNK):
            chunks.append(_dot(ws_ref[g], vn[c * SG_CHUNK:(c + 1) * SG_CHUNK, :]) + bs_ref[g])
        groups.append(jnp.concatenate(chunks, axis=0))
    y_sg = u * jnp.concatenate(groups, axis=-1)
    acc = acc + _sigmoid(gs_ref[...].astype(F32)) * _dot(y_sg.astype(BF16), ps_ref[...])

    acc = acc + _sigmoid(gf_ref[...].astype(F32)) * _dot(yft_ref[...].astype(BF16), pf_ref[...])
    m_ref[...] = acc.astype(BF16)


def _branches(z, o_f, o_b, y_ft, gla_norm, w_s, b_s, p_gla, p_sg, p_ft, *, cols, tm):
    t_rows = z.shape[0]
    d = p_gla.shape[1]

    def zspec(width, col0):
        return pl.BlockSpec((tm, width), lambda i: (i, col0 // width))

    def rows(width):
        return pl.BlockSpec((tm, width), lambda i: (i, 0))

    def const(shape):
        return pl.BlockSpec(shape, lambda i: (0,) * len(shape))

    return pl.pallas_call(
        _branch_kernel,
        grid=(t_rows // tm,),
        in_specs=[
            zspec(d, cols["gates"]), zspec(d, cols["gates"] + d), zspec(d, cols["gates"] + 2 * d),
            rows(W_V), rows(W_V), zspec(W_V, cols["r"]), zspec(W_SG, cols["su"]), zspec(W_SG, cols["sv"]),
            rows(W_FT),
            const((1, W_V)), const((SG_GROUPS, SG_CHUNK, SG_CHUNK)), const((SG_GROUPS, SG_CHUNK, SG_CH)),
            const((W_V, d)), const((W_SG, d)), const((W_FT, d)),
        ],
        out_specs=rows(d),
        out_shape=jax.ShapeDtypeStruct((t_rows, d), BF16),
        compiler_params=_cparams("parallel"),
        name="branch_merge",
    )(z, z, z, o_f, o_b, z, z, z, y_ft, gla_norm, w_s, b_s, p_gla, p_sg, p_ft)


def _outproj_kernel(m_ref, x_ref, w_ref, g1_ref, n2_ref, sh_ref, sc_ref, wr_ref, br_ref,
                    xo_ref, h_ref, lg_ref, *, n_ctx_tiles, tiles_per_batch, ctx_row):
    row = _mod_row(pl.program_id(0), n_ctx_tiles, tiles_per_batch, ctx_row)
    x = x_ref[...] + g1_ref[pl.ds(row, 1), :] * _dot(m_ref[...], w_ref[...])
    xo_ref[...] = x
    y = x * lax.rsqrt(jnp.mean(x * x, axis=-1, keepdims=True) + EPS) * n2_ref[...]
    h = y * (1.0 + sc_ref[pl.ds(row, 1), :]) + sh_ref[pl.ds(row, 1), :]
    h_hi = h.astype(BF16)
    h_ref[...] = _pack_bf16_pairs(h_hi)
    h_lo = (h - h_hi.astype(F32)).astype(BF16)
    w = wr_ref[...]
    w_hi = w.astype(BF16)
    w_lo = (w - w_hi.astype(F32)).astype(BF16)
    lg_ref[...] = _dot(h_hi, w_hi) + (_dot(h_hi, w_lo) + _dot(h_lo, w_hi)) + br_ref[...]


def _out_project(merged, x, w_out, mod, norm2, w_router, b_router, *, tm, n_ctx_tiles, tiles_per_batch, ctx_row):
    t_rows, d = x.shape
    kern = functools.partial(_outproj_kernel, n_ctx_tiles=n_ctx_tiles,
                             tiles_per_batch=tiles_per_batch, ctx_row=ctx_row)

    def rows(width):
        return pl.BlockSpec((tm, width), lambda i: (i, 0))

    def modspec(k):
        return pl.BlockSpec((8, d), lambda i: (0, k))

    return pl.pallas_call(
        kern,
        grid=(t_rows // tm,),
        in_specs=[rows(d), rows(d), pl.BlockSpec((d, d), lambda i: (0, 0)),
                  modspec(2), pl.BlockSpec((1, d), lambda i: (0, 0)), modspec(3), modspec(4),
                  pl.BlockSpec((d, LANES), lambda i: (0, 0)), pl.BlockSpec((1, LANES), lambda i: (0, 0))],
        out_specs=[rows(d), rows(d // 2), rows(LANES)],
        out_shape=[jax.ShapeDtypeStruct((t_rows, d), F32), jax.ShapeDtypeStruct((t_rows, d // 2), jnp.uint32),
                   jax.ShapeDtypeStruct((t_rows, LANES), F32)],
        compiler_params=_cparams("parallel"),
        name="out_project",
    )(merged, x, w_out, mod, norm2, mod, mod, w_router, b_router)


def _expert_kernel(be_ref, jb_ref, nblk_ref, ord_ref, nxt_ref, nu_ref,
                   x_ref, wgu_hbm, bgu_ref, wd_hbm, bd_ref, y_ref,
                   wgu_bf, wd_bf, stage, sems, *, layer):
    i = pl.program_id(0)
    n_gu_chunks = wgu_bf.shape[1] // W_CHUNK
    n_chunks = n_gu_chunks + wd_bf.shape[1] // W_CHUNK

    def chunk_copy(e, q):
        if q < n_gu_chunks:
            src = wgu_hbm.at[layer, e, pl.ds(q * W_CHUNK, W_CHUNK), :]
        else:
            src = wd_hbm.at[layer, e, pl.ds((q - n_gu_chunks) * W_CHUNK, W_CHUNK), :]
        return pltpu.make_async_copy(src, stage.at[q % 2], sems.at[q % 2])

    def land(e, q, slot):
        chunk_copy(e, q).wait()
        w = stage[q % 2].astype(BF16)
        if q < n_gu_chunks:
            wgu_bf[slot, pl.ds(q * W_CHUNK, W_CHUNK), :] = w
        else:
            wd_bf[slot, pl.ds((q - n_gu_chunks) * W_CHUNK, W_CHUNK), :] = w
        if q + 2 < n_chunks:
            chunk_copy(e, q + 2).start()

    active = i < nu_ref[0]
    e = be_ref[i]
    j = jb_ref[i]
    n = nblk_ref[i]
    slot = ord_ref[i] % 2
    nxt = nxt_ref[i]

    @pl.when(i == 0)
    def _():
        chunk_copy(e, 0).start()
        chunk_copy(e, 1).start()
        for q in range(n_chunks):
            land(e, q, slot)

    prefetch = jnp.logical_and(active, nxt >= 0)

    @pl.when(jnp.logical_and(prefetch, j == 0))
    def _():
        chunk_copy(nxt, 0).start()
        chunk_copy(nxt, 1).start()

    @pl.when(active)
    def _():
        gu = _dot(_unpack_bf16_pairs(x_ref[...]), wgu_bf[slot]) + bgu_ref[0, 0]
        de = gu.shape[1] // 2
        gate = jnp.minimum(gu[:, :de], SWIGLU_LIMIT)
        up = jnp.clip(gu[:, de:], -SWIGLU_LIMIT, SWIGLU_LIMIT)
        hdn = (up + 1.0) * (gate * _sigmoid(SWIGLU_ALPHA * gate))
        y_ref[...] = _dot(hdn.astype(BF16), wd_bf[slot]) + bd_ref[0, 0]

    @pl.when(jnp.logical_not(active))
    def _():
        y_ref[...] = jnp.zeros_like(y_ref)

    lo = (n_chunks * j) // n
    hi = (n_chunks * (j + 1)) // n
    for q in range(n_chunks):
        @pl.when(jnp.logical_and(prefetch, jnp.logical_and(lo <= q, q < hi)))
        def _(q=q):
            land(nxt, q, 1 - slot)


def _experts(xs, sched, layer, w_gu, b_gu, w_down, b_down):
    rows_total = xs.shape[0]
    nb = rows_total // MOE_BLOCK
    depth, ne, d, n_gu = w_gu.shape
    assert xs.shape[1] * 2 == d and xs.dtype == jnp.uint32
    de = w_down.shape[2]
    assert n_gu == d and d % W_CHUNK == 0 and de % W_CHUNK == 0

    def blk(i, *s):
        return jnp.minimum(i, s[-1][0] - 1)

    def bspec(n):
        return pl.BlockSpec((1, 1, 1, n), lambda i, *s: (layer, s[0][blk(i, *s)], 0, 0))

    grid_spec = pltpu.PrefetchScalarGridSpec(
        num_scalar_prefetch=len(sched),
        grid=(nb,),
        in_specs=[
            pl.BlockSpec((MOE_BLOCK, d // 2), lambda i, *s: (blk(i, *s), 0)),
            pl.BlockSpec(memory_space=pl.ANY), bspec(n_gu),
            pl.BlockSpec(memory_space=pl.ANY), bspec(d),
        ],
        out_specs=pl.BlockSpec((MOE_BLOCK, d), lambda i, *s: (i, 0)),
        scratch_shapes=[pltpu.VMEM((2, d, n_gu), BF16), pltpu.VMEM((2, de, d), BF16),
                        pltpu.VMEM((2, W_CHUNK, d), F32), pltpu.SemaphoreType.DMA((2,))],
    )
    return pl.pallas_call(
        functools.partial(_expert_kernel, layer=layer),
        grid_spec=grid_spec,
        out_shape=jax.ShapeDtypeStruct((rows_total, d), F32),
        compiler_params=_cparams("arbitrary"),
        name="moe_experts",
    )(*sched, xs, w_gu, b_gu.reshape(depth, ne, 1, n_gu), w_down, b_down.reshape(depth, ne, 1, d))


def _route(logits, n_tok):
    top_val, top_idx = lax.top_k(logits, TOP_K)
    probs = jax.nn.softmax(top_val, axis=-1)
    m = n_tok * TOP_K
    flat_e = top_idx.reshape(m).astype(jnp.int32)
    onehot = (flat_e[:, None] == jnp.arange(N_EXPERTS, dtype=jnp.int32)[None, :]).astype(jnp.int32)
    csum = jnp.cumsum(onehot, axis=0)
    counts = csum[-1]
    rank = jnp.sum(csum * onehot, axis=1) - 1
    pad_counts = (counts + MOE_BLOCK - 1) // MOE_BLOCK * MOE_BLOCK
    pad_end = jnp.cumsum(pad_counts)
    pad_start = pad_end - pad_counts
    pos = pad_start[flat_e] + rank
    nb = (m + MOE_BLOCK - 1) // MOE_BLOCK + N_EXPERTS
    idx_bits = (m - 1).bit_length()
    assert (2 * N_EXPERTS + 2) << idx_bits < 2 ** 31
    experts = jnp.arange(N_EXPERTS, dtype=jnp.int32)
    real_keys = ((2 * flat_e) << idx_bits) | jnp.arange(m, dtype=jnp.int32)
    n_fill = nb * MOE_BLOCK - m
    fill_e = jnp.arange(n_fill, dtype=jnp.int32) // (n_fill // N_EXPERTS)
    fill_i = jnp.arange(n_fill, dtype=jnp.int32) % (n_fill // N_EXPERTS)
    assert n_fill % N_EXPERTS == 0 and n_fill // N_EXPERTS >= MOE_BLOCK - 1
    fill_keys = jnp.where(fill_i < (pad_counts - counts)[fill_e],
                          (2 * fill_e + 1) << idx_bits, (2 * N_EXPERTS + 1) << idx_bits)
    keys = lax.sort(jnp.concatenate([real_keys, fill_keys]))
    is_real = ((keys >> idx_bits) & 1) == 0
    row_tok = jnp.where(is_real, (keys & ((1 << idx_bits) - 1)) // TOP_K, 0)
    blk_start = jnp.arange(nb, dtype=jnp.int32) * MOE_BLOCK
    blk_e = jnp.minimum(jnp.sum((pad_end[None, :] <= blk_start[:, None]).astype(jnp.int32), axis=1),
                        N_EXPERTS - 1)
    n_used = (pad_end[-1] // MOE_BLOCK).astype(jnp.int32).reshape(1)
    has_blocks = pad_counts > 0
    order_e = jnp.cumsum(has_blocks.astype(jnp.int32)) - 1
    later = jnp.logical_and(has_blocks[None, :], experts[None, :] > experts[:, None])
    next_e = jnp.min(jnp.where(later, experts[None, :], N_EXPERTS), axis=1)
    next_e = jnp.where(next_e == N_EXPERTS, -1, next_e)
    blk_j = (blk_start - pad_start[blk_e]) // MOE_BLOCK
    blk_n = jnp.maximum(pad_counts[blk_e] // MOE_BLOCK, 1)
    sched = (blk_e, blk_j, blk_n, order_e[blk_e], next_e[blk_e], n_used)
    return probs, pos.reshape(n_tok, TOP_K), row_tok, sched


def _rows(x, idx):
    return x.at[idx].get(mode="promise_in_bounds")


def _moe(h2, logits, layer, w_gu, b_gu, w_down, b_down):
    n_tok = h2.shape[0]
    probs, pos, row_tok, sched = _route(logits[:, :N_EXPERTS], n_tok)
    xs = _rows(h2, row_tok)
    yb = _experts(xs, sched, layer, w_gu, b_gu, w_down, b_down)
    y = _rows(yb, pos.T.reshape(-1)).reshape(TOP_K, n_tok, -1)
    return y, probs


def _combine_kernel(x_ref, y0_ref, y1_ref, y2_ref, y3_ref, p_ref, g2_ref, nf_ref, o_ref,
                    *, tile0, final, n_ctx_tiles, tiles_per_batch, ctx_row):
    row = _mod_row(pl.program_id(0) + tile0, n_ctx_tiles, tiles_per_batch, ctx_row)
    p = p_ref[...]
    y = (p[:, 0:1] * y0_ref[0] + p[:, 1:2] * y1_ref[0]) + (p[:, 2:3] * y2_ref[0] + p[:, 3:4] * y3_ref[0])
    x = x_ref[...] + g2_ref[pl.ds(row, 1), :] * y
    if final:
        x = x * lax.rsqrt(jnp.mean(x * x, axis=-1, keepdims=True) + EPS) * nf_ref[...]
    o_ref[...] = x


def _combine(x, y, probs, mod, norm_f, *, tile0, final, n_ctx_tiles, tiles_per_batch, ctx_row):
    assert y.shape[0] == TOP_K == 4
    _, n_rows, d = y.shape
    tm = ROW_TILE
    kern = functools.partial(_combine_kernel, tile0=tile0, final=final, n_ctx_tiles=n_ctx_tiles,
                             tiles_per_batch=tiles_per_batch, ctx_row=ctx_row)

    def yspec(k):
        return pl.BlockSpec((1, tm, d), lambda i: (k, i, 0))

    return pl.pallas_call(
        kern,
        grid=(n_rows // tm,),
        in_specs=[pl.BlockSpec((tm, d), lambda i: (tile0 + i, 0)),
                  yspec(0), yspec(1), yspec(2), yspec(3),
                  pl.BlockSpec((tm, TOP_K), lambda i: (i, 0)),
                  pl.BlockSpec((8, d), lambda i: (0, 5)),
                  pl.BlockSpec((1, d), lambda i: (0, 0))],
        out_specs=pl.BlockSpec((tm, d), lambda i: (i, 0)),
        out_shape=jax.ShapeDtypeStruct((n_rows, d), F32),
        compiler_params=_cparams("parallel"),
        name="moe_combine",
    )(x, y, y, y, y, probs, mod, norm_f)


def _grid_pos_embed(n, d):
    rows = n // GRID_W
    row = jnp.broadcast_to(jnp.arange(rows, dtype=F32)[:, None], (rows, GRID_W)).reshape(-1)
    col = jnp.broadcast_to(jnp.arange(GRID_W, dtype=F32)[None, :], (rows, GRID_W)).reshape(-1)
    quarter = d // 4
    omega = 1.0 / (10000.0 ** (jnp.arange(quarter, dtype=F32) / quarter))

    def enc(p):
        a = p[:, None] * omega
        return jnp.concatenate([jnp.sin(a), jnp.cos(a)], axis=-1)

    return jnp.concatenate([enc(row), enc(col)], axis=-1)


def kernel(x, c, ctx, c_ctx, w_mod, b_mod, norm1, w_in, w_a2, b_a, gla_norm, w_s, b_s, p_gla, p_sg, p_ft,
           w_out, norm2, w_router, b_router, w_gu, b_gu, w_down, b_down, norm_f):
    batch, l, d = x.shape
    lc = ctx.shape[1]
    depth = w_mod.shape[0]
    n_ctx_rows = batch * lc
    t_rows = n_ctx_rows + batch * l
    assert lc % ROW_TILE == 0 and l % ROW_TILE == 0 and batch < 8

    tm1 = math.gcd(math.gcd(n_ctx_rows, l), 1024)
    assert tm1 % ROW_TILE == 0
    proj_tiles = dict(n_ctx_tiles=n_ctx_rows // tm1, tiles_per_batch=l // tm1, ctx_row=batch)
    row_tiles = dict(n_ctx_tiles=n_ctx_rows // ROW_TILE, tiles_per_batch=l // ROW_TILE, ctx_row=batch)
    tm2 = ROW_TILE
    big_tiles = dict(n_ctx_tiles=n_ctx_rows // tm2, tiles_per_batch=l // tm2, ctx_row=batch)

    o_q = 0
    o_k = o_q + W_Q
    o_v = o_k + W_Q
    o_r = o_v + W_V
    o_a = o_r + W_V
    o_su = o_a + 2 * GLA_RANK
    o_sv = o_su + W_SG
    o_f = o_sv + W_SG
    o_g = o_f + W_FT
    n_gates = 3 * d
    cols = dict(gates=0, q=n_gates, r=n_gates + 2 * W_Q + W_V, su=n_gates + 2 * W_Q + 2 * W_V)
    cols["sv"] = cols["su"] + W_SG
    cols["f"] = cols["sv"] + W_SG
    n_main = cols["f"] + W_FT

    xt = jnp.concatenate([ctx.reshape(n_ctx_rows, d),
                          (x + _grid_pos_embed(l, d)[None]).reshape(batch * l, d)], axis=0)

    cond = jnp.zeros((8, d), F32).at[:batch].set(c).at[batch].set(c_ctx)
    mods = _mod_vectors(cond, w_mod, b_mod)

    cc, sc = _dft_tables(FT_CH)
    chan_tab = jnp.concatenate([cc, sc], axis=1)
    cos_l, sin_l = _dft_tables(l)
    cos_c, sin_c = _dft_tables(lc)

    for i in range(depth):
        last = i == depth - 1
        wi = w_in[i]
        w_main = jnp.concatenate(
            [wi[:, o_g:], wi[:, o_q:o_a], wi[:, o_su:o_g]], axis=1).astype(BF16)
        w_a = jnp.pad(wi[:, o_a:o_su], ((0, 0), (0, LANES - 2 * GLA_RANK))).astype(BF16)
        wa2 = jnp.zeros((LANES, 2 * W_Q), F32)
        wa2 = wa2.at[:GLA_RANK, :W_Q].set(w_a2[i, 0]).at[GLA_RANK:2 * GLA_RANK, W_Q:].set(w_a2[i, 1])
        mod = mods[i]

        z, la = _project(xt, mod, norm1[i][None], w_main, w_a, wa2.astype(BF16), b_a[i].reshape(1, 2 * W_Q),
                         tm=tm1, tn=1536, **proj_tiles)
        assert z.shape[1] == n_main

        o_fw, o_bw = _gla(z, la, batch=batch, lc=lc, l=l, col_q=cols["q"])

        xc, xs = _ft_channel(z, chan_tab, col_f=cols["f"])
        y_ft = jnp.concatenate([
            _ft_positions(cos_c, sin_c, xc[:n_ctx_rows], xs[:n_ctx_rows], batch=batch, seq=lc),
            _ft_positions(cos_l, sin_l, xc[n_ctx_rows:], xs[n_ctx_rows:], batch=batch, seq=l),
        ], axis=0)

        bs_b = jnp.broadcast_to(b_s[i][:, :, None], (SG_GROUPS, SG_CHUNK, SG_CH))
        merged = _branches(z, o_fw, o_bw, y_ft, gla_norm[i][None], w_s[i].astype(BF16), bs_b,
                           p_gla[i].astype(BF16), p_sg[i].astype(BF16), p_ft[i].astype(BF16), cols=cols, tm=tm2)

        w_r = jnp.pad(w_router[i], ((0, 0), (0, LANES - N_EXPERTS)))
        b_r = jnp.pad(b_router[i], (0, LANES - N_EXPERTS)).reshape(1, LANES)
        xt, h2, logits = _out_project(merged, xt, w_out[i].astype(BF16), mod, norm2[i][None], w_r, b_r,
                                      tm=tm2, **big_tiles)

        row0 = n_ctx_rows if last else 0
        y, probs = _moe(h2[row0:], logits[row0:], i, w_gu, b_gu, w_down, b_down)
        xt = _combine(xt, y, probs, mod, norm_f[None], tile0=row0 // ROW_TILE, final=last, **row_tiles)

    return xt.reshape(batch, l, d)
```

```python
import functools
import math

import jax
import jax.numpy as jnp
from jax import lax
from jax.experimental import pallas as pl
from jax.experimental.pallas import tpu as pltpu

F32 = jnp.float32
BF16 = jnp.bfloat16

EPS = 1e-6
GRID_W = 64
GLA_HEADS = 4
GLA_DK = 128
GLA_DV = 256
GLA_RANK = 16
GLA_TAU = 16.0
GLA_CHUNK = 64
GLA_SUB = 16
GLA_STAGES = 3
SG_GROUPS = 4
SG_CH = 128
SG_CHUNK = 128
FT_GROUPS = 4
FT_CH = 128
N_EXPERTS = 32
TOP_K = 4
SWIGLU_LIMIT = 7.0
SWIGLU_ALPHA = 1.702
MOE_BLOCK = 256
W_CHUNK = 512

LOG2E = math.log2(math.e)
LANES = 128
ROW_TILE = 256
VMEM_LIMIT = 56 * 1024 * 1024

W_Q = GLA_HEADS * GLA_DK
W_V = GLA_HEADS * GLA_DV
W_SG = SG_GROUPS * SG_CH
W_FT = FT_GROUPS * FT_CH


def _cparams(*sem):
    return pltpu.CompilerParams(dimension_semantics=sem, vmem_limit_bytes=VMEM_LIMIT)


def _sigmoid(x):
    return 1.0 / (1.0 + jnp.exp(-x))


def _gelu_tanh(x):
    return 0.5 * x * (1.0 + jnp.tanh(math.sqrt(2.0 / math.pi) * (x + 0.044715 * (x * x * x))))


def _dot(a, b):
    return jnp.dot(a, b, preferred_element_type=F32)


def _pack_bf16_pairs(x):
    bits = lax.bitcast_convert_type(x.astype(F32), jnp.uint32)
    half = x.shape[1] // 2
    return (bits[:, :half] >> 16) | (bits[:, half:] & jnp.uint32(0xFFFF0000))


def _unpack_bf16_pairs(u):
    lo = lax.bitcast_convert_type(u << 16, F32)
    hi = lax.bitcast_convert_type(u & jnp.uint32(0xFFFF0000), F32)
    return jnp.concatenate([lo, hi], axis=1).astype(BF16)


def _mod_row(tile, n_ctx_tiles, tiles_per_batch, ctx_row):
    return jnp.where(tile < n_ctx_tiles, ctx_row, (tile - n_ctx_tiles) // tiles_per_batch)


def _mod_kernel(c_ref, w_ref, b_ref, o_ref):
    c = c_ref[...]
    s = (c * _sigmoid(c)).astype(BF16)
    o_ref[0] = _dot(s, w_ref[0].astype(BF16)) + b_ref[0]


def _mod_vectors(cond, w_mod, b_mod):
    depth, d, n = w_mod.shape
    tn = 1024
    return pl.pallas_call(
        _mod_kernel,
        grid=(depth, n // tn),
        in_specs=[
            pl.BlockSpec((8, d), lambda l, j: (0, 0)),
            pl.BlockSpec((1, d, tn), lambda l, j: (l, 0, j)),
            pl.BlockSpec((1, 1, tn), lambda l, j: (l, 0, j)),
        ],
        out_specs=pl.BlockSpec((1, 8, tn), lambda l, j: (l, 0, j)),
        out_shape=jax.ShapeDtypeStruct((depth, 8, n), F32),
        compiler_params=_cparams("parallel", "parallel"),
        name="mod_vectors",
    )(cond, w_mod, b_mod.reshape(depth, 1, n))


def _proj_kernel(x_ref, sh_ref, sc_ref, g_ref, w_ref, wa_ref, wa2_ref, ba_ref,
                 z_ref, la_ref, a_scr, *, n_ctx_tiles, tiles_per_batch, ctx_row):
    i = pl.program_id(0)
    j = pl.program_id(1)

    @pl.when(j == 0)
    def _():
        row = _mod_row(i, n_ctx_tiles, tiles_per_batch, ctx_row)
        scale = 1.0 + sc_ref[pl.ds(row, 1), :]
        shift = sh_ref[pl.ds(row, 1), :]

        def body(r, carry):
            rows = pl.ds(pl.multiple_of(r * ROW_TILE, ROW_TILE), ROW_TILE)
            x = x_ref[rows, :]
            y = x * lax.rsqrt(jnp.mean(x * x, axis=-1, keepdims=True) + EPS) * g_ref[...]
            hb = (y * scale + shift).astype(BF16)
            a_scr[rows, :] = hb
            a_lr = _dot(hb, wa_ref[...])
            pre = _dot(a_lr.astype(BF16), wa2_ref[...]) + ba_ref[...]
            la_ref[rows, :] = -(jnp.maximum(-pre, 0.0) + jnp.log1p(jnp.exp(-jnp.abs(pre)))) * (1.0 / GLA_TAU)
            return carry

        lax.fori_loop(0, x_ref.shape[0] // ROW_TILE, body, 0)

    z_ref[...] = _dot(a_scr[...], w_ref[0]).astype(BF16)


def _project(x, mod, g, w_main, w_a, w_a2, b_a, *, layer, tm, tn, n_ctx_tiles, tiles_per_batch, ctx_row):
    t, d = x.shape
    n = w_main.shape[2]
    kern = functools.partial(_proj_kernel, n_ctx_tiles=n_ctx_tiles,
                             tiles_per_batch=tiles_per_batch, ctx_row=ctx_row)
    return pl.pallas_call(
        kern,
        grid=(t // tm, n // tn),
        in_specs=[
            pl.BlockSpec((tm, d), lambda i, j: (i, 0)),
            pl.BlockSpec((8, d), lambda i, j: (0, 0)),
            pl.BlockSpec((8, d), lambda i, j: (0, 1)),
            pl.BlockSpec((1, d), lambda i, j: (0, 0)),
            pl.BlockSpec((1, d, tn), lambda i, j: (layer, 0, j)),
            pl.BlockSpec((d, LANES), lambda i, j: (0, 0)),
            pl.BlockSpec((LANES, 2 * W_Q), lambda i, j: (0, 0)),
            pl.BlockSpec((1, 2 * W_Q), lambda i, j: (0, 0)),
        ],
        out_specs=[
            pl.BlockSpec((tm, tn), lambda i, j: (i, j)),
            pl.BlockSpec((tm, 2 * W_Q), lambda i, j: (i, 0)),
        ],
        out_shape=[
            jax.ShapeDtypeStruct((t, n), BF16),
            jax.ShapeDtypeStruct((t, 2 * W_Q), F32),
        ],
        scratch_shapes=[pltpu.VMEM((tm, d), BF16)],
        compiler_params=_cparams("parallel", "arbitrary"),
        name="norm_mod_project",
    )(x, mod, mod, g, w_main, w_a, w_a2, b_a)


def _gla_pairs(rev):
    c, sub = GLA_CHUNK, GLA_SUB
    half = c // 2
    if rev:
        return ((0, half, half, c, half - 1),
                (0, sub, sub, half, sub - 1),
                (half, half + sub, half + sub, c, half + sub - 1))
    return ((half, c, 0, half, half),
            (sub, half, 0, sub, sub),
            (half + sub, c, half, half + sub, half + sub))


def _place_rows(x, r0, n):
    parts = []
    if r0:
        parts.append(jnp.zeros((r0, x.shape[1]), x.dtype))
    parts.append(x)
    rest = n - r0 - x.shape[0]
    if rest:
        parts.append(jnp.zeros((rest, x.shape[1]), x.dtype))
    return jnp.concatenate(parts, axis=0)


def _gla_chunk(q_ref, k_ref, v_ref, la_ref, e_ref, o_ref, s_ref, cum_ref, kc_ref, r0, h, rev):
    c, sub = GLA_CHUNK, GLA_SUB
    nsub = c // sub
    rows = pl.ds(r0, c)
    kcols = slice(h * GLA_DK, (h + 1) * GLA_DK)
    vcols = slice(h * GLA_DV, (h + 1) * GLA_DV)
    q = q_ref[rows, kcols].astype(F32) * (GLA_DK ** -0.5)
    k = k_ref[rows, kcols].astype(F32)
    v = v_ref[rows, vcols]
    la = la_ref[rows, kcols]

    ri = lax.broadcasted_iota(jnp.int32, (c, c), 0)
    ci = lax.broadcasted_iota(jnp.int32, (c, c), 1)
    before = (ci >= ri) if rev else (ci <= ri)
    tri = jnp.where(before, 1.0, 0.0).astype(BF16)
    la_hi = la.astype(BF16)
    rem = la - la_hi.astype(F32)
    la_mid = rem.astype(BF16)
    la_lo = (rem - la_mid.astype(F32)).astype(BF16)
    cum = (_dot(tri, la_hi) + _dot(tri, la_mid) + _dot(tri, la_lo)) * LOG2E
    cum_ref[...] = cum
    kc_ref[...] = k
    yield
    last = 0 if rev else c - 1
    total = cum[last:last + 1, :]

    s = s_ref[...]
    o = _dot((q * jnp.exp2(cum)).astype(BF16), s.astype(BF16))

    qparts, kparts = [], []
    for l0, l1, e0, e1, ref in _gla_pairs(rev):
        cr = cum[ref:ref + 1, :]
        qparts.append(_place_rows(q[l0:l1] * jnp.exp2(cum[l0:l1] - cr), l0, c))
        kparts.append(_place_rows(k[e0:e1] * jnp.exp2(cr - cum[e0:e1]), e0, c))
    s_off = lax.dot_general(jnp.concatenate(qparts, axis=1).astype(BF16),
                            jnp.concatenate(kparts, axis=1).astype(BF16),
                            (((1,), (1,)), ((), ())), preferred_element_type=F32)

    parts = []
    for jj in range(sub):
        cj = jnp.concatenate(
            [jnp.broadcast_to(cum_ref[a * sub + jj:a * sub + jj + 1, :], (sub, GLA_DK)) for a in range(nsub)],
            axis=0)
        kj = jnp.concatenate(
            [jnp.broadcast_to(kc_ref[a * sub + jj:a * sub + jj + 1, :], (sub, GLA_DK)) for a in range(nsub)],
            axis=0)
        parts.append((q * jnp.exp2(jnp.minimum(cum - cj, 0.0)) * kj).astype(BF16))
    diag = _dot(jnp.concatenate(parts, axis=1), e_ref[...])
    yield

    lag = (ci - ri) if rev else (ri - ci)
    room = (sub - 1 - ri % sub) if rev else (ri % sub)
    in_diag = lag.astype(jnp.uint32) <= room.astype(jnp.uint32)
    p = jnp.where(in_diag, diag, 0.0) + s_off
    o_ref[rows, vcols] = o + _dot(p.astype(BF16), v)

    k_out = (k * jnp.exp2(total - cum)).astype(BF16)
    upd = lax.dot_general(k_out, v, (((0,), (0,)), ((), ())), preferred_element_type=F32)
    dmat = jnp.broadcast_to(jnp.exp2(total), (GLA_DK, GLA_DK)).T
    for half in range(GLA_DV // GLA_DK):
        cols = slice(half * GLA_DK, (half + 1) * GLA_DK)
        s_ref[:, cols] = s[:, cols] * dmat + upd[:, cols]


def _gla_kernel(qf, kf, vf, laf, qb, kb, vb, lab, e_ref, of_ref, ob_ref, *scratch):
    n = 2 * GLA_HEADS
    states, cums, keys = scratch[:n], scratch[n:2 * n], scratch[2 * n:]

    @pl.when(pl.program_id(1) == 0)
    def _():
        for s_ref in states:
            s_ref[...] = jnp.zeros_like(s_ref)

    nchunk = ROW_TILE // GLA_CHUNK

    def body(c, carry):
        r_f = pl.multiple_of(c * GLA_CHUNK, GLA_CHUNK)
        r_b = pl.multiple_of((nchunk - 1 - c) * GLA_CHUNK, GLA_CHUNK)
        units = []
        for h in range(GLA_HEADS):
            f, b = h, GLA_HEADS + h
            units.append(_gla_chunk(qf, kf, vf, laf, e_ref, of_ref, states[f], cums[f], keys[f], r_f, h, False))
            units.append(_gla_chunk(qb, kb, vb, lab, e_ref, ob_ref, states[b], cums[b], keys[b], r_b, h, True))
        for _ in range(GLA_STAGES):
            for u in units:
                next(u, None)
        return carry

    lax.fori_loop(0, nchunk, body, 0)


def _gla(z, la, *, batch, lc, l, col_q):
    t_rows = z.shape[0]
    nctx = lc // ROW_TILE
    nlat = l // ROW_TILE
    steps = nctx + nlat
    qb0 = col_q // W_Q
    kb0 = (col_q + W_Q) // W_Q
    vb0 = (col_q + 2 * W_Q) // W_V

    def fwd_tile(b, t):
        return jnp.where(t < nctx, b * nctx + t, batch * nctx + b * nlat + (t - nctx))

    def bwd_tile(b, t):
        return jnp.where(t < nctx, b * nctx + (nctx - 1 - t), batch * nctx + b * nlat + (steps - 1 - t))

    def spec(width, tile_fn, col):
        return pl.BlockSpec((ROW_TILE, width), lambda b, t: (tile_fn(b, t), col))

    kk = jnp.arange(GLA_SUB * GLA_DK, dtype=jnp.int32)[:, None] // GLA_DK
    cc = jnp.arange(GLA_CHUNK, dtype=jnp.int32)[None, :] % GLA_SUB
    e_mat = (kk == cc).astype(BF16)

    o_shape = jax.ShapeDtypeStruct((t_rows, W_V), F32)
    n_units = 2 * GLA_HEADS
    state = pltpu.VMEM((GLA_DK, GLA_DV), F32)
    chunk = pltpu.VMEM((GLA_CHUNK, GLA_DK), F32)
    return pl.pallas_call(
        _gla_kernel,
        grid=(batch, steps),
        in_specs=[
            spec(W_Q, fwd_tile, qb0), spec(W_Q, fwd_tile, kb0), spec(W_V, fwd_tile, vb0), spec(W_Q, fwd_tile, 0),
            spec(W_Q, bwd_tile, qb0), spec(W_Q, bwd_tile, kb0), spec(W_V, bwd_tile, vb0), spec(W_Q, bwd_tile, 1),
            pl.BlockSpec((GLA_SUB * GLA_DK, GLA_CHUNK), lambda b, t: (0, 0)),
        ],
        out_specs=[spec(W_V, fwd_tile, 0), spec(W_V, bwd_tile, 0)],
        out_shape=[o_shape, o_shape],
        scratch_shapes=[state] * n_units + [chunk] * (2 * n_units),
        compiler_params=_cparams("parallel", "arbitrary"),
        name="gla_scan",
    )(z, z, z, la, z, z, z, la, e_mat)


def _dft_tables(n):
    scale = n ** -0.5
    p = jnp.arange(n, dtype=jnp.int32)[:, None]

    def trig(cols, period):
        ang = ((p * cols[None, :]) % period).astype(F32) * (2.0 * math.pi / period)
        return jnp.cos(ang), jnp.sin(ang)

    n2 = 1
    while n2 * n2 < n:
        n2 *= 2
    if n % n2 or n2 == 1:
        c, s = trig(jnp.arange(n, dtype=jnp.int32), n)
        return (c * scale).astype(BF16), (s * scale).astype(BF16)
    n1 = n // n2
    ca, sa = trig(jnp.arange(n1, dtype=jnp.int32), n1)
    cb, sb = trig(jnp.arange(n2, dtype=jnp.int32), n)
    ca, sa, cb, sb = ca[:, :, None], sa[:, :, None], cb[:, None, :] * scale, sb[:, None, :] * scale
    return ((ca * cb - sa * sb).reshape(n, n).astype(BF16), (sa * cb + ca * sb).reshape(n, n).astype(BF16))


def _ft_chan_kernel(f_ref, cs_ref, xc_ref, xs_ref):
    for g in range(FT_GROUPS):
        cols = slice(g * FT_CH, (g + 1) * FT_CH)
        y = _dot(f_ref[:, cols], cs_ref[...])
        xc_ref[:, cols] = y[:, :FT_CH].astype(BF16)
        xs_ref[:, cols] = y[:, FT_CH:].astype(BF16)


def _ft_channel(z, cs, *, col_f, row0, n_rows):
    tm = math.gcd(math.gcd(row0, n_rows), 1024)
    shape = jax.ShapeDtypeStruct((n_rows, W_FT), BF16)
    return pl.pallas_call(
        _ft_chan_kernel,
        grid=(n_rows // tm,),
        in_specs=[pl.BlockSpec((tm, W_FT), lambda i: (row0 // tm + i, col_f // W_FT)),
                  pl.BlockSpec((FT_CH, 2 * FT_CH), lambda i: (0, 0))],
        out_specs=[pl.BlockSpec((tm, W_FT), lambda i: (i, 0))] * 2,
        out_shape=[shape, shape],
        compiler_params=_cparams("parallel"),
        name="fourier_channels",
    )(z, cs)


def _ft_pos_kernel(c_ref, s_ref, xc_ref, xs_ref, o_ref):
    o_ref[...] = _dot(c_ref[...], xc_ref[...]) - _dot(s_ref[...], xs_ref[...])


def _ft_positions(cos_t, sin_t, xc, xs, *, batch, seq):
    nt = seq // ROW_TILE
    x_spec = pl.BlockSpec((seq, W_FT), lambda b, j: (b, 0))
    t_spec = pl.BlockSpec((ROW_TILE, seq), lambda b, j: (j, 0))
    return pl.pallas_call(
        _ft_pos_kernel,
        grid=(batch, nt),
        in_specs=[t_spec, t_spec, x_spec, x_spec],
        out_specs=pl.BlockSpec((ROW_TILE, W_FT), lambda b, j: (b * nt + j, 0)),
        out_shape=jax.ShapeDtypeStruct((batch * seq, W_FT), F32),
        compiler_params=_cparams("parallel", "parallel"),
        name="fourier_positions",
    )(cos_t, sin_t, xc, xs)


def _branch_kernel(gg_ref, gs_ref, gf_ref, of_ref, ob_ref, r_ref, su_ref, sv_ref, yc_ref, yl_ref,
                   gn_ref, ws_ref, bs_ref, pg_ref, ps_ref, pf_ref, m_ref, *, n_ctx_tiles):
    o = of_ref[...] + ob_ref[...]
    heads = []
    for h in range(GLA_HEADS):
        oh = o[:, h * GLA_DV:(h + 1) * GLA_DV]
        heads.append(oh * lax.rsqrt(jnp.mean(oh * oh, axis=-1, keepdims=True) + EPS))
    r = r_ref[...].astype(F32)
    y_gla = (jnp.concatenate(heads, axis=-1) * gn_ref[...]) * (r * _sigmoid(r))
    acc = _sigmoid(gg_ref[...].astype(F32)) * _dot(y_gla.astype(BF16), pg_ref[...])

    u = _gelu_tanh(su_ref[...].astype(F32))
    v = _gelu_tanh(sv_ref[...].astype(F32))
    groups = []
    for g in range(SG_GROUPS):
        vg = v[:, g * SG_CH:(g + 1) * SG_CH]
        mu = jnp.mean(vg, axis=-1, keepdims=True)
        var = jnp.mean(jnp.square(vg - mu), axis=-1, keepdims=True)
        vn = ((vg - mu) * lax.rsqrt(var + EPS)).astype(BF16)
        chunks = []
        for c in range(v.shape[0] // SG_CHUNK):
            chunks.append(_dot(ws_ref[g], vn[c * SG_CHUNK:(c + 1) * SG_CHUNK, :]) + bs_ref[g])
        groups.append(jnp.concatenate(chunks, axis=0))
    y_sg = u * jnp.concatenate(groups, axis=-1)
    acc = acc + _sigmoid(gs_ref[...].astype(F32)) * _dot(y_sg.astype(BF16), ps_ref[...])

    y_ft = jnp.where(pl.program_id(0) < n_ctx_tiles, yc_ref[...], yl_ref[...])
    acc = acc + _sigmoid(gf_ref[...].astype(F32)) * _dot(y_ft.astype(BF16), pf_ref[...])
    m_ref[...] = acc.astype(BF16)


def _branches(z, o_f, o_b, y_ft, gla_norm, w_s, b_s, p_gla, p_sg, p_ft, *, cols, tm):
    t_rows = z.shape[0]
    d = p_gla.shape[1]
    n_ctx_tiles = y_ft[0].shape[0] // tm
    kern = functools.partial(_branch_kernel, n_ctx_tiles=n_ctx_tiles)

    def zspec(width, col0):
        return pl.BlockSpec((tm, width), lambda i: (i, col0 // width))

    def rows(width):
        return pl.BlockSpec((tm, width), lambda i: (i, 0))

    def const(shape):
        return pl.BlockSpec(shape, lambda i: (0,) * len(shape))

    return pl.pallas_call(
        kern,
        grid=(t_rows // tm,),
        in_specs=[
            zspec(d, cols["gates"]), zspec(d, cols["gates"] + d), zspec(d, cols["gates"] + 2 * d),
            rows(W_V), rows(W_V), zspec(W_V, cols["r"]), zspec(W_SG, cols["su"]), zspec(W_SG, cols["sv"]),
            pl.BlockSpec((tm, W_FT), lambda i: (jnp.minimum(i, n_ctx_tiles - 1), 0)),
            pl.BlockSpec((tm, W_FT), lambda i: (jnp.maximum(i - n_ctx_tiles, 0), 0)),
            const((1, W_V)), const((SG_GROUPS, SG_CHUNK, SG_CHUNK)), const((SG_GROUPS, SG_CHUNK, SG_CH)),
            const((W_V, d)), const((W_SG, d)), const((W_FT, d)),
        ],
        out_specs=rows(d),
        out_shape=jax.ShapeDtypeStruct((t_rows, d), BF16),
        compiler_params=_cparams("parallel"),
        name="branch_merge",
    )(z, z, z, o_f, o_b, z, z, z, y_ft[0], y_ft[1], gla_norm, w_s, b_s, p_gla, p_sg, p_ft)


def _outproj_kernel(m_ref, x_ref, w_ref, g1_ref, n2_ref, sh_ref, sc_ref, wr_ref, br_ref,
                    xo_ref, h_ref, lg_ref, *, n_ctx_tiles, tiles_per_batch, ctx_row):
    row = _mod_row(pl.program_id(0), n_ctx_tiles, tiles_per_batch, ctx_row)
    x = x_ref[...] + g1_ref[pl.ds(row, 1), :] * _dot(m_ref[...], w_ref[...])
    xo_ref[...] = x
    y = x * lax.rsqrt(jnp.mean(x * x, axis=-1, keepdims=True) + EPS) * n2_ref[...]
    h = y * (1.0 + sc_ref[pl.ds(row, 1), :]) + sh_ref[pl.ds(row, 1), :]
    h_hi = h.astype(BF16)
    h_ref[...] = _pack_bf16_pairs(h_hi)
    h_lo = (h - h_hi.astype(F32)).astype(BF16)
    w = wr_ref[...]
    w_hi = w.astype(BF16)
    w_lo = (w - w_hi.astype(F32)).astype(BF16)
    lg = _dot(h_hi, w_hi) + (_dot(h_hi, w_lo) + _dot(h_lo, w_hi)) + br_ref[...]
    lg_ref[...] = lg.T


def _out_project(merged, x, w_out, mod, norm2, w_router, b_router, *, tm, n_ctx_tiles, tiles_per_batch, ctx_row):
    t_rows, d = x.shape
    kern = functools.partial(_outproj_kernel, n_ctx_tiles=n_ctx_tiles,
                             tiles_per_batch=tiles_per_batch, ctx_row=ctx_row)

    def rows(width):
        return pl.BlockSpec((tm, width), lambda i: (i, 0))

    def modspec(k):
        return pl.BlockSpec((8, d), lambda i: (0, k))

    return pl.pallas_call(
        kern,
        grid=(t_rows // tm,),
        in_specs=[rows(d), rows(d), pl.BlockSpec((d, d), lambda i: (0, 0)),
                  modspec(2), pl.BlockSpec((1, d), lambda i: (0, 0)), modspec(3), modspec(4),
                  pl.BlockSpec((d, LANES), lambda i: (0, 0)), pl.BlockSpec((1, LANES), lambda i: (0, 0))],
        out_specs=[rows(d), rows(d // 2), pl.BlockSpec((LANES, tm), lambda i: (0, i))],
        out_shape=[jax.ShapeDtypeStruct((t_rows, d), F32), jax.ShapeDtypeStruct((t_rows, d // 2), jnp.uint32),
                   jax.ShapeDtypeStruct((LANES, t_rows), F32)],
        compiler_params=_cparams("parallel"),
        name="out_project",
    )(merged, x, w_out, mod, norm2, mod, mod, w_router, b_router)


def _expert_kernel(be_ref, jb_ref, nblk_ref, ord_ref, nxt_ref, nu_ref,
                   x_ref, wgu_hbm, bgu_ref, wd_hbm, bd_ref, y_ref,
                   wgu_bf, wd_bf, stage, sems, *, layer):
    i = pl.program_id(0)
    n_gu_chunks = wgu_bf.shape[1] // W_CHUNK
    n_chunks = n_gu_chunks + wd_bf.shape[1] // W_CHUNK

    def chunk_copy(e, q):
        if q < n_gu_chunks:
            src = wgu_hbm.at[layer, e, pl.ds(q * W_CHUNK, W_CHUNK), :]
        else:
            src = wd_hbm.at[layer, e, pl.ds((q - n_gu_chunks) * W_CHUNK, W_CHUNK), :]
        return pltpu.make_async_copy(src, stage.at[q % 2], sems.at[q % 2])

    def land(e, q, slot):
        chunk_copy(e, q).wait()
        w = stage[q % 2].astype(BF16)
        if q < n_gu_chunks:
            wgu_bf[slot, pl.ds(q * W_CHUNK, W_CHUNK), :] = w
        else:
            wd_bf[slot, pl.ds((q - n_gu_chunks) * W_CHUNK, W_CHUNK), :] = w
        if q + 2 < n_chunks:
            chunk_copy(e, q + 2).start()

    active = i < nu_ref[0]
    e = be_ref[i]
    j = jb_ref[i]
    n = nblk_ref[i]
    slot = ord_ref[i] % 2
    nxt = nxt_ref[i]

    @pl.when(i == 0)
    def _():
        chunk_copy(e, 0).start()
        chunk_copy(e, 1).start()
        for q in range(n_chunks):
            land(e, q, slot)

    prefetch = jnp.logical_and(active, nxt >= 0)

    @pl.when(jnp.logical_and(prefetch, j == 0))
    def _():
        chunk_copy(nxt, 0).start()
        chunk_copy(nxt, 1).start()

    @pl.when(active)
    def _():
        gu = _dot(_unpack_bf16_pairs(x_ref[...]), wgu_bf[slot]) + bgu_ref[0, 0]
        de = gu.shape[1] // 2
        gate = jnp.minimum(gu[:, :de], SWIGLU_LIMIT)
        up = jnp.clip(gu[:, de:], -SWIGLU_LIMIT, SWIGLU_LIMIT)
        hdn = (up + 1.0) * (gate * _sigmoid(SWIGLU_ALPHA * gate))
        y_ref[...] = _dot(hdn.astype(BF16), wd_bf[slot]) + bd_ref[0, 0]

    @pl.when(jnp.logical_not(active))
    def _():
        y_ref[...] = jnp.zeros_like(y_ref)

    lo = (n_chunks * j) // n
    hi = (n_chunks * (j + 1)) // n
    for q in range(n_chunks):
        @pl.when(jnp.logical_and(prefetch, jnp.logical_and(lo <= q, q < hi)))
        def _(q=q):
            land(nxt, q, 1 - slot)


def _experts(xs, sched, layer, w_gu, b_gu, w_down, b_down):
    rows_total = xs.shape[0]
    nb = rows_total // MOE_BLOCK
    depth, ne, d, n_gu = w_gu.shape
    assert xs.shape[1] * 2 == d and xs.dtype == jnp.uint32
    de = w_down.shape[2]
    assert n_gu == d and d % W_CHUNK == 0 and de % W_CHUNK == 0

    def blk(i, *s):
        return jnp.minimum(i, s[-1][0] - 1)

    def bspec(n):
        return pl.BlockSpec((1, 1, 1, n), lambda i, *s: (layer, s[0][blk(i, *s)], 0, 0))

    grid_spec = pltpu.PrefetchScalarGridSpec(
        num_scalar_prefetch=len(sched),
        grid=(nb,),
        in_specs=[
            pl.BlockSpec((MOE_BLOCK, d // 2), lambda i, *s: (blk(i, *s), 0)),
            pl.BlockSpec(memory_space=pl.ANY), bspec(n_gu),
            pl.BlockSpec(memory_space=pl.ANY), bspec(d),
        ],
        out_specs=pl.BlockSpec((MOE_BLOCK, d), lambda i, *s: (i, 0)),
        scratch_shapes=[pltpu.VMEM((2, d, n_gu), BF16), pltpu.VMEM((2, de, d), BF16),
                        pltpu.VMEM((2, W_CHUNK, d), F32), pltpu.SemaphoreType.DMA((2,))],
    )
    return pl.pallas_call(
        functools.partial(_expert_kernel, layer=layer),
        grid_spec=grid_spec,
        out_shape=jax.ShapeDtypeStruct((rows_total, d), F32),
        compiler_params=_cparams("arbitrary"),
        name="moe_experts",
    )(*sched, xs, w_gu, b_gu.reshape(depth, ne, 1, n_gu), w_down, b_down.reshape(depth, ne, 1, d))


def _route(logits_t, n_tok):
    experts = jnp.arange(N_EXPERTS, dtype=jnp.int32)
    lg = logits_t
    e_iota = lax.broadcasted_iota(jnp.int32, lg.shape, 0)
    vals, idxs = [], []
    for _ in range(TOP_K):
        mx = jnp.max(lg, axis=0)
        ix = jnp.min(jnp.where(lg == mx[None, :], e_iota, N_EXPERTS), axis=0)
        vals.append(mx)
        idxs.append(ix)
        lg = jnp.where(e_iota == ix[None, :], -jnp.inf, lg)
    top_val = jnp.stack(vals)
    top_idx = jnp.stack(idxs)
    probs = jax.nn.softmax(top_val, axis=0)

    m = n_tok * TOP_K
    onehot = (top_idx[:, :, None] == experts[None, None, :]).astype(jnp.int32)
    csum = jnp.cumsum(onehot, axis=1)
    per_choice = csum[:, -1, :]
    before_choice = jnp.cumsum(per_choice, axis=0) - per_choice
    counts = jnp.sum(per_choice, axis=0)
    pad_counts = (counts + MOE_BLOCK - 1) // MOE_BLOCK * MOE_BLOCK
    pad_end = jnp.cumsum(pad_counts)
    pad_start = pad_end - pad_counts
    base = before_choice + (pad_start - 1)[None, :]
    pos = jnp.sum((csum + base[:, None, :]) * onehot, axis=2)

    nb = (m + MOE_BLOCK - 1) // MOE_BLOCK + N_EXPERTS
    idx_bits = (m - 1).bit_length()
    assert (2 * N_EXPERTS + 2) << idx_bits < 2 ** 31
    assign = lax.broadcasted_iota(jnp.int32, (TOP_K, n_tok), 0) * n_tok + \
        lax.broadcasted_iota(jnp.int32, (TOP_K, n_tok), 1)
    real_keys = (((2 * top_idx) << idx_bits) | assign).reshape(m)
    n_fill = nb * MOE_BLOCK - m
    assert n_fill % N_EXPERTS == 0 and n_fill // N_EXPERTS >= MOE_BLOCK - 1
    fill_i = lax.broadcasted_iota(jnp.int32, (N_EXPERTS, n_fill // N_EXPERTS), 1)
    fill_keys = jnp.where(fill_i < (pad_counts - counts)[:, None],
                          ((2 * experts + 1) << idx_bits)[:, None], (2 * N_EXPERTS + 1) << idx_bits)
    keys = lax.sort(jnp.concatenate([real_keys, fill_keys.reshape(n_fill)]))
    is_real = ((keys >> idx_bits) & 1) == 0
    row_tok = jnp.where(is_real, (keys & ((1 << idx_bits) - 1)) % n_tok, 0)

    blk_start = jnp.arange(nb, dtype=jnp.int32) * MOE_BLOCK
    blk_e = jnp.minimum(jnp.sum((pad_end[None, :] <= blk_start[:, None]).astype(jnp.int32), axis=1),
                        N_EXPERTS - 1)
    n_used = (pad_end[-1] // MOE_BLOCK).astype(jnp.int32).reshape(1)
    has_blocks = pad_counts > 0
    order_e = jnp.cumsum(has_blocks.astype(jnp.int32)) - 1
    later = jnp.logical_and(has_blocks[None, :], experts[None, :] > experts[:, None])
    next_e = jnp.min(jnp.where(later, experts[None, :], N_EXPERTS), axis=1)
    next_e = jnp.where(next_e == N_EXPERTS, -1, next_e)
    blk_is = blk_e[:, None] == experts[None, :]

    def per_block(table):
        return jnp.sum(jnp.where(blk_is, table[None, :], 0), axis=1)

    blk_j = (blk_start - per_block(pad_start)) // MOE_BLOCK
    blk_n = jnp.maximum(per_block(pad_counts) // MOE_BLOCK, 1)
    sched = (blk_e, blk_j, blk_n, per_block(order_e), per_block(next_e), n_used)
    return probs, pos, row_tok, sched


def _rows(x, idx):
    return x.at[idx].get(mode="promise_in_bounds")


def _moe(h2, logits_t, layer, w_gu, b_gu, w_down, b_down):
    n_tok = h2.shape[0]
    probs, pos, row_tok, sched = _route(logits_t, n_tok)
    xs = _rows(h2, row_tok)
    yb = _experts(xs, sched, layer, w_gu, b_gu, w_down, b_down)
    y = _rows(yb, pos.reshape(-1)).reshape(TOP_K, n_tok, -1)
    return y, probs.T


def _combine_kernel(x_ref, y0_ref, y1_ref, y2_ref, y3_ref, p_ref, g2_ref, nf_ref, o_ref,
                    *, tile0, final, n_ctx_tiles, tiles_per_batch, ctx_row):
    row = _mod_row(pl.program_id(0) + tile0, n_ctx_tiles, tiles_per_batch, ctx_row)
    p = p_ref[...]
    y = (p[:, 0:1] * y0_ref[0] + p[:, 1:2] * y1_ref[0]) + (p[:, 2:3] * y2_ref[0] + p[:, 3:4] * y3_ref[0])
    x = x_ref[...] + g2_ref[pl.ds(row, 1), :] * y
    if final:
        x = x * lax.rsqrt(jnp.mean(x * x, axis=-1, keepdims=True) + EPS) * nf_ref[...]
    o_ref[...] = x


def _combine(x, y, probs, mod, norm_f, *, tile0, final, n_ctx_tiles, tiles_per_batch, ctx_row):
    assert y.shape[0] == TOP_K == 4
    _, n_rows, d = y.shape
    tm = ROW_TILE
    kern = functools.partial(_combine_kernel, tile0=tile0, final=final, n_ctx_tiles=n_ctx_tiles,
                             tiles_per_batch=tiles_per_batch, ctx_row=ctx_row)

    def yspec(k):
        return pl.BlockSpec((1, tm, d), lambda i: (k, i, 0))

    return pl.pallas_call(
        kern,
        grid=(n_rows // tm,),
        in_specs=[pl.BlockSpec((tm, d), lambda i: (tile0 + i, 0)),
                  yspec(0), yspec(1), yspec(2), yspec(3),
                  pl.BlockSpec((tm, TOP_K), lambda i: (i, 0)),
                  pl.BlockSpec((8, d), lambda i: (0, 5)),
                  pl.BlockSpec((1, d), lambda i: (0, 0))],
        out_specs=pl.BlockSpec((tm, d), lambda i: (i, 0)),
        out_shape=jax.ShapeDtypeStruct((n_rows, d), F32),
        compiler_params=_cparams("parallel"),
        name="moe_combine",
    )(x, y, y, y, y, probs, mod, norm_f)


def _embed_kernel(ctx_ref, x_ref, pos_ref, o_ref, *, n_ctx_tiles):
    j = pl.program_id(0)
    b = pl.program_id(1)

    @pl.when(jnp.logical_and(j == 0, b < n_ctx_tiles))
    def _():
        o_ref[...] = ctx_ref[...]

    @pl.when(j > 0)
    def _():
        o_ref[...] = x_ref[...] + pos_ref[...]


def _embed(ctx2, x2, pos, *, batch):
    tm = ROW_TILE
    d = x2.shape[1]
    l = pos.shape[0]
    n_ctx_tiles = ctx2.shape[0] // tm
    nt = l // tm
    assert n_ctx_tiles <= batch
    return pl.pallas_call(
        functools.partial(_embed_kernel, n_ctx_tiles=n_ctx_tiles),
        grid=(1 + nt, batch),
        in_specs=[pl.BlockSpec((tm, d), lambda j, b: (jnp.minimum(b, n_ctx_tiles - 1), 0)),
                  pl.BlockSpec((tm, d), lambda j, b: (b * nt + jnp.maximum(j - 1, 0), 0)),
                  pl.BlockSpec((tm, d), lambda j, b: (jnp.maximum(j - 1, 0), 0))],
        out_specs=pl.BlockSpec(
            (tm, d), lambda j, b: (jnp.where(j == 0, jnp.minimum(b, n_ctx_tiles - 1),
                                             n_ctx_tiles + b * nt + j - 1), 0)),
        out_shape=jax.ShapeDtypeStruct((ctx2.shape[0] + x2.shape[0], d), F32),
        compiler_params=_cparams("arbitrary", "arbitrary"),
        name="embed_tokens",
    )(ctx2, x2, pos)


def _grid_pos_embed(n, d):
    rows = n // GRID_W
    row = jnp.broadcast_to(jnp.arange(rows, dtype=F32)[:, None], (rows, GRID_W)).reshape(-1)
    col = jnp.broadcast_to(jnp.arange(GRID_W, dtype=F32)[None, :], (rows, GRID_W)).reshape(-1)
    quarter = d // 4
    omega = 1.0 / (10000.0 ** (jnp.arange(quarter, dtype=F32) / quarter))

    def enc(p):
        a = p[:, None] * omega
        return jnp.concatenate([jnp.sin(a), jnp.cos(a)], axis=-1)

    return jnp.concatenate([enc(row), enc(col)], axis=-1)


def kernel(x, c, ctx, c_ctx, w_mod, b_mod, norm1, w_in, w_a2, b_a, gla_norm, w_s, b_s, p_gla, p_sg, p_ft,
           w_out, norm2, w_router, b_router, w_gu, b_gu, w_down, b_down, norm_f):
    batch, l, d = x.shape
    lc = ctx.shape[1]
    depth = w_mod.shape[0]
    n_ctx_rows = batch * lc
    t_rows = n_ctx_rows + batch * l
    assert lc % ROW_TILE == 0 and l % ROW_TILE == 0 and batch < 8

    tm1 = math.gcd(math.gcd(n_ctx_rows, l), 1024)
    assert tm1 % ROW_TILE == 0
    proj_tiles = dict(n_ctx_tiles=n_ctx_rows // tm1, tiles_per_batch=l // tm1, ctx_row=batch)
    row_tiles = dict(n_ctx_tiles=n_ctx_rows // ROW_TILE, tiles_per_batch=l // ROW_TILE, ctx_row=batch)
    tm2 = ROW_TILE
    big_tiles = dict(n_ctx_tiles=n_ctx_rows // tm2, tiles_per_batch=l // tm2, ctx_row=batch)

    o_q = 0
    o_k = o_q + W_Q
    o_v = o_k + W_Q
    o_r = o_v + W_V
    o_a = o_r + W_V
    o_su = o_a + 2 * GLA_RANK
    o_sv = o_su + W_SG
    o_f = o_sv + W_SG
    o_g = o_f + W_FT
    n_gates = 3 * d
    cols = dict(gates=0, q=n_gates, r=n_gates + 2 * W_Q + W_V, su=n_gates + 2 * W_Q + 2 * W_V)
    cols["sv"] = cols["su"] + W_SG
    cols["f"] = cols["sv"] + W_SG
    n_main = cols["f"] + W_FT

    xt = _embed(ctx.reshape(n_ctx_rows, d), x.reshape(batch * l, d), _grid_pos_embed(l, d), batch=batch)

    cond = jnp.zeros((8, d), F32).at[:batch].set(c).at[batch].set(c_ctx)
    mods = _mod_vectors(cond, w_mod, b_mod)

    cc, sc = _dft_tables(FT_CH)
    chan_tab = jnp.concatenate([cc, sc], axis=1)
    cos_l, sin_l = _dft_tables(l)
    cos_c, sin_c = _dft_tables(lc)

    w_main = jnp.concatenate(
        [w_in[:, :, o_g:], w_in[:, :, o_q:o_a], w_in[:, :, o_su:o_g]], axis=2).astype(BF16)
    w_alr = jnp.pad(w_in[:, :, o_a:o_su], ((0, 0), (0, 0), (0, LANES - 2 * GLA_RANK))).astype(BF16)

    for i in range(depth):
        last = i == depth - 1
        w_a = w_alr[i]
        wa2 = jnp.zeros((LANES, 2 * W_Q), F32)
        wa2 = wa2.at[:GLA_RANK, :W_Q].set(w_a2[i, 0]).at[GLA_RANK:2 * GLA_RANK, W_Q:].set(w_a2[i, 1])
        mod = mods[i]

        z, la = _project(xt, mod, norm1[i][None], w_main, w_a, wa2.astype(BF16), b_a[i].reshape(1, 2 * W_Q),
                         layer=i, tm=tm1, tn=1536, **proj_tiles)
        assert z.shape[1] == n_main

        o_fw, o_bw = _gla(z, la, batch=batch, lc=lc, l=l, col_q=cols["q"])

        xc_c, xs_c = _ft_channel(z, chan_tab, col_f=cols["f"], row0=0, n_rows=n_ctx_rows)
        xc_l, xs_l = _ft_channel(z, chan_tab, col_f=cols["f"], row0=n_ctx_rows, n_rows=batch * l)
        y_ft = (_ft_positions(cos_c, sin_c, xc_c, xs_c, batch=batch, seq=lc),
                _ft_positions(cos_l, sin_l, xc_l, xs_l, batch=batch, seq=l))

        bs_b = jnp.broadcast_to(b_s[i][:, :, None], (SG_GROUPS, SG_CHUNK, SG_CH))
        merged = _branches(z, o_fw, o_bw, y_ft, gla_norm[i][None], w_s[i].astype(BF16), bs_b,
                           p_gla[i].astype(BF16), p_sg[i].astype(BF16), p_ft[i].astype(BF16), cols=cols, tm=tm2)

        w_r = jnp.pad(w_router[i], ((0, 0), (0, LANES - N_EXPERTS)))
        b_r = jnp.pad(b_router[i], (0, LANES - N_EXPERTS)).reshape(1, LANES)
        xt, h2, logits_t = _out_project(merged, xt, w_out[i].astype(BF16), mod, norm2[i][None], w_r, b_r,
                                      tm=tm2, **big_tiles)

        row0 = n_ctx_rows if last else 0
        y, probs = _moe(h2[row0:], logits_t[:N_EXPERTS, row0:], i, w_gu, b_gu, w_down, b_down)
        xt = _combine(xt, y, probs, mod, norm_f[None], tile0=row0 // ROW_TILE, final=last, **row_tiles)

    return xt.reshape(batch, l, d)
```

```python
import functools
import math

import jax
import jax.numpy as jnp
from jax import lax
from jax.experimental import pallas as pl
from jax.experimental.pallas import tpu as pltpu

F32 = jnp.float32
BF16 = jnp.bfloat16

EPS = 1e-6
GRID_W = 64
GLA_HEADS = 4
GLA_DK = 128
GLA_DV = 256
GLA_RANK = 16
GLA_TAU = 16.0
GLA_CHUNK = 64
GLA_SUB = 16
GLA_STAGES = 3
SG_GROUPS = 4
SG_CH = 128
SG_CHUNK = 128
FT_GROUPS = 4
FT_CH = 128
N_EXPERTS = 32
TOP_K = 4
SWIGLU_LIMIT = 7.0
SWIGLU_ALPHA = 1.702
MOE_BLOCK = 256
W_CHUNK = 512

LOG2E = math.log2(math.e)
LANES = 128
ROW_TILE = 256
VMEM_LIMIT = 56 * 1024 * 1024

W_Q = GLA_HEADS * GLA_DK
W_V = GLA_HEADS * GLA_DV
W_SG = SG_GROUPS * SG_CH
W_FT = FT_GROUPS * FT_CH


def _cparams(*sem):
    return pltpu.CompilerParams(dimension_semantics=sem, vmem_limit_bytes=VMEM_LIMIT)


def _sigmoid(x):
    return 1.0 / (1.0 + jnp.exp(-x))


def _gelu_tanh(x):
    return 0.5 * x * (1.0 + jnp.tanh(math.sqrt(2.0 / math.pi) * (x + 0.044715 * (x * x * x))))


def _dot(a, b):
    return jnp.dot(a, b, preferred_element_type=F32)


def _pack_bf16_pairs(x):
    bits = lax.bitcast_convert_type(x.astype(F32), jnp.uint32)
    half = x.shape[1] // 2
    return (bits[:, :half] >> 16) | (bits[:, half:] & jnp.uint32(0xFFFF0000))


def _unpack_bf16_pairs(u):
    lo = lax.bitcast_convert_type(u << 16, F32)
    hi = lax.bitcast_convert_type(u & jnp.uint32(0xFFFF0000), F32)
    return jnp.concatenate([lo, hi], axis=1).astype(BF16)


def _mod_row(tile, n_ctx_tiles, tiles_per_batch, ctx_row):
    return jnp.where(tile < n_ctx_tiles, ctx_row, (tile - n_ctx_tiles) // tiles_per_batch)


def _mod_kernel(c_ref, w_ref, b_ref, o_ref):
    c = c_ref[...]
    s = (c * _sigmoid(c)).astype(BF16)
    o_ref[0] = _dot(s, w_ref[0].astype(BF16)) + b_ref[0]


def _mod_vectors(cond, w_mod, b_mod):
    depth, d, n = w_mod.shape
    tn = 1024
    return pl.pallas_call(
        _mod_kernel,
        grid=(depth, n // tn),
        in_specs=[
            pl.BlockSpec((8, d), lambda l, j: (0, 0)),
            pl.BlockSpec((1, d, tn), lambda l, j: (l, 0, j)),
            pl.BlockSpec((1, 1, tn), lambda l, j: (l, 0, j)),
        ],
        out_specs=pl.BlockSpec((1, 8, tn), lambda l, j: (l, 0, j)),
        out_shape=jax.ShapeDtypeStruct((depth, 8, n), F32),
        compiler_params=_cparams("parallel", "parallel"),
        name="mod_vectors",
    )(cond, w_mod, b_mod.reshape(depth, 1, n))


def _proj_kernel(x_ref, sh_ref, sc_ref, g_ref, w_ref, wa_ref, wa2_ref, ba_ref,
                 z_ref, la_ref, a_scr, *, n_ctx_tiles, tiles_per_batch, ctx_row):
    i = pl.program_id(0)
    j = pl.program_id(1)

    @pl.when(j == 0)
    def _():
        row = _mod_row(i, n_ctx_tiles, tiles_per_batch, ctx_row)
        scale = 1.0 + sc_ref[pl.ds(row, 1), :]
        shift = sh_ref[pl.ds(row, 1), :]

        def body(r, carry):
            rows = pl.ds(pl.multiple_of(r * ROW_TILE, ROW_TILE), ROW_TILE)
            x = x_ref[rows, :]
            y = x * lax.rsqrt(jnp.mean(x * x, axis=-1, keepdims=True) + EPS) * g_ref[...]
            hb = (y * scale + shift).astype(BF16)
            a_scr[rows, :] = hb
            a_lr = _dot(hb, wa_ref[...])
            pre = _dot(a_lr.astype(BF16), wa2_ref[...]) + ba_ref[...]
            la_ref[rows, :] = -(jnp.maximum(-pre, 0.0) + jnp.log1p(jnp.exp(-jnp.abs(pre)))) * (1.0 / GLA_TAU)
            return carry

        lax.fori_loop(0, x_ref.shape[0] // ROW_TILE, body, 0)

    z_ref[...] = _dot(a_scr[...], w_ref[0]).astype(BF16)


def _project(x, mod, g, w_main, w_a, w_a2, b_a, *, layer, tm, tn, n_ctx_tiles, tiles_per_batch, ctx_row):
    t, d = x.shape
    n = w_main.shape[2]
    kern = functools.partial(_proj_kernel, n_ctx_tiles=n_ctx_tiles,
                             tiles_per_batch=tiles_per_batch, ctx_row=ctx_row)
    return pl.pallas_call(
        kern,
        grid=(t // tm, n // tn),
        in_specs=[
            pl.BlockSpec((tm, d), lambda i, j: (i, 0)),
            pl.BlockSpec((8, d), lambda i, j: (0, 0)),
            pl.BlockSpec((8, d), lambda i, j: (0, 1)),
            pl.BlockSpec((1, d), lambda i, j: (0, 0)),
            pl.BlockSpec((1, d, tn), lambda i, j: (layer, 0, j)),
            pl.BlockSpec((d, LANES), lambda i, j: (0, 0)),
            pl.BlockSpec((LANES, 2 * W_Q), lambda i, j: (0, 0)),
            pl.BlockSpec((1, 2 * W_Q), lambda i, j: (0, 0)),
        ],
        out_specs=[
            pl.BlockSpec((tm, tn), lambda i, j: (i, j)),
            pl.BlockSpec((tm, 2 * W_Q), lambda i, j: (i, 0)),
        ],
        out_shape=[
            jax.ShapeDtypeStruct((t, n), BF16),
            jax.ShapeDtypeStruct((t, 2 * W_Q), F32),
        ],
        scratch_shapes=[pltpu.VMEM((tm, d), BF16)],
        compiler_params=_cparams("parallel", "arbitrary"),
        name="norm_mod_project",
    )(x, mod, mod, g, w_main, w_a, w_a2, b_a)


def _gla_pairs(rev):
    c, sub = GLA_CHUNK, GLA_SUB
    half = c // 2
    if rev:
        return ((0, half, half, c, half - 1),
                (0, sub, sub, half, sub - 1),
                (half, half + sub, half + sub, c, half + sub - 1))
    return ((half, c, 0, half, half),
            (sub, half, 0, sub, sub),
            (half + sub, c, half, half + sub, half + sub))


def _place_rows(x, r0, n):
    parts = []
    if r0:
        parts.append(jnp.zeros((r0, x.shape[1]), x.dtype))
    parts.append(x)
    rest = n - r0 - x.shape[0]
    if rest:
        parts.append(jnp.zeros((rest, x.shape[1]), x.dtype))
    return jnp.concatenate(parts, axis=0)


def _gla_chunk(q_ref, k_ref, v_ref, la_ref, e_ref, o_ref, s_ref, cum_ref, kc_ref, r0, h, rev):
    c, sub = GLA_CHUNK, GLA_SUB
    nsub = c // sub
    rows = pl.ds(r0, c)
    kcols = slice(h * GLA_DK, (h + 1) * GLA_DK)
    vcols = slice(h * GLA_DV, (h + 1) * GLA_DV)
    q = q_ref[rows, kcols].astype(F32) * (GLA_DK ** -0.5)
    k = k_ref[rows, kcols].astype(F32)
    v = v_ref[rows, vcols]
    la = la_ref[rows, kcols]

    ri = lax.broadcasted_iota(jnp.int32, (c, c), 0)
    ci = lax.broadcasted_iota(jnp.int32, (c, c), 1)
    before = (ci >= ri) if rev else (ci <= ri)
    tri = jnp.where(before, 1.0, 0.0).astype(BF16)
    la_hi = la.astype(BF16)
    rem = la - la_hi.astype(F32)
    la_mid = rem.astype(BF16)
    la_lo = (rem - la_mid.astype(F32)).astype(BF16)
    cum = (_dot(tri, la_hi) + _dot(tri, la_mid) + _dot(tri, la_lo)) * LOG2E
    cum_ref[...] = cum
    kc_ref[...] = k
    yield
    last = 0 if rev else c - 1
    total = cum[last:last + 1, :]

    s = s_ref[...]
    o = _dot((q * jnp.exp2(cum)).astype(BF16), s.astype(BF16))

    qparts, kparts = [], []
    for l0, l1, e0, e1, ref in _gla_pairs(rev):
        cr = cum[ref:ref + 1, :]
        qparts.append(_place_rows(q[l0:l1] * jnp.exp2(cum[l0:l1] - cr), l0, c))
        kparts.append(_place_rows(k[e0:e1] * jnp.exp2(cr - cum[e0:e1]), e0, c))
    s_off = lax.dot_general(jnp.concatenate(qparts, axis=1).astype(BF16),
                            jnp.concatenate(kparts, axis=1).astype(BF16),
                            (((1,), (1,)), ((), ())), preferred_element_type=F32)

    hs = sub // 2

    def halves(x, which):
        return jnp.concatenate([x[a * sub + w * hs:a * sub + (w + 1) * hs] for a in range(nsub) for w in which],
                               axis=0)

    parts = []
    for jj in range(sub):
        if rev:
            which = (0, 1) if jj >= hs else (0,)
        else:
            which = (0, 1) if jj < hs else (1,)
        nrow = hs * len(which)
        cj = jnp.concatenate(
            [jnp.broadcast_to(cum_ref[a * sub + jj:a * sub + jj + 1, :], (nrow, GLA_DK)) for a in range(nsub)],
            axis=0)
        kj = jnp.concatenate(
            [jnp.broadcast_to(kc_ref[a * sub + jj:a * sub + jj + 1, :], (nrow, GLA_DK)) for a in range(nsub)],
            axis=0)
        prod = halves(q, which) * jnp.exp2(jnp.minimum(halves(cum, which) - cj, 0.0)) * kj
        if len(which) == 1:
            zero = jnp.zeros((hs, GLA_DK), F32)
            pieces = []
            for a in range(nsub):
                live = prod[a * hs:(a + 1) * hs]
                pieces += [live, zero] if which[0] == 0 else [zero, live]
            prod = jnp.concatenate(pieces, axis=0)
        parts.append(prod.astype(BF16))
    diag = _dot(jnp.concatenate(parts, axis=1), e_ref[...])
    yield

    lag = (ci - ri) if rev else (ri - ci)
    room = (sub - 1 - ri % sub) if rev else (ri % sub)
    in_diag = lag.astype(jnp.uint32) <= room.astype(jnp.uint32)
    p = jnp.where(in_diag, diag, 0.0) + s_off
    o_ref[rows, vcols] = o + _dot(p.astype(BF16), v)

    k_out = (k * jnp.exp2(total - cum)).astype(BF16)
    upd = lax.dot_general(k_out, v, (((0,), (0,)), ((), ())), preferred_element_type=F32)
    dmat = jnp.broadcast_to(jnp.exp2(total), (GLA_DK, GLA_DK)).T
    for half in range(GLA_DV // GLA_DK):
        cols = slice(half * GLA_DK, (half + 1) * GLA_DK)
        s_ref[:, cols] = s[:, cols] * dmat + upd[:, cols]


def _gla_kernel(qf, kf, vf, laf, qb, kb, vb, lab, e_ref, of_ref, ob_ref, *scratch):
    n = 2 * GLA_HEADS
    states, cums, keys = scratch[:n], scratch[n:2 * n], scratch[2 * n:]

    @pl.when(pl.program_id(1) == 0)
    def _():
        for s_ref in states:
            s_ref[...] = jnp.zeros_like(s_ref)

    nchunk = ROW_TILE // GLA_CHUNK

    def body(c, carry):
        r_f = pl.multiple_of(c * GLA_CHUNK, GLA_CHUNK)
        r_b = pl.multiple_of((nchunk - 1 - c) * GLA_CHUNK, GLA_CHUNK)
        units = []
        for h in range(GLA_HEADS):
            f, b = h, GLA_HEADS + h
            units.append(_gla_chunk(qf, kf, vf, laf, e_ref, of_ref, states[f], cums[f], keys[f], r_f, h, False))
            units.append(_gla_chunk(qb, kb, vb, lab, e_ref, ob_ref, states[b], cums[b], keys[b], r_b, h, True))
        for _ in range(GLA_STAGES):
            for u in units:
                next(u, None)
        return carry

    lax.fori_loop(0, nchunk, body, 0)


def _gla(z, la, *, batch, lc, l, col_q):
    t_rows = z.shape[0]
    nctx = lc // ROW_TILE
    nlat = l // ROW_TILE
    steps = nctx + nlat
    qb0 = col_q // W_Q
    kb0 = (col_q + W_Q) // W_Q
    vb0 = (col_q + 2 * W_Q) // W_V

    def fwd_tile(b, t):
        return jnp.where(t < nctx, b * nctx + t, batch * nctx + b * nlat + (t - nctx))

    def bwd_tile(b, t):
        return jnp.where(t < nctx, b * nctx + (nctx - 1 - t), batch * nctx + b * nlat + (steps - 1 - t))

    def spec(width, tile_fn, col):
        return pl.BlockSpec((ROW_TILE, width), lambda b, t: (tile_fn(b, t), col))

    kk = jnp.arange(GLA_SUB * GLA_DK, dtype=jnp.int32)[:, None] // GLA_DK
    cc = jnp.arange(GLA_CHUNK, dtype=jnp.int32)[None, :] % GLA_SUB
    e_mat = (kk == cc).astype(BF16)

    o_shape = jax.ShapeDtypeStruct((t_rows, W_V), F32)
    n_units = 2 * GLA_HEADS
    state = pltpu.VMEM((GLA_DK, GLA_DV), F32)
    chunk = pltpu.VMEM((GLA_CHUNK, GLA_DK), F32)
    return pl.pallas_call(
        _gla_kernel,
        grid=(batch, steps),
        in_specs=[
            spec(W_Q, fwd_tile, qb0), spec(W_Q, fwd_tile, kb0), spec(W_V, fwd_tile, vb0), spec(W_Q, fwd_tile, 0),
            spec(W_Q, bwd_tile, qb0), spec(W_Q, bwd_tile, kb0), spec(W_V, bwd_tile, vb0), spec(W_Q, bwd_tile, 1),
            pl.BlockSpec((GLA_SUB * GLA_DK, GLA_CHUNK), lambda b, t: (0, 0)),
        ],
        out_specs=[spec(W_V, fwd_tile, 0), spec(W_V, bwd_tile, 0)],
        out_shape=[o_shape, o_shape],
        scratch_shapes=[state] * n_units + [chunk] * (2 * n_units),
        compiler_params=_cparams("parallel", "arbitrary"),
        name="gla_scan",
    )(z, z, z, la, z, z, z, la, e_mat)


def _dft_tables(n):
    scale = n ** -0.5
    p = jnp.arange(n, dtype=jnp.int32)[:, None]

    def trig(cols, period):
        ang = ((p * cols[None, :]) % period).astype(F32) * (2.0 * math.pi / period)
        return jnp.cos(ang), jnp.sin(ang)

    n2 = 1
    while n2 * n2 < n:
        n2 *= 2
    if n % n2 or n2 == 1:
        c, s = trig(jnp.arange(n, dtype=jnp.int32), n)
        return (c * scale).astype(BF16), (s * scale).astype(BF16)
    n1 = n // n2
    ca, sa = trig(jnp.arange(n1, dtype=jnp.int32), n1)
    cb, sb = trig(jnp.arange(n2, dtype=jnp.int32), n)
    ca, sa, cb, sb = ca[:, :, None], sa[:, :, None], cb[:, None, :] * scale, sb[:, None, :] * scale
    return ((ca * cb - sa * sb).reshape(n, n).astype(BF16), (sa * cb + ca * sb).reshape(n, n).astype(BF16))


def _ft_chan_kernel(f_ref, cs_ref, xc_ref, xs_ref):
    for g in range(FT_GROUPS):
        cols = slice(g * FT_CH, (g + 1) * FT_CH)
        y = _dot(f_ref[:, cols], cs_ref[...])
        xc_ref[:, cols] = y[:, :FT_CH].astype(BF16)
        xs_ref[:, cols] = y[:, FT_CH:].astype(BF16)


def _ft_channel(z, cs, *, col_f, row0, n_rows):
    tm = math.gcd(math.gcd(row0, n_rows), 1024)
    shape = jax.ShapeDtypeStruct((n_rows, W_FT), BF16)
    return pl.pallas_call(
        _ft_chan_kernel,
        grid=(n_rows // tm,),
        in_specs=[pl.BlockSpec((tm, W_FT), lambda i: (row0 // tm + i, col_f // W_FT)),
                  pl.BlockSpec((FT_CH, 2 * FT_CH), lambda i: (0, 0))],
        out_specs=[pl.BlockSpec((tm, W_FT), lambda i: (i, 0))] * 2,
        out_shape=[shape, shape],
        compiler_params=_cparams("parallel"),
        name="fourier_channels",
    )(z, cs)


def _ft_pos_kernel(c_ref, s_ref, xc_ref, xs_ref, o_ref):
    o_ref[...] = _dot(c_ref[...], xc_ref[...]) - _dot(s_ref[...], xs_ref[...])


def _ft_positions(cos_t, sin_t, xc, xs, *, batch, seq):
    nt = seq // ROW_TILE
    x_spec = pl.BlockSpec((seq, W_FT), lambda b, j: (b, 0))
    t_spec = pl.BlockSpec((ROW_TILE, seq), lambda b, j: (j, 0))
    return pl.pallas_call(
        _ft_pos_kernel,
        grid=(batch, nt),
        in_specs=[t_spec, t_spec, x_spec, x_spec],
        out_specs=pl.BlockSpec((ROW_TILE, W_FT), lambda b, j: (b * nt + j, 0)),
        out_shape=jax.ShapeDtypeStruct((batch * seq, W_FT), F32),
        compiler_params=_cparams("parallel", "parallel"),
        name="fourier_positions",
    )(cos_t, sin_t, xc, xs)


def _branch_kernel(gg_ref, gs_ref, gf_ref, of_ref, ob_ref, r_ref, su_ref, sv_ref, yc_ref, yl_ref,
                   gn_ref, ws_ref, bs_ref, pg_ref, ps_ref, pf_ref, m_ref, *, n_ctx_tiles):
    o = of_ref[...] + ob_ref[...]
    heads = []
    for h in range(GLA_HEADS):
        oh = o[:, h * GLA_DV:(h + 1) * GLA_DV]
        heads.append(oh * lax.rsqrt(jnp.mean(oh * oh, axis=-1, keepdims=True) + EPS))
    r = r_ref[...].astype(F32)
    y_gla = (jnp.concatenate(heads, axis=-1) * gn_ref[...]) * (r * _sigmoid(r))
    acc = _sigmoid(gg_ref[...].astype(F32)) * _dot(y_gla.astype(BF16), pg_ref[...])

    u = _gelu_tanh(su_ref[...].astype(F32))
    v = _gelu_tanh(sv_ref[...].astype(F32))
    groups = []
    for g in range(SG_GROUPS):
        vg = v[:, g * SG_CH:(g + 1) * SG_CH]
        mu = jnp.mean(vg, axis=-1, keepdims=True)
        var = jnp.mean(jnp.square(vg - mu), axis=-1, keepdims=True)
        vn = ((vg - mu) * lax.rsqrt(var + EPS)).astype(BF16)
        chunks = []
        for c in range(v.shape[0] // SG_CHUNK):
            chunks.append(_dot(ws_ref[g], vn[c * SG_CHUNK:(c + 1) * SG_CHUNK, :]) + bs_ref[g])
        groups.append(jnp.concatenate(chunks, axis=0))
    y_sg = u * jnp.concatenate(groups, axis=-1)
    acc = acc + _sigmoid(gs_ref[...].astype(F32)) * _dot(y_sg.astype(BF16), ps_ref[...])

    y_ft = jnp.where(pl.program_id(0) < n_ctx_tiles, yc_ref[...], yl_ref[...])
    acc = acc + _sigmoid(gf_ref[...].astype(F32)) * _dot(y_ft.astype(BF16), pf_ref[...])
    m_ref[...] = acc.astype(BF16)


def _branches(z, o_f, o_b, y_ft, gla_norm, w_s, b_s, p_gla, p_sg, p_ft, *, cols, tm):
    t_rows = z.shape[0]
    d = p_gla.shape[1]
    n_ctx_tiles = y_ft[0].shape[0] // tm
    kern = functools.partial(_branch_kernel, n_ctx_tiles=n_ctx_tiles)

    def zspec(width, col0):
        return pl.BlockSpec((tm, width), lambda i: (i, col0 // width))

    def rows(width):
        return pl.BlockSpec((tm, width), lambda i: (i, 0))

    def const(shape):
        return pl.BlockSpec(shape, lambda i: (0,) * len(shape))

    return pl.pallas_call(
        kern,
        grid=(t_rows // tm,),
        in_specs=[
            zspec(d, cols["gates"]), zspec(d, cols["gates"] + d), zspec(d, cols["gates"] + 2 * d),
            rows(W_V), rows(W_V), zspec(W_V, cols["r"]), zspec(W_SG, cols["su"]), zspec(W_SG, cols["sv"]),
            pl.BlockSpec((tm, W_FT), lambda i: (jnp.minimum(i, n_ctx_tiles - 1), 0)),
            pl.BlockSpec((tm, W_FT), lambda i: (jnp.maximum(i - n_ctx_tiles, 0), 0)),
            const((1, W_V)), const((SG_GROUPS, SG_CHUNK, SG_CHUNK)), const((SG_GROUPS, SG_CHUNK, SG_CH)),
            const((W_V, d)), const((W_SG, d)), const((W_FT, d)),
        ],
        out_specs=rows(d),
        out_shape=jax.ShapeDtypeStruct((t_rows, d), BF16),
        compiler_params=_cparams("parallel"),
        name="branch_merge",
    )(z, z, z, o_f, o_b, z, z, z, y_ft[0], y_ft[1], gla_norm, w_s, b_s, p_gla, p_sg, p_ft)


def _outproj_kernel(m_ref, x_ref, w_ref, g1_ref, n2_ref, sh_ref, sc_ref, wr_ref, br_ref,
                    xo_ref, h_ref, lg_ref, *, n_ctx_tiles, tiles_per_batch, ctx_row):
    row = _mod_row(pl.program_id(0), n_ctx_tiles, tiles_per_batch, ctx_row)
    gate1 = g1_ref[pl.ds(row, 1), :]
    scale2 = 1.0 + sc_ref[pl.ds(row, 1), :]
    shift2 = sh_ref[pl.ds(row, 1), :]
    w = wr_ref[...]
    w_hi = w.astype(BF16)
    w_lo = (w - w_hi.astype(F32)).astype(BF16)
    for r0 in range(0, x_ref.shape[0], ROW_TILE):
        rows = slice(r0, r0 + ROW_TILE)
        x = x_ref[rows, :] + gate1 * _dot(m_ref[rows, :], w_ref[...])
        xo_ref[rows, :] = x
        y = x * lax.rsqrt(jnp.mean(x * x, axis=-1, keepdims=True) + EPS) * n2_ref[...]
        h = y * scale2 + shift2
        h_hi = h.astype(BF16)
        h_ref[rows, :] = _pack_bf16_pairs(h_hi)
        h_lo = (h - h_hi.astype(F32)).astype(BF16)
        lg = _dot(h_hi, w_hi) + (_dot(h_hi, w_lo) + _dot(h_lo, w_hi)) + br_ref[...]
        lg_ref[:, rows] = lg.T


def _out_project(merged, x, w_out, mod, norm2, w_router, b_router, *, tm, n_ctx_tiles, tiles_per_batch, ctx_row):
    t_rows, d = x.shape
    kern = functools.partial(_outproj_kernel, n_ctx_tiles=n_ctx_tiles,
                             tiles_per_batch=tiles_per_batch, ctx_row=ctx_row)

    def rows(width):
        return pl.BlockSpec((tm, width), lambda i: (i, 0))

    def modspec(k):
        return pl.BlockSpec((8, d), lambda i: (0, k))

    return pl.pallas_call(
        kern,
        grid=(t_rows // tm,),
        in_specs=[rows(d), rows(d), pl.BlockSpec((d, d), lambda i: (0, 0)),
                  modspec(2), pl.BlockSpec((1, d), lambda i: (0, 0)), modspec(3), modspec(4),
                  pl.BlockSpec((d, LANES), lambda i: (0, 0)), pl.BlockSpec((1, LANES), lambda i: (0, 0))],
        out_specs=[rows(d), rows(d // 2), pl.BlockSpec((LANES, tm), lambda i: (0, i))],
        out_shape=[jax.ShapeDtypeStruct((t_rows, d), F32), jax.ShapeDtypeStruct((t_rows, d // 2), jnp.uint32),
                   jax.ShapeDtypeStruct((LANES, t_rows), F32)],
        compiler_params=_cparams("parallel"),
        name="out_project",
    )(merged, x, w_out, mod, norm2, mod, mod, w_router, b_router)


def _expert_kernel(be_ref, jb_ref, nblk_ref, ord_ref, nxt_ref, nu_ref,
                   x_ref, wgu_hbm, bgu_ref, wd_hbm, bd_ref, y_ref,
                   wgu_bf, wd_bf, stage, sems, *, layer):
    i = pl.program_id(0)
    n_gu_chunks = wgu_bf.shape[1] // W_CHUNK
    n_chunks = n_gu_chunks + wd_bf.shape[1] // W_CHUNK

    def chunk_copy(e, q):
        if q < n_gu_chunks:
            src = wgu_hbm.at[layer, e, pl.ds(q * W_CHUNK, W_CHUNK), :]
        else:
            src = wd_hbm.at[layer, e, pl.ds((q - n_gu_chunks) * W_CHUNK, W_CHUNK), :]
        return pltpu.make_async_copy(src, stage.at[q % 2], sems.at[q % 2])

    def land(e, q, slot):
        chunk_copy(e, q).wait()
        w = stage[q % 2].astype(BF16)
        if q < n_gu_chunks:
            wgu_bf[slot, pl.ds(q * W_CHUNK, W_CHUNK), :] = w
        else:
            wd_bf[slot, pl.ds((q - n_gu_chunks) * W_CHUNK, W_CHUNK), :] = w
        if q + 2 < n_chunks:
            chunk_copy(e, q + 2).start()

    active = i < nu_ref[0]
    e = be_ref[i]
    j = jb_ref[i]
    n = nblk_ref[i]
    slot = ord_ref[i] % 2
    nxt = nxt_ref[i]

    @pl.when(i == 0)
    def _():
        chunk_copy(e, 0).start()
        chunk_copy(e, 1).start()
        for q in range(n_chunks):
            land(e, q, slot)

    prefetch = jnp.logical_and(active, nxt >= 0)

    @pl.when(jnp.logical_and(prefetch, j == 0))
    def _():
        chunk_copy(nxt, 0).start()
        chunk_copy(nxt, 1).start()

    @pl.when(active)
    def _():
        gu = _dot(_unpack_bf16_pairs(x_ref[...]), wgu_bf[slot]) + bgu_ref[0, 0]
        de = gu.shape[1] // 2
        gate = jnp.minimum(gu[:, :de], SWIGLU_LIMIT)
        up = jnp.clip(gu[:, de:], -SWIGLU_LIMIT, SWIGLU_LIMIT)
        hdn = (up + 1.0) * (gate * _sigmoid(SWIGLU_ALPHA * gate))
        y_ref[...] = _dot(hdn.astype(BF16), wd_bf[slot]) + bd_ref[0, 0]

    @pl.when(jnp.logical_not(active))
    def _():
        y_ref[...] = jnp.zeros_like(y_ref)

    lo = (n_chunks * j) // n
    hi = (n_chunks * (j + 1)) // n
    for q in range(n_chunks):
        @pl.when(jnp.logical_and(prefetch, jnp.logical_and(lo <= q, q < hi)))
        def _(q=q):
            land(nxt, q, 1 - slot)


def _experts(xs, sched, layer, w_gu, b_gu, w_down, b_down):
    rows_total = xs.shape[0]
    nb = rows_total // MOE_BLOCK
    depth, ne, d, n_gu = w_gu.shape
    assert xs.shape[1] * 2 == d and xs.dtype == jnp.uint32
    de = w_down.shape[2]
    assert n_gu == d and d % W_CHUNK == 0 and de % W_CHUNK == 0

    def blk(i, *s):
        return jnp.minimum(i, s[-1][0] - 1)

    def bspec(n):
        return pl.BlockSpec((1, 1, 1, n), lambda i, *s: (layer, s[0][blk(i, *s)], 0, 0))

    grid_spec = pltpu.PrefetchScalarGridSpec(
        num_scalar_prefetch=len(sched),
        grid=(nb,),
        in_specs=[
            pl.BlockSpec((MOE_BLOCK, d // 2), lambda i, *s: (blk(i, *s), 0)),
            pl.BlockSpec(memory_space=pl.ANY), bspec(n_gu),
            pl.BlockSpec(memory_space=pl.ANY), bspec(d),
        ],
        out_specs=pl.BlockSpec((MOE_BLOCK, d), lambda i, *s: (i, 0)),
        scratch_shapes=[pltpu.VMEM((2, d, n_gu), BF16), pltpu.VMEM((2, de, d), BF16),
                        pltpu.VMEM((2, W_CHUNK, d), F32), pltpu.SemaphoreType.DMA((2,))],
    )
    return pl.pallas_call(
        functools.partial(_expert_kernel, layer=layer),
        grid_spec=grid_spec,
        out_shape=jax.ShapeDtypeStruct((rows_total, d), F32),
        compiler_params=_cparams("arbitrary"),
        name="moe_experts",
    )(*sched, xs, w_gu, b_gu.reshape(depth, ne, 1, n_gu), w_down, b_down.reshape(depth, ne, 1, d))


def _route(logits_t, n_tok):
    experts = jnp.arange(N_EXPERTS, dtype=jnp.int32)
    lg = logits_t
    e_iota = lax.broadcasted_iota(jnp.int32, lg.shape, 0)
    vals, idxs = [], []
    for _ in range(TOP_K):
        mx = jnp.max(lg, axis=0)
        ix = jnp.min(jnp.where(lg == mx[None, :], e_iota, N_EXPERTS), axis=0)
        vals.append(mx)
        idxs.append(ix)
        lg = jnp.where(e_iota == ix[None, :], -jnp.inf, lg)
    top_val = jnp.stack(vals)
    top_idx = jnp.stack(idxs)
    probs = jax.nn.softmax(top_val, axis=0)

    m = n_tok * TOP_K
    onehot = (top_idx[:, :, None] == experts[None, None, :]).astype(jnp.int32)
    per_tok = jnp.sum(onehot, axis=0)
    csum = jnp.cumsum(per_tok, axis=0)
    counts = csum[-1]
    pad_counts = (counts + MOE_BLOCK - 1) // MOE_BLOCK * MOE_BLOCK
    pad_end = jnp.cumsum(pad_counts)
    pad_start = pad_end - pad_counts
    base = csum - per_tok + pad_start[None, :]
    pos = jnp.sum(base[None, :, :] * onehot, axis=2)

    nb = (m + MOE_BLOCK - 1) // MOE_BLOCK + N_EXPERTS
    idx_bits = (m - 1).bit_length()
    assert (2 * N_EXPERTS + 2) << idx_bits < 2 ** 31
    assign = lax.broadcasted_iota(jnp.int32, (TOP_K, n_tok), 1) * TOP_K + \
        lax.broadcasted_iota(jnp.int32, (TOP_K, n_tok), 0)
    real_keys = (((2 * top_idx) << idx_bits) | assign).reshape(m)
    n_fill = nb * MOE_BLOCK - m
    assert n_fill % N_EXPERTS == 0 and n_fill // N_EXPERTS >= MOE_BLOCK - 1
    fill_i = lax.broadcasted_iota(jnp.int32, (N_EXPERTS, n_fill // N_EXPERTS), 1)
    fill_keys = jnp.where(fill_i < (pad_counts - counts)[:, None],
                          ((2 * experts + 1) << idx_bits)[:, None], (2 * N_EXPERTS + 1) << idx_bits)
    keys = lax.sort(jnp.concatenate([real_keys, fill_keys.reshape(n_fill)]))
    is_real = ((keys >> idx_bits) & 1) == 0
    row_tok = jnp.where(is_real, (keys & ((1 << idx_bits) - 1)) // TOP_K, 0)

    blk_start = jnp.arange(nb, dtype=jnp.int32) * MOE_BLOCK
    blk_e = jnp.minimum(jnp.sum((pad_end[None, :] <= blk_start[:, None]).astype(jnp.int32), axis=1),
                        N_EXPERTS - 1)
    n_used = (pad_end[-1] // MOE_BLOCK).astype(jnp.int32).reshape(1)
    has_blocks = pad_counts > 0
    order_e = jnp.cumsum(has_blocks.astype(jnp.int32)) - 1
    later = jnp.logical_and(has_blocks[None, :], experts[None, :] > experts[:, None])
    next_e = jnp.min(jnp.where(later, experts[None, :], N_EXPERTS), axis=1)
    next_e = jnp.where(next_e == N_EXPERTS, -1, next_e)
    blk_is = blk_e[:, None] == experts[None, :]

    def per_block(table):
        return jnp.sum(jnp.where(blk_is, table[None, :], 0), axis=1)

    blk_j = (blk_start - per_block(pad_start)) // MOE_BLOCK
    blk_n = jnp.maximum(per_block(pad_counts) // MOE_BLOCK, 1)
    sched = (blk_e, blk_j, blk_n, per_block(order_e), per_block(next_e), n_used)
    return probs, pos, row_tok, sched


def _rows(x, idx):
    return x.at[idx].get(mode="promise_in_bounds")


def _moe(h2, logits_t, layer, w_gu, b_gu, w_down, b_down):
    n_tok = h2.shape[0]
    probs, pos, row_tok, sched = _route(logits_t, n_tok)
    xs = _rows(h2, row_tok)
    yb = _experts(xs, sched, layer, w_gu, b_gu, w_down, b_down)
    y = _rows(yb, pos.reshape(-1)).reshape(TOP_K, n_tok, -1)
    return y, probs.T


def _combine_kernel(x_ref, y0_ref, y1_ref, y2_ref, y3_ref, p_ref, g2_ref, nf_ref, o_ref,
                    *, tile0, final, n_ctx_tiles, tiles_per_batch, ctx_row):
    row = _mod_row(pl.program_id(0) + tile0, n_ctx_tiles, tiles_per_batch, ctx_row)
    p = p_ref[...]
    y = (p[:, 0:1] * y0_ref[0] + p[:, 1:2] * y1_ref[0]) + (p[:, 2:3] * y2_ref[0] + p[:, 3:4] * y3_ref[0])
    x = x_ref[...] + g2_ref[pl.ds(row, 1), :] * y
    if final:
        x = x * lax.rsqrt(jnp.mean(x * x, axis=-1, keepdims=True) + EPS) * nf_ref[...]
    o_ref[...] = x


def _combine(x, y, probs, mod, norm_f, *, tile0, final, n_ctx_tiles, tiles_per_batch, ctx_row):
    assert y.shape[0] == TOP_K == 4
    _, n_rows, d = y.shape
    tm = ROW_TILE
    kern = functools.partial(_combine_kernel, tile0=tile0, final=final, n_ctx_tiles=n_ctx_tiles,
                             tiles_per_batch=tiles_per_batch, ctx_row=ctx_row)

    def yspec(k):
        return pl.BlockSpec((1, tm, d), lambda i: (k, i, 0))

    return pl.pallas_call(
        kern,
        grid=(n_rows // tm,),
        in_specs=[pl.BlockSpec((tm, d), lambda i: (tile0 + i, 0)),
                  yspec(0), yspec(1), yspec(2), yspec(3),
                  pl.BlockSpec((tm, TOP_K), lambda i: (i, 0)),
                  pl.BlockSpec((8, d), lambda i: (0, 5)),
                  pl.BlockSpec((1, d), lambda i: (0, 0))],
        out_specs=pl.BlockSpec((tm, d), lambda i: (i, 0)),
        out_shape=jax.ShapeDtypeStruct((n_rows, d), F32),
        compiler_params=_cparams("parallel"),
        name="moe_combine",
    )(x, y, y, y, y, probs, mod, norm_f)


def _embed_kernel(ctx_ref, x_ref, pos_ref, o_ref, *, n_ctx_tiles):
    j = pl.program_id(0)
    b = pl.program_id(1)

    @pl.when(jnp.logical_and(j == 0, b < n_ctx_tiles))
    def _():
        o_ref[...] = ctx_ref[...]

    @pl.when(j > 0)
    def _():
        o_ref[...] = x_ref[...] + pos_ref[...]


def _embed(ctx2, x2, pos, *, batch):
    tm = ROW_TILE
    d = x2.shape[1]
    l = pos.shape[0]
    n_ctx_tiles = ctx2.shape[0] // tm
    nt = l // tm
    assert n_ctx_tiles <= batch
    return pl.pallas_call(
        functools.partial(_embed_kernel, n_ctx_tiles=n_ctx_tiles),
        grid=(1 + nt, batch),
        in_specs=[pl.BlockSpec((tm, d), lambda j, b: (jnp.minimum(b, n_ctx_tiles - 1), 0)),
                  pl.BlockSpec((tm, d), lambda j, b: (b * nt + jnp.maximum(j - 1, 0), 0)),
                  pl.BlockSpec((tm, d), lambda j, b: (jnp.maximum(j - 1, 0), 0))],
        out_specs=pl.BlockSpec(
            (tm, d), lambda j, b: (jnp.where(j == 0, jnp.minimum(b, n_ctx_tiles - 1),
                                             n_ctx_tiles + b * nt + j - 1), 0)),
        out_shape=jax.ShapeDtypeStruct((ctx2.shape[0] + x2.shape[0], d), F32),
        compiler_params=_cparams("arbitrary", "arbitrary"),
        name="embed_tokens",
    )(ctx2, x2, pos)


def _grid_pos_embed(n, d):
    rows = n // GRID_W
    row = jnp.broadcast_to(jnp.arange(rows, dtype=F32)[:, None], (rows, GRID_W)).reshape(-1)
    col = jnp.broadcast_to(jnp.arange(GRID_W, dtype=F32)[None, :], (rows, GRID_W)).reshape(-1)
    quarter = d // 4
    omega = 1.0 / (10000.0 ** (jnp.arange(quarter, dtype=F32) / quarter))

    def enc(p):
        a = p[:, None] * omega
        return jnp.concatenate([jnp.sin(a), jnp.cos(a)], axis=-1)

    return jnp.concatenate([enc(row), enc(col)], axis=-1)


def kernel(x, c, ctx, c_ctx, w_mod, b_mod, norm1, w_in, w_a2, b_a, gla_norm, w_s, b_s, p_gla, p_sg, p_ft,
           w_out, norm2, w_router, b_router, w_gu, b_gu, w_down, b_down, norm_f):
    batch, l, d = x.shape
    lc = ctx.shape[1]
    depth = w_mod.shape[0]
    n_ctx_rows = batch * lc
    t_rows = n_ctx_rows + batch * l
    assert lc % ROW_TILE == 0 and l % ROW_TILE == 0 and batch < 8

    tm1 = math.gcd(math.gcd(n_ctx_rows, l), 1024)
    assert tm1 % ROW_TILE == 0
    proj_tiles = dict(n_ctx_tiles=n_ctx_rows // tm1, tiles_per_batch=l // tm1, ctx_row=batch)
    row_tiles = dict(n_ctx_tiles=n_ctx_rows // ROW_TILE, tiles_per_batch=l // ROW_TILE, ctx_row=batch)
    tm2 = ROW_TILE
    tm3 = math.gcd(tm1, 2 * ROW_TILE)
    big_tiles = dict(n_ctx_tiles=n_ctx_rows // tm3, tiles_per_batch=l // tm3, ctx_row=batch)

    o_q = 0
    o_k = o_q + W_Q
    o_v = o_k + W_Q
    o_r = o_v + W_V
    o_a = o_r + W_V
    o_su = o_a + 2 * GLA_RANK
    o_sv = o_su + W_SG
    o_f = o_sv + W_SG
    o_g = o_f + W_FT
    n_gates = 3 * d
    cols = dict(gates=0, q=n_gates, r=n_gates + 2 * W_Q + W_V, su=n_gates + 2 * W_Q + 2 * W_V)
    cols["sv"] = cols["su"] + W_SG
    cols["f"] = cols["sv"] + W_SG
    n_main = cols["f"] + W_FT

    xt = _embed(ctx.reshape(n_ctx_rows, d), x.reshape(batch * l, d), _grid_pos_embed(l, d), batch=batch)

    cond = jnp.zeros((8, d), F32).at[:batch].set(c).at[batch].set(c_ctx)
    mods = _mod_vectors(cond, w_mod, b_mod)

    cc, sc = _dft_tables(FT_CH)
    chan_tab = jnp.concatenate([cc, sc], axis=1)
    cos_l, sin_l = _dft_tables(l)
    cos_c, sin_c = _dft_tables(lc)

    w_main = jnp.concatenate(
        [w_in[:, :, o_g:], w_in[:, :, o_q:o_a], w_in[:, :, o_su:o_g]], axis=2).astype(BF16)
    w_alr = jnp.pad(w_in[:, :, o_a:o_su], ((0, 0), (0, 0), (0, LANES - 2 * GLA_RANK))).astype(BF16)

    for i in range(depth):
        last = i == depth - 1
        w_a = w_alr[i]
        wa2 = jnp.zeros((LANES, 2 * W_Q), F32)
        wa2 = wa2.at[:GLA_RANK, :W_Q].set(w_a2[i, 0]).at[GLA_RANK:2 * GLA_RANK, W_Q:].set(w_a2[i, 1])
        mod = mods[i]

        z, la = _project(xt, mod, norm1[i][None], w_main, w_a, wa2.astype(BF16), b_a[i].reshape(1, 2 * W_Q),
                         layer=i, tm=tm1, tn=1536, **proj_tiles)
        assert z.shape[1] == n_main

        o_fw, o_bw = _gla(z, la, batch=batch, lc=lc, l=l, col_q=cols["q"])

        xc_c, xs_c = _ft_channel(z, chan_tab, col_f=cols["f"], row0=0, n_rows=n_ctx_rows)
        xc_l, xs_l = _ft_channel(z, chan_tab, col_f=cols["f"], row0=n_ctx_rows, n_rows=batch * l)
        y_ft = (_ft_positions(cos_c, sin_c, xc_c, xs_c, batch=batch, seq=lc),
                _ft_positions(cos_l, sin_l, xc_l, xs_l, batch=batch, seq=l))

        bs_b = jnp.broadcast_to(b_s[i][:, :, None], (SG_GROUPS, SG_CHUNK, SG_CH))
        merged = _branches(z, o_fw, o_bw, y_ft, gla_norm[i][None], w_s[i].astype(BF16), bs_b,
                           p_gla[i].astype(BF16), p_sg[i].astype(BF16), p_ft[i].astype(BF16), cols=cols, tm=tm2)

        w_r = jnp.pad(w_router[i], ((0, 0), (0, LANES - N_EXPERTS)))
        b_r = jnp.pad(b_router[i], (0, LANES - N_EXPERTS)).reshape(1, LANES)
        xt, h2, logits_t = _out_project(merged, xt, w_out[i].astype(BF16), mod, norm2[i][None], w_r, b_r,
                                      tm=tm3, **big_tiles)

        row0 = n_ctx_rows if last else 0
        y, probs = _moe(h2[row0:], logits_t[:N_EXPERTS, row0:], i, w_gu, b_gu, w_down, b_down)
        xt = _combine(xt, y, probs, mod, norm_f[None], tile0=row0 // ROW_TILE, final=last, **row_tiles)

    return xt.reshape(batch, l, d)
```

```python
import functools
import math

import jax
import jax.numpy as jnp
from jax import lax
from jax.experimental import pallas as pl
from jax.experimental.pallas import tpu as pltpu

F32 = jnp.float32
BF16 = jnp.bfloat16

EPS = 1e-6
GRID_W = 64
GLA_HEADS = 4
GLA_DK = 128
GLA_DV = 256
GLA_RANK = 16
GLA_TAU = 16.0
GLA_CHUNK = 64
GLA_SUB = 16
GLA_STAGES = 3
SG_GROUPS = 4
SG_CH = 128
SG_CHUNK = 128
FT_GROUPS = 4
FT_CH = 128
N_EXPERTS = 32
TOP_K = 4
SWIGLU_LIMIT = 7.0
SWIGLU_ALPHA = 1.702
MOE_BLOCK = 256
W_CHUNK = 512
TC_GATHER_SHARE = 4

LOG2E = math.log2(math.e)
LANES = 128
ROW_TILE = 256
VMEM_LIMIT = 56 * 1024 * 1024

W_Q = GLA_HEADS * GLA_DK
W_V = GLA_HEADS * GLA_DV
W_SG = SG_GROUPS * SG_CH
W_FT = FT_GROUPS * FT_CH


def _cparams(*sem):
    return pltpu.CompilerParams(dimension_semantics=sem, vmem_limit_bytes=VMEM_LIMIT)


def _sigmoid(x):
    return 1.0 / (1.0 + jnp.exp(-x))


def _gelu_tanh(x):
    return 0.5 * x * (1.0 + jnp.tanh(math.sqrt(2.0 / math.pi) * (x + 0.044715 * (x * x * x))))


def _dot(a, b):
    return jnp.dot(a, b, preferred_element_type=F32)


def _pack_bf16_pairs(x):
    bits = lax.bitcast_convert_type(x.astype(F32), jnp.uint32)
    half = x.shape[1] // 2
    return (bits[:, :half] >> 16) | (bits[:, half:] & jnp.uint32(0xFFFF0000))


def _unpack_bf16_pairs(u):
    lo = lax.bitcast_convert_type(u << 16, F32)
    hi = lax.bitcast_convert_type(u & jnp.uint32(0xFFFF0000), F32)
    return jnp.concatenate([lo, hi], axis=1).astype(BF16)


def _mod_row(tile, n_ctx_tiles, tiles_per_batch, ctx_row):
    return jnp.where(tile < n_ctx_tiles, ctx_row, (tile - n_ctx_tiles) // tiles_per_batch)


def _mod_kernel(c_ref, w_ref, b_ref, o_ref):
    c = c_ref[...]
    s = (c * _sigmoid(c)).astype(BF16)
    o_ref[0] = _dot(s, w_ref[0].astype(BF16)) + b_ref[0]


def _mod_vectors(cond, w_mod, b_mod):
    depth, d, n = w_mod.shape
    tn = 1024
    return pl.pallas_call(
        _mod_kernel,
        grid=(depth, n // tn),
        in_specs=[
            pl.BlockSpec((8, d), lambda l, j: (0, 0)),
            pl.BlockSpec((1, d, tn), lambda l, j: (l, 0, j)),
            pl.BlockSpec((1, 1, tn), lambda l, j: (l, 0, j)),
        ],
        out_specs=pl.BlockSpec((1, 8, tn), lambda l, j: (l, 0, j)),
        out_shape=jax.ShapeDtypeStruct((depth, 8, n), F32),
        compiler_params=_cparams("parallel", "parallel"),
        name="mod_vectors",
    )(cond, w_mod, b_mod.reshape(depth, 1, n))


def _proj_kernel(x_ref, sh_ref, sc_ref, g_ref, w_ref, wa_ref, wa2_ref, ba_ref,
                 z_ref, la_ref, a_scr, *, n_ctx_tiles, tiles_per_batch, ctx_row):
    i = pl.program_id(0)
    j = pl.program_id(1)

    @pl.when(j == 0)
    def _():
        row = _mod_row(i, n_ctx_tiles, tiles_per_batch, ctx_row)
        scale = 1.0 + sc_ref[pl.ds(row, 1), :]
        shift = sh_ref[pl.ds(row, 1), :]

        def body(r, carry):
            rows = pl.ds(pl.multiple_of(r * ROW_TILE, ROW_TILE), ROW_TILE)
            x = x_ref[rows, :]
            y = x * lax.rsqrt(jnp.mean(x * x, axis=-1, keepdims=True) + EPS) * g_ref[...]
            hb = (y * scale + shift).astype(BF16)
            a_scr[rows, :] = hb
            a_lr = _dot(hb, wa_ref[...])
            pre = _dot(a_lr.astype(BF16), wa2_ref[...]) + ba_ref[...]
            la_ref[rows, :] = -(jnp.maximum(-pre, 0.0) + jnp.log1p(jnp.exp(-jnp.abs(pre)))) * (1.0 / GLA_TAU)
            return carry

        lax.fori_loop(0, x_ref.shape[0] // ROW_TILE, body, 0)

    z_ref[...] = _dot(a_scr[...], w_ref[0]).astype(BF16)


def _project(x, mod, g, w_main, w_a, w_a2, b_a, *, layer, tm, tn, n_ctx_tiles, tiles_per_batch, ctx_row):
    t, d = x.shape
    n = w_main.shape[2]
    kern = functools.partial(_proj_kernel, n_ctx_tiles=n_ctx_tiles,
                             tiles_per_batch=tiles_per_batch, ctx_row=ctx_row)
    return pl.pallas_call(
        kern,
        grid=(t // tm, n // tn),
        in_specs=[
            pl.BlockSpec((tm, d), lambda i, j: (i, 0)),
            pl.BlockSpec((8, d), lambda i, j: (0, 0)),
            pl.BlockSpec((8, d), lambda i, j: (0, 1)),
            pl.BlockSpec((1, d), lambda i, j: (0, 0)),
            pl.BlockSpec((1, d, tn), lambda i, j: (layer, 0, j)),
            pl.BlockSpec((d, LANES), lambda i, j: (0, 0)),
            pl.BlockSpec((LANES, 2 * W_Q), lambda i, j: (0, 0)),
            pl.BlockSpec((1, 2 * W_Q), lambda i, j: (0, 0)),
        ],
        out_specs=[
            pl.BlockSpec((tm, tn), lambda i, j: (i, j)),
            pl.BlockSpec((tm, 2 * W_Q), lambda i, j: (i, 0)),
        ],
        out_shape=[
            jax.ShapeDtypeStruct((t, n), BF16),
            jax.ShapeDtypeStruct((t, 2 * W_Q), F32),
        ],
        scratch_shapes=[pltpu.VMEM((tm, d), BF16)],
        compiler_params=_cparams("parallel", "arbitrary"),
        name="norm_mod_project",
    )(x, mod, mod, g, w_main, w_a, w_a2, b_a)


def _gla_pairs(rev):
    c, sub = GLA_CHUNK, GLA_SUB
    half = c // 2
    if rev:
        return ((0, half, half, c, half - 1),
                (0, sub, sub, half, sub - 1),
                (half, half + sub, half + sub, c, half + sub - 1))
    return ((half, c, 0, half, half),
            (sub, half, 0, sub, sub),
            (half + sub, c, half, half + sub, half + sub))


def _place_rows(x, r0, n):
    parts = []
    if r0:
        parts.append(jnp.zeros((r0, x.shape[1]), x.dtype))
    parts.append(x)
    rest = n - r0 - x.shape[0]
    if rest:
        parts.append(jnp.zeros((rest, x.shape[1]), x.dtype))
    return jnp.concatenate(parts, axis=0)


def _gla_chunk(q_ref, k_ref, v_ref, la_ref, e_ref, o_ref, s_ref, cum_ref, kc_ref, r0, h, rev):
    c, sub = GLA_CHUNK, GLA_SUB
    nsub = c // sub
    rows = pl.ds(r0, c)
    kcols = slice(h * GLA_DK, (h + 1) * GLA_DK)
    vcols = slice(h * GLA_DV, (h + 1) * GLA_DV)
    q = q_ref[rows, kcols].astype(F32) * (GLA_DK ** -0.5)
    k = k_ref[rows, kcols].astype(F32)
    v = v_ref[rows, vcols]
    la = la_ref[rows, kcols]

    ri = lax.broadcasted_iota(jnp.int32, (c, c), 0)
    ci = lax.broadcasted_iota(jnp.int32, (c, c), 1)
    before = (ci >= ri) if rev else (ci <= ri)
    tri = jnp.where(before, 1.0, 0.0).astype(BF16)
    la_hi = la.astype(BF16)
    rem = la - la_hi.astype(F32)
    la_mid = rem.astype(BF16)
    la_lo = (rem - la_mid.astype(F32)).astype(BF16)
    cum = (_dot(tri, la_hi) + _dot(tri, la_mid) + _dot(tri, la_lo)) * LOG2E
    cum_ref[...] = cum
    kc_ref[...] = k
    yield
    last = 0 if rev else c - 1
    total = cum[last:last + 1, :]

    s = s_ref[...]
    o = _dot((q * jnp.exp2(cum)).astype(BF16), s.astype(BF16))

    qparts, kparts = [], []
    for l0, l1, e0, e1, ref in _gla_pairs(rev):
        cr = cum[ref:ref + 1, :]
        qparts.append(_place_rows(q[l0:l1] * jnp.exp2(cum[l0:l1] - cr), l0, c))
        kparts.append(_place_rows(k[e0:e1] * jnp.exp2(cr - cum[e0:e1]), e0, c))
    s_off = lax.dot_general(jnp.concatenate(qparts, axis=1).astype(BF16),
                            jnp.concatenate(kparts, axis=1).astype(BF16),
                            (((1,), (1,)), ((), ())), preferred_element_type=F32)

    hs = sub // 2

    def halves(x, which):
        return jnp.concatenate([x[a * sub + w * hs:a * sub + (w + 1) * hs] for a in range(nsub) for w in which],
                               axis=0)

    parts = []
    for jj in range(sub):
        if rev:
            which = (0, 1) if jj >= hs else (0,)
        else:
            which = (0, 1) if jj < hs else (1,)
        nrow = hs * len(which)
        cj = jnp.concatenate(
            [jnp.broadcast_to(cum_ref[a * sub + jj:a * sub + jj + 1, :], (nrow, GLA_DK)) for a in range(nsub)],
            axis=0)
        kj = jnp.concatenate(
            [jnp.broadcast_to(kc_ref[a * sub + jj:a * sub + jj + 1, :], (nrow, GLA_DK)) for a in range(nsub)],
            axis=0)
        prod = halves(q, which) * jnp.exp2(jnp.minimum(halves(cum, which) - cj, 0.0)) * kj
        if len(which) == 1:
            zero = jnp.zeros((hs, GLA_DK), F32)
            pieces = []
            for a in range(nsub):
                live = prod[a * hs:(a + 1) * hs]
                pieces += [live, zero] if which[0] == 0 else [zero, live]
            prod = jnp.concatenate(pieces, axis=0)
        parts.append(prod.astype(BF16))
    diag = _dot(jnp.concatenate(parts, axis=1), e_ref[...])
    yield

    lag = (ci - ri) if rev else (ri - ci)
    room = (sub - 1 - ri % sub) if rev else (ri % sub)
    in_diag = lag.astype(jnp.uint32) <= room.astype(jnp.uint32)
    p = jnp.where(in_diag, diag, 0.0) + s_off
    o_ref[rows, vcols] = o + _dot(p.astype(BF16), v)

    k_out = (k * jnp.exp2(total - cum)).astype(BF16)
    upd = lax.dot_general(k_out, v, (((0,), (0,)), ((), ())), preferred_element_type=F32)
    dmat = jnp.broadcast_to(jnp.exp2(total), (GLA_DK, GLA_DK)).T
    for half in range(GLA_DV // GLA_DK):
        cols = slice(half * GLA_DK, (half + 1) * GLA_DK)
        s_ref[:, cols] = s[:, cols] * dmat + upd[:, cols]


def _gla_kernel(qf, kf, vf, laf, qb, kb, vb, lab, e_ref, of_ref, ob_ref, *scratch):
    n = 2 * GLA_HEADS
    states, cums, keys = scratch[:n], scratch[n:2 * n], scratch[2 * n:]

    @pl.when(pl.program_id(1) == 0)
    def _():
        for s_ref in states:
            s_ref[...] = jnp.zeros_like(s_ref)

    nchunk = ROW_TILE // GLA_CHUNK

    def body(c, carry):
        r_f = pl.multiple_of(c * GLA_CHUNK, GLA_CHUNK)
        r_b = pl.multiple_of((nchunk - 1 - c) * GLA_CHUNK, GLA_CHUNK)
        units = []
        for h in range(GLA_HEADS):
            f, b = h, GLA_HEADS + h
            units.append(_gla_chunk(qf, kf, vf, laf, e_ref, of_ref, states[f], cums[f], keys[f], r_f, h, False))
            units.append(_gla_chunk(qb, kb, vb, lab, e_ref, ob_ref, states[b], cums[b], keys[b], r_b, h, True))
        for _ in range(GLA_STAGES):
            for u in units:
                next(u, None)
        return carry

    lax.fori_loop(0, nchunk, body, 0)


def _gla(z, la, *, batch, lc, l, col_q):
    t_rows = z.shape[0]
    nctx = lc // ROW_TILE
    nlat = l // ROW_TILE
    steps = nctx + nlat
    qb0 = col_q // W_Q
    kb0 = (col_q + W_Q) // W_Q
    vb0 = (col_q + 2 * W_Q) // W_V

    def fwd_tile(b, t):
        return jnp.where(t < nctx, b * nctx + t, batch * nctx + b * nlat + (t - nctx))

    def bwd_tile(b, t):
        return jnp.where(t < nctx, b * nctx + (nctx - 1 - t), batch * nctx + b * nlat + (steps - 1 - t))

    def spec(width, tile_fn, col):
        return pl.BlockSpec((ROW_TILE, width), lambda b, t: (tile_fn(b, t), col))

    kk = jnp.arange(GLA_SUB * GLA_DK, dtype=jnp.int32)[:, None] // GLA_DK
    cc = jnp.arange(GLA_CHUNK, dtype=jnp.int32)[None, :] % GLA_SUB
    e_mat = (kk == cc).astype(BF16)

    o_shape = jax.ShapeDtypeStruct((t_rows, W_V), F32)
    n_units = 2 * GLA_HEADS
    state = pltpu.VMEM((GLA_DK, GLA_DV), F32)
    chunk = pltpu.VMEM((GLA_CHUNK, GLA_DK), F32)
    return pl.pallas_call(
        _gla_kernel,
        grid=(batch, steps),
        in_specs=[
            spec(W_Q, fwd_tile, qb0), spec(W_Q, fwd_tile, kb0), spec(W_V, fwd_tile, vb0), spec(W_Q, fwd_tile, 0),
            spec(W_Q, bwd_tile, qb0), spec(W_Q, bwd_tile, kb0), spec(W_V, bwd_tile, vb0), spec(W_Q, bwd_tile, 1),
            pl.BlockSpec((GLA_SUB * GLA_DK, GLA_CHUNK), lambda b, t: (0, 0)),
        ],
        out_specs=[spec(W_V, fwd_tile, 0), spec(W_V, bwd_tile, 0)],
        out_shape=[o_shape, o_shape],
        scratch_shapes=[state] * n_units + [chunk] * (2 * n_units),
        compiler_params=_cparams("parallel", "arbitrary"),
        name="gla_scan",
    )(z, z, z, la, z, z, z, la, e_mat)


def _dft_tables(n):
    scale = n ** -0.5
    p = jnp.arange(n, dtype=jnp.int32)[:, None]

    def trig(cols, period):
        ang = ((p * cols[None, :]) % period).astype(F32) * (2.0 * math.pi / period)
        return jnp.cos(ang), jnp.sin(ang)

    n2 = 1
    while n2 * n2 < n:
        n2 *= 2
    if n % n2 or n2 == 1:
        c, s = trig(jnp.arange(n, dtype=jnp.int32), n)
        return (c * scale).astype(BF16), (s * scale).astype(BF16)
    n1 = n // n2
    ca, sa = trig(jnp.arange(n1, dtype=jnp.int32), n1)
    cb, sb = trig(jnp.arange(n2, dtype=jnp.int32), n)
    ca, sa, cb, sb = ca[:, :, None], sa[:, :, None], cb[:, None, :] * scale, sb[:, None, :] * scale
    return ((ca * cb - sa * sb).reshape(n, n).astype(BF16), (sa * cb + ca * sb).reshape(n, n).astype(BF16))


def _ft_chan_kernel(f_ref, cs_ref, xc_ref, xs_ref):
    for g in range(FT_GROUPS):
        cols = slice(g * FT_CH, (g + 1) * FT_CH)
        y = _dot(f_ref[:, cols], cs_ref[...])
        xc_ref[:, cols] = y[:, :FT_CH].astype(BF16)
        xs_ref[:, cols] = y[:, FT_CH:].astype(BF16)


def _ft_channel(z, cs, *, col_f, row0, n_rows):
    tm = math.gcd(math.gcd(row0, n_rows), 1024)
    shape = jax.ShapeDtypeStruct((n_rows, W_FT), BF16)
    return pl.pallas_call(
        _ft_chan_kernel,
        grid=(n_rows // tm,),
        in_specs=[pl.BlockSpec((tm, W_FT), lambda i: (row0 // tm + i, col_f // W_FT)),
                  pl.BlockSpec((FT_CH, 2 * FT_CH), lambda i: (0, 0))],
        out_specs=[pl.BlockSpec((tm, W_FT), lambda i: (i, 0))] * 2,
        out_shape=[shape, shape],
        compiler_params=_cparams("parallel"),
        name="fourier_channels",
    )(z, cs)


def _ft_pos_kernel(c_ref, s_ref, xc_ref, xs_ref, o_ref):
    o_ref[...] = _dot(c_ref[...], xc_ref[...]) - _dot(s_ref[...], xs_ref[...])


def _ft_positions(cos_t, sin_t, xc, xs, *, batch, seq):
    nt = seq // ROW_TILE
    x_spec = pl.BlockSpec((seq, W_FT), lambda b, j: (b, 0))
    t_spec = pl.BlockSpec((ROW_TILE, seq), lambda b, j: (j, 0))
    return pl.pallas_call(
        _ft_pos_kernel,
        grid=(batch, nt),
        in_specs=[t_spec, t_spec, x_spec, x_spec],
        out_specs=pl.BlockSpec((ROW_TILE, W_FT), lambda b, j: (b * nt + j, 0)),
        out_shape=jax.ShapeDtypeStruct((batch * seq, W_FT), F32),
        compiler_params=_cparams("parallel", "parallel"),
        name="fourier_positions",
    )(cos_t, sin_t, xc, xs)


def _branch_kernel(gg_ref, gs_ref, gf_ref, of_ref, ob_ref, r_ref, su_ref, sv_ref, yc_ref, yl_ref,
                   gn_ref, ws_ref, bs_ref, pg_ref, ps_ref, pf_ref, m_ref, *, n_ctx_tiles):
    o = of_ref[...] + ob_ref[...]
    heads = []
    for h in range(GLA_HEADS):
        oh = o[:, h * GLA_DV:(h + 1) * GLA_DV]
        heads.append(oh * lax.rsqrt(jnp.mean(oh * oh, axis=-1, keepdims=True) + EPS))
    r = r_ref[...].astype(F32)
    y_gla = (jnp.concatenate(heads, axis=-1) * gn_ref[...]) * (r * _sigmoid(r))
    acc = _sigmoid(gg_ref[...].astype(F32)) * _dot(y_gla.astype(BF16), pg_ref[...])

    u = _gelu_tanh(su_ref[...].astype(F32))
    v = _gelu_tanh(sv_ref[...].astype(F32))
    groups = []
    for g in range(SG_GROUPS):
        vg = v[:, g * SG_CH:(g + 1) * SG_CH]
        mu = jnp.mean(vg, axis=-1, keepdims=True)
        var = jnp.mean(jnp.square(vg - mu), axis=-1, keepdims=True)
        vn = ((vg - mu) * lax.rsqrt(var + EPS)).astype(BF16)
        chunks = []
        for c in range(v.shape[0] // SG_CHUNK):
            chunks.append(_dot(ws_ref[g], vn[c * SG_CHUNK:(c + 1) * SG_CHUNK, :]) + bs_ref[g])
        groups.append(jnp.concatenate(chunks, axis=0))
    y_sg = u * jnp.concatenate(groups, axis=-1)
    acc = acc + _sigmoid(gs_ref[...].astype(F32)) * _dot(y_sg.astype(BF16), ps_ref[...])

    y_ft = jnp.where(pl.program_id(0) < n_ctx_tiles, yc_ref[...], yl_ref[...])
    acc = acc + _sigmoid(gf_ref[...].astype(F32)) * _dot(y_ft.astype(BF16), pf_ref[...])
    m_ref[...] = acc.astype(BF16)


def _branches(z, o_f, o_b, y_ft, gla_norm, w_s, b_s, p_gla, p_sg, p_ft, *, cols, tm):
    t_rows = z.shape[0]
    d = p_gla.shape[1]
    n_ctx_tiles = y_ft[0].shape[0] // tm
    kern = functools.partial(_branch_kernel, n_ctx_tiles=n_ctx_tiles)

    def zspec(width, col0):
        return pl.BlockSpec((tm, width), lambda i: (i, col0 // width))

    def rows(width):
        return pl.BlockSpec((tm, width), lambda i: (i, 0))

    def const(shape):
        return pl.BlockSpec(shape, lambda i: (0,) * len(shape))

    return pl.pallas_call(
        kern,
        grid=(t_rows // tm,),
        in_specs=[
            zspec(d, cols["gates"]), zspec(d, cols["gates"] + d), zspec(d, cols["gates"] + 2 * d),
            rows(W_V), rows(W_V), zspec(W_V, cols["r"]), zspec(W_SG, cols["su"]), zspec(W_SG, cols["sv"]),
            pl.BlockSpec((tm, W_FT), lambda i: (jnp.minimum(i, n_ctx_tiles - 1), 0)),
            pl.BlockSpec((tm, W_FT), lambda i: (jnp.maximum(i - n_ctx_tiles, 0), 0)),
            const((1, W_V)), const((SG_GROUPS, SG_CHUNK, SG_CHUNK)), const((SG_GROUPS, SG_CHUNK, SG_CH)),
            const((W_V, d)), const((W_SG, d)), const((W_FT, d)),
        ],
        out_specs=rows(d),
        out_shape=jax.ShapeDtypeStruct((t_rows, d), BF16),
        compiler_params=_cparams("parallel"),
        name="branch_merge",
    )(z, z, z, o_f, o_b, z, z, z, y_ft[0], y_ft[1], gla_norm, w_s, b_s, p_gla, p_sg, p_ft)


def _outproj_kernel(m_ref, x_ref, w_ref, g1_ref, n2_ref, sh_ref, sc_ref, wr_ref, br_ref,
                    xo_ref, h_ref, lg_ref, *, n_ctx_tiles, tiles_per_batch, ctx_row):
    row = _mod_row(pl.program_id(0), n_ctx_tiles, tiles_per_batch, ctx_row)
    gate1 = g1_ref[pl.ds(row, 1), :]
    scale2 = 1.0 + sc_ref[pl.ds(row, 1), :]
    shift2 = sh_ref[pl.ds(row, 1), :]
    w = wr_ref[...]
    w_hi = w.astype(BF16)
    w_lo = (w - w_hi.astype(F32)).astype(BF16)
    for r0 in range(0, x_ref.shape[0], ROW_TILE):
        rows = slice(r0, r0 + ROW_TILE)
        x = x_ref[rows, :] + gate1 * _dot(m_ref[rows, :], w_ref[...])
        xo_ref[rows, :] = x
        y = x * lax.rsqrt(jnp.mean(x * x, axis=-1, keepdims=True) + EPS) * n2_ref[...]
        h = y * scale2 + shift2
        h_hi = h.astype(BF16)
        h_ref[rows, :] = _pack_bf16_pairs(h_hi)
        h_lo = (h - h_hi.astype(F32)).astype(BF16)
        lg = _dot(h_hi, w_hi) + (_dot(h_hi, w_lo) + _dot(h_lo, w_hi)) + br_ref[...]
        lg_ref[:, rows] = lg.T


def _out_project(merged, x, w_out, mod, norm2, w_router, b_router, *, tm, n_ctx_tiles, tiles_per_batch, ctx_row):
    t_rows, d = x.shape
    kern = functools.partial(_outproj_kernel, n_ctx_tiles=n_ctx_tiles,
                             tiles_per_batch=tiles_per_batch, ctx_row=ctx_row)

    def rows(width):
        return pl.BlockSpec((tm, width), lambda i: (i, 0))

    def modspec(k):
        return pl.BlockSpec((8, d), lambda i: (0, k))

    return pl.pallas_call(
        kern,
        grid=(t_rows // tm,),
        in_specs=[rows(d), rows(d), pl.BlockSpec((d, d), lambda i: (0, 0)),
                  modspec(2), pl.BlockSpec((1, d), lambda i: (0, 0)), modspec(3), modspec(4),
                  pl.BlockSpec((d, LANES), lambda i: (0, 0)), pl.BlockSpec((1, LANES), lambda i: (0, 0))],
        out_specs=[rows(d), rows(d // 2), pl.BlockSpec((LANES, tm), lambda i: (0, i))],
        out_shape=[jax.ShapeDtypeStruct((t_rows, d), F32), jax.ShapeDtypeStruct((t_rows, d // 2), jnp.uint32),
                   jax.ShapeDtypeStruct((LANES, t_rows), F32)],
        compiler_params=_cparams("parallel"),
        name="out_project",
    )(merged, x, w_out, mod, norm2, mod, mod, w_router, b_router)


def _expert_kernel(be_ref, jb_ref, nblk_ref, ord_ref, nxt_ref, nu_ref,
                   xa_ref, xb_ref, wgu_hbm, bgu_ref, wd_hbm, bd_ref, y_ref,
                   wgu_bf, wd_bf, stage, sems, *, layer, n_a):
    i = pl.program_id(0)
    n_gu_chunks = wgu_bf.shape[1] // W_CHUNK
    n_chunks = n_gu_chunks + wd_bf.shape[1] // W_CHUNK

    def chunk_copy(e, q):
        if q < n_gu_chunks:
            src = wgu_hbm.at[layer, e, pl.ds(q * W_CHUNK, W_CHUNK), :]
        else:
            src = wd_hbm.at[layer, e, pl.ds((q - n_gu_chunks) * W_CHUNK, W_CHUNK), :]
        return pltpu.make_async_copy(src, stage.at[q % 2], sems.at[q % 2])

    def land(e, q, slot):
        chunk_copy(e, q).wait()
        w = stage[q % 2].astype(BF16)
        if q < n_gu_chunks:
            wgu_bf[slot, pl.ds(q * W_CHUNK, W_CHUNK), :] = w
        else:
            wd_bf[slot, pl.ds((q - n_gu_chunks) * W_CHUNK, W_CHUNK), :] = w
        if q + 2 < n_chunks:
            chunk_copy(e, q + 2).start()

    active = i < nu_ref[0]
    e = be_ref[i]
    j = jb_ref[i]
    n = nblk_ref[i]
    slot = ord_ref[i] % 2
    nxt = nxt_ref[i]

    @pl.when(i == 0)
    def _():
        chunk_copy(e, 0).start()
        chunk_copy(e, 1).start()
        for q in range(n_chunks):
            land(e, q, slot)

    prefetch = jnp.logical_and(active, nxt >= 0)

    @pl.when(jnp.logical_and(prefetch, j == 0))
    def _():
        chunk_copy(nxt, 0).start()
        chunk_copy(nxt, 1).start()

    @pl.when(active)
    def _():
        x = jnp.where(i < n_a, xa_ref[...], xb_ref[...])
        gu = _dot(_unpack_bf16_pairs(x), wgu_bf[slot]) + bgu_ref[0, 0]
        de = gu.shape[1] // 2
        gate = jnp.minimum(gu[:, :de], SWIGLU_LIMIT)
        up = jnp.clip(gu[:, de:], -SWIGLU_LIMIT, SWIGLU_LIMIT)
        hdn = (up + 1.0) * (gate * _sigmoid(SWIGLU_ALPHA * gate))
        y_ref[...] = _dot(hdn.astype(BF16), wd_bf[slot]) + bd_ref[0, 0]

    @pl.when(jnp.logical_not(active))
    def _():
        y_ref[...] = jnp.zeros_like(y_ref)

    lo = (n_chunks * j) // n
    hi = (n_chunks * (j + 1)) // n
    for q in range(n_chunks):
        @pl.when(jnp.logical_and(prefetch, jnp.logical_and(lo <= q, q < hi)))
        def _(q=q):
            land(nxt, q, 1 - slot)


def _experts(xs_a, xs_b, sched, layer, w_gu, b_gu, w_down, b_down):
    n_a = xs_a.shape[0] // MOE_BLOCK
    rows_total = xs_a.shape[0] + xs_b.shape[0]
    nb = rows_total // MOE_BLOCK
    depth, ne, d, n_gu = w_gu.shape
    assert xs_a.shape[1] * 2 == d and xs_a.dtype == jnp.uint32 and xs_b.dtype == jnp.uint32
    de = w_down.shape[2]
    assert n_gu == d and d % W_CHUNK == 0 and de % W_CHUNK == 0

    def blk(i, *s):
        return jnp.minimum(i, s[-1][0] - 1)

    def bspec(n):
        return pl.BlockSpec((1, 1, 1, n), lambda i, *s: (layer, s[0][blk(i, *s)], 0, 0))

    grid_spec = pltpu.PrefetchScalarGridSpec(
        num_scalar_prefetch=len(sched),
        grid=(nb,),
        in_specs=[
            pl.BlockSpec((MOE_BLOCK, d // 2), lambda i, *s: (jnp.minimum(blk(i, *s), n_a - 1), 0)),
            pl.BlockSpec((MOE_BLOCK, d // 2), lambda i, *s: (jnp.maximum(blk(i, *s) - n_a, 0), 0)),
            pl.BlockSpec(memory_space=pl.ANY), bspec(n_gu),
            pl.BlockSpec(memory_space=pl.ANY), bspec(d),
        ],
        out_specs=pl.BlockSpec((MOE_BLOCK, d), lambda i, *s: (i, 0)),
        scratch_shapes=[pltpu.VMEM((2, d, n_gu), BF16), pltpu.VMEM((2, de, d), BF16),
                        pltpu.VMEM((2, W_CHUNK, d), F32), pltpu.SemaphoreType.DMA((2,))],
    )
    return pl.pallas_call(
        functools.partial(_expert_kernel, layer=layer, n_a=n_a),
        grid_spec=grid_spec,
        out_shape=jax.ShapeDtypeStruct((rows_total, d), F32),
        compiler_params=_cparams("arbitrary"),
        name="moe_experts",
    )(*sched, xs_a, xs_b, w_gu, b_gu.reshape(depth, ne, 1, n_gu), w_down, b_down.reshape(depth, ne, 1, d))


def _gather_kernel(idx_ref, src_hbm, out_hbm, sem):
    n = idx_ref.shape[0]
    base = pl.program_id(0) * n

    def row_copy(r, src_row):
        return pltpu.make_async_copy(src_hbm.at[pl.ds(src_row, 1)], out_hbm.at[pl.ds(base + r, 1)], sem)

    def issue(r, carry):
        row_copy(r, idx_ref[r]).start()
        return carry

    def drain(r, carry):
        row_copy(r, 0).wait()
        return carry

    lax.fori_loop(0, n, issue, 0)
    lax.fori_loop(0, n, drain, 0)


def _gather_rows(src, idx):
    n = idx.shape[0]
    return pl.pallas_call(
        _gather_kernel,
        grid=(n // MOE_BLOCK,),
        in_specs=[pl.BlockSpec((MOE_BLOCK,), lambda i: (i,), memory_space=pltpu.SMEM),
                  pl.BlockSpec(memory_space=pl.ANY)],
        out_specs=pl.BlockSpec(memory_space=pl.ANY),
        out_shape=jax.ShapeDtypeStruct((n,) + src.shape[1:], src.dtype),
        scratch_shapes=[pltpu.SemaphoreType.DMA(())],
        compiler_params=_cparams("arbitrary"),
        name="gather_rows",
    )(idx, src)


def _route(logits_t, n_tok):
    experts = jnp.arange(N_EXPERTS, dtype=jnp.int32)
    lg = logits_t
    e_iota = lax.broadcasted_iota(jnp.int32, lg.shape, 0)
    vals, idxs = [], []
    for _ in range(TOP_K):
        mx = jnp.max(lg, axis=0)
        ix = jnp.min(jnp.where(lg == mx[None, :], e_iota, N_EXPERTS), axis=0)
        vals.append(mx)
        idxs.append(ix)
        lg = jnp.where(e_iota == ix[None, :], -jnp.inf, lg)
    top_val = jnp.stack(vals)
    top_idx = jnp.stack(idxs)
    probs = jax.nn.softmax(top_val, axis=0)

    m = n_tok * TOP_K
    onehot = (top_idx[:, :, None] == experts[None, None, :]).astype(jnp.int32)
    per_tok = jnp.sum(onehot, axis=0)
    csum = jnp.cumsum(per_tok, axis=0)
    counts = csum[-1]
    pad_counts = (counts + MOE_BLOCK - 1) // MOE_BLOCK * MOE_BLOCK
    pad_end = jnp.cumsum(pad_counts)
    pad_start = pad_end - pad_counts
    base = csum - per_tok + pad_start[None, :]
    pos = jnp.sum(base[None, :, :] * onehot, axis=2)

    nb = (m + MOE_BLOCK - 1) // MOE_BLOCK + N_EXPERTS
    idx_bits = (m - 1).bit_length()
    assert (2 * N_EXPERTS + 2) << idx_bits < 2 ** 31
    assign = lax.broadcasted_iota(jnp.int32, (TOP_K, n_tok), 1) * TOP_K + \
        lax.broadcasted_iota(jnp.int32, (TOP_K, n_tok), 0)
    real_keys = (((2 * top_idx) << idx_bits) | assign).reshape(m)
    n_fill = nb * MOE_BLOCK - m
    assert n_fill % N_EXPERTS == 0 and n_fill // N_EXPERTS >= MOE_BLOCK - 1
    fill_i = lax.broadcasted_iota(jnp.int32, (N_EXPERTS, n_fill // N_EXPERTS), 1)
    fill_keys = jnp.where(fill_i < (pad_counts - counts)[:, None],
                          ((2 * experts + 1) << idx_bits)[:, None], (2 * N_EXPERTS + 1) << idx_bits)
    keys = lax.sort(jnp.concatenate([real_keys, fill_keys.reshape(n_fill)]))
    is_real = ((keys >> idx_bits) & 1) == 0
    row_tok = jnp.where(is_real, (keys & ((1 << idx_bits) - 1)) // TOP_K,
                        jnp.arange(nb * MOE_BLOCK, dtype=jnp.int32) % n_tok)

    blk_start = jnp.arange(nb, dtype=jnp.int32) * MOE_BLOCK
    blk_e = jnp.minimum(jnp.sum((pad_end[None, :] <= blk_start[:, None]).astype(jnp.int32), axis=1),
                        N_EXPERTS - 1)
    n_used = (pad_end[-1] // MOE_BLOCK).astype(jnp.int32).reshape(1)
    has_blocks = pad_counts > 0
    order_e = jnp.cumsum(has_blocks.astype(jnp.int32)) - 1
    later = jnp.logical_and(has_blocks[None, :], experts[None, :] > experts[:, None])
    next_e = jnp.min(jnp.where(later, experts[None, :], N_EXPERTS), axis=1)
    next_e = jnp.where(next_e == N_EXPERTS, -1, next_e)
    blk_is = blk_e[:, None] == experts[None, :]

    def per_block(table):
        return jnp.sum(jnp.where(blk_is, table[None, :], 0), axis=1)

    blk_j = (blk_start - per_block(pad_start)) // MOE_BLOCK
    blk_n = jnp.maximum(per_block(pad_counts) // MOE_BLOCK, 1)
    sched = (blk_e, blk_j, blk_n, per_block(order_e), per_block(next_e), n_used)
    return probs, pos, row_tok, sched


def _rows(x, idx):
    return x.at[idx].get(mode="promise_in_bounds")


def _moe(h2, logits_t, layer, w_gu, b_gu, w_down, b_down):
    n_tok = h2.shape[0]
    probs, pos, row_tok, sched = _route(logits_t, n_tok)
    n_b = (row_tok.shape[0] // MOE_BLOCK) // TC_GATHER_SHARE * MOE_BLOCK
    n_a = row_tok.shape[0] - n_b
    xs_a = _rows(h2, row_tok[:n_a])
    xs_b = _gather_rows(h2, row_tok[n_a:])
    yb = _experts(xs_a, xs_b, sched, layer, w_gu, b_gu, w_down, b_down)
    y = _rows(yb, pos.reshape(-1)).reshape(TOP_K, n_tok, -1)
    return y, probs.T


def _combine_kernel(x_ref, y0_ref, y1_ref, y2_ref, y3_ref, p_ref, g2_ref, nf_ref, o_ref,
                    *, tile0, final, n_ctx_tiles, tiles_per_batch, ctx_row):
    row = _mod_row(pl.program_id(0) + tile0, n_ctx_tiles, tiles_per_batch, ctx_row)
    p = p_ref[...]
    y = (p[:, 0:1] * y0_ref[0] + p[:, 1:2] * y1_ref[0]) + (p[:, 2:3] * y2_ref[0] + p[:, 3:4] * y3_ref[0])
    x = x_ref[...] + g2_ref[pl.ds(row, 1), :] * y
    if final:
        x = x * lax.rsqrt(jnp.mean(x * x, axis=-1, keepdims=True) + EPS) * nf_ref[...]
    o_ref[...] = x


def _combine(x, y, probs, mod, norm_f, *, tile0, final, n_ctx_tiles, tiles_per_batch, ctx_row):
    assert y.shape[0] == TOP_K == 4
    _, n_rows, d = y.shape
    tm = ROW_TILE
    kern = functools.partial(_combine_kernel, tile0=tile0, final=final, n_ctx_tiles=n_ctx_tiles,
                             tiles_per_batch=tiles_per_batch, ctx_row=ctx_row)

    def yspec(k):
        return pl.BlockSpec((1, tm, d), lambda i: (k, i, 0))

    return pl.pallas_call(
        kern,
        grid=(n_rows // tm,),
        in_specs=[pl.BlockSpec((tm, d), lambda i: (tile0 + i, 0)),
                  yspec(0), yspec(1), yspec(2), yspec(3),
                  pl.BlockSpec((tm, TOP_K), lambda i: (i, 0)),
                  pl.BlockSpec((8, d), lambda i: (0, 5)),
                  pl.BlockSpec((1, d), lambda i: (0, 0))],
        out_specs=pl.BlockSpec((tm, d), lambda i: (i, 0)),
        out_shape=jax.ShapeDtypeStruct((n_rows, d), F32),
        compiler_params=_cparams("parallel"),
        name="moe_combine",
    )(x, y, y, y, y, probs, mod, norm_f)


def _embed_kernel(ctx_ref, x_ref, pos_ref, o_ref, *, n_ctx_tiles):
    j = pl.program_id(0)
    b = pl.program_id(1)

    @pl.when(jnp.logical_and(j == 0, b < n_ctx_tiles))
    def _():
        o_ref[...] = ctx_ref[...]

    @pl.when(j > 0)
    def _():
        o_ref[...] = x_ref[...] + pos_ref[...]


def _embed(ctx2, x2, pos, *, batch):
    tm = ROW_TILE
    d = x2.shape[1]
    l = pos.shape[0]
    n_ctx_tiles = ctx2.shape[0] // tm
    nt = l // tm
    assert n_ctx_tiles <= batch
    return pl.pallas_call(
        functools.partial(_embed_kernel, n_ctx_tiles=n_ctx_tiles),
        grid=(1 + nt, batch),
        in_specs=[pl.BlockSpec((tm, d), lambda j, b: (jnp.minimum(b, n_ctx_tiles - 1), 0)),
                  pl.BlockSpec((tm, d), lambda j, b: (b * nt + jnp.maximum(j - 1, 0), 0)),
                  pl.BlockSpec((tm, d), lambda j, b: (jnp.maximum(j - 1, 0), 0))],
        out_specs=pl.BlockSpec(
            (tm, d), lambda j, b: (jnp.where(j == 0, jnp.minimum(b, n_ctx_tiles - 1),
                                             n_ctx_tiles + b * nt + j - 1), 0)),
        out_shape=jax.ShapeDtypeStruct((ctx2.shape[0] + x2.shape[0], d), F32),
        compiler_params=_cparams("arbitrary", "arbitrary"),
        name="embed_tokens",
    )(ctx2, x2, pos)


def _grid_pos_embed(n, d):
    rows = n // GRID_W
    row = jnp.broadcast_to(jnp.arange(rows, dtype=F32)[:, None], (rows, GRID_W)).reshape(-1)
    col = jnp.broadcast_to(jnp.arange(GRID_W, dtype=F32)[None, :], (rows, GRID_W)).reshape(-1)
    quarter = d // 4
    omega = 1.0 / (10000.0 ** (jnp.arange(quarter, dtype=F32) / quarter))

    def enc(p):
        a = p[:, None] * omega
        return jnp.concatenate([jnp.sin(a), jnp.cos(a)], axis=-1)

    return jnp.concatenate([enc(row), enc(col)], axis=-1)


def kernel(x, c, ctx, c_ctx, w_mod, b_mod, norm1, w_in, w_a2, b_a, gla_norm, w_s, b_s, p_gla, p_sg, p_ft,
           w_out, norm2, w_router, b_router, w_gu, b_gu, w_down, b_down, norm_f):
    batch, l, d = x.shape
    lc = ctx.shape[1]
    depth = w_mod.shape[0]
    n_ctx_rows = batch * lc
    t_rows = n_ctx_rows + batch * l
    assert lc % ROW_TILE == 0 and l % ROW_TILE == 0 and batch < 8

    tm1 = math.gcd(math.gcd(n_ctx_rows, l), 1024)
    assert tm1 % ROW_TILE == 0
    proj_tiles = dict(n_ctx_tiles=n_ctx_rows // tm1, tiles_per_batch=l // tm1, ctx_row=batch)
    row_tiles = dict(n_ctx_tiles=n_ctx_rows // ROW_TILE, tiles_per_batch=l // ROW_TILE, ctx_row=batch)
    tm2 = ROW_TILE
    tm3 = math.gcd(tm1, 2 * ROW_TILE)
    big_tiles = dict(n_ctx_tiles=n_ctx_rows // tm3, tiles_per_batch=l // tm3, ctx_row=batch)

    o_q = 0
    o_k = o_q + W_Q
    o_v = o_k + W_Q
    o_r = o_v + W_V
    o_a = o_r + W_V
    o_su = o_a + 2 * GLA_RANK
    o_sv = o_su + W_SG
    o_f = o_sv + W_SG
    o_g = o_f + W_FT
    n_gates = 3 * d
    cols = dict(gates=0, q=n_gates, r=n_gates + 2 * W_Q + W_V, su=n_gates + 2 * W_Q + 2 * W_V)
    cols["sv"] = cols["su"] + W_SG
    cols["f"] = cols["sv"] + W_SG
    n_main = cols["f"] + W_FT

    xt = _embed(ctx.reshape(n_ctx_rows, d), x.reshape(batch * l, d), _grid_pos_embed(l, d), batch=batch)

    cond = jnp.zeros((8, d), F32).at[:batch].set(c).at[batch].set(c_ctx)
    mods = _mod_vectors(cond, w_mod, b_mod)

    cc, sc = _dft_tables(FT_CH)
    chan_tab = jnp.concatenate([cc, sc], axis=1)
    cos_l, sin_l = _dft_tables(l)
    cos_c, sin_c = _dft_tables(lc)

    w_main = jnp.concatenate(
        [w_in[:, :, o_g:], w_in[:, :, o_q:o_a], w_in[:, :, o_su:o_g]], axis=2).astype(BF16)
    w_alr = jnp.pad(w_in[:, :, o_a:o_su], ((0, 0), (0, 0), (0, LANES - 2 * GLA_RANK))).astype(BF16)

    for i in range(depth):
        last = i == depth - 1
        w_a = w_alr[i]
        wa2 = jnp.zeros((LANES, 2 * W_Q), F32)
        wa2 = wa2.at[:GLA_RANK, :W_Q].set(w_a2[i, 0]).at[GLA_RANK:2 * GLA_RANK, W_Q:].set(w_a2[i, 1])
        mod = mods[i]

        z, la = _project(xt, mod, norm1[i][None], w_main, w_a, wa2.astype(BF16), b_a[i].reshape(1, 2 * W_Q),
                         layer=i, tm=tm1, tn=1536, **proj_tiles)
        assert z.shape[1] == n_main

        o_fw, o_bw = _gla(z, la, batch=batch, lc=lc, l=l, col_q=cols["q"])

        xc_c, xs_c = _ft_channel(z, chan_tab, col_f=cols["f"], row0=0, n_rows=n_ctx_rows)
        xc_l, xs_l = _ft_channel(z, chan_tab, col_f=cols["f"], row0=n_ctx_rows, n_rows=batch * l)
        y_ft = (_ft_positions(cos_c, sin_c, xc_c, xs_c, batch=batch, seq=lc),
                _ft_positions(cos_l, sin_l, xc_l, xs_l, batch=batch, seq=l))

        bs_b = jnp.broadcast_to(b_s[i][:, :, None], (SG_GROUPS, SG_CHUNK, SG_CH))
        merged = _branches(z, o_fw, o_bw, y_ft, gla_norm[i][None], w_s[i].astype(BF16), bs_b,
                           p_gla[i].astype(BF16), p_sg[i].astype(BF16), p_ft[i].astype(BF16), cols=cols, tm=tm2)

        w_r = jnp.pad(w_router[i], ((0, 0), (0, LANES - N_EXPERTS)))
        b_r = jnp.pad(b_router[i], (0, LANES - N_EXPERTS)).reshape(1, LANES)
        xt, h2, logits_t = _out_project(merged, xt, w_out[i].astype(BF16), mod, norm2[i][None], w_r, b_r,
                                      tm=tm3, **big_tiles)

        row0 = n_ctx_rows if last else 0
        y, probs = _moe(h2[row0:], logits_t[:N_EXPERTS, row0:], i, w_gu, b_gu, w_down, b_down)
        xt = _combine(xt, y, probs, mod, norm_f[None], tile0=row0 // ROW_TILE, final=last, **row_tiles)

    return xt.reshape(batch, l, d)
```

```python
import functools
import math

import jax
import jax.numpy as jnp
from jax import lax
from jax.experimental import pallas as pl
from jax.experimental.pallas import tpu as pltpu

F32 = jnp.float32
BF16 = jnp.bfloat16

EPS = 1e-6
GRID_W = 64
GLA_HEADS = 4
GLA_DK = 128
GLA_DV = 256
GLA_RANK = 16
GLA_TAU = 16.0
GLA_CHUNK = 64
GLA_SUB = 16
GLA_STAGES = 3
SG_GROUPS = 4
SG_CH = 128
SG_CHUNK = 128
FT_GROUPS = 4
FT_CH = 128
N_EXPERTS = 32
TOP_K = 4
SWIGLU_LIMIT = 7.0
SWIGLU_ALPHA = 1.702
MOE_BLOCK = 256
W_CHUNK = 512

LOG2E = math.log2(math.e)
LANES = 128
ROW_TILE = 256
VMEM_LIMIT = 56 * 1024 * 1024

W_Q = GLA_HEADS * GLA_DK
W_V = GLA_HEADS * GLA_DV
W_SG = SG_GROUPS * SG_CH
W_FT = FT_GROUPS * FT_CH


def _cparams(*sem):
    return pltpu.CompilerParams(dimension_semantics=sem, vmem_limit_bytes=VMEM_LIMIT)


def _sigmoid(x):
    return 1.0 / (1.0 + jnp.exp(-x))


def _gelu_tanh(x):
    return 0.5 * x * (1.0 + jnp.tanh(math.sqrt(2.0 / math.pi) * (x + 0.044715 * (x * x * x))))


def _dot(a, b):
    return jnp.dot(a, b, preferred_element_type=F32)


def _pack_bf16_pairs(x):
    bits = lax.bitcast_convert_type(x.astype(F32), jnp.uint32)
    half = x.shape[1] // 2
    return (bits[:, :half] >> 16) | (bits[:, half:] & jnp.uint32(0xFFFF0000))


def _unpack_bf16_pairs(u):
    lo = lax.bitcast_convert_type(u << 16, F32)
    hi = lax.bitcast_convert_type(u & jnp.uint32(0xFFFF0000), F32)
    return jnp.concatenate([lo, hi], axis=1).astype(BF16)


def _mod_row(tile, n_ctx_tiles, tiles_per_batch, ctx_row):
    return jnp.where(tile < n_ctx_tiles, ctx_row, (tile - n_ctx_tiles) // tiles_per_batch)


def _mod_kernel(c_ref, w_ref, b_ref, o_ref):
    c = c_ref[...]
    s = (c * _sigmoid(c)).astype(BF16)
    o_ref[0] = _dot(s, w_ref[0].astype(BF16)) + b_ref[0]


def _mod_vectors(cond, w_mod, b_mod):
    depth, d, n = w_mod.shape
    tn = 1024
    return pl.pallas_call(
        _mod_kernel,
        grid=(depth, n // tn),
        in_specs=[
            pl.BlockSpec((8, d), lambda l, j: (0, 0)),
            pl.BlockSpec((1, d, tn), lambda l, j: (l, 0, j)),
            pl.BlockSpec((1, 1, tn), lambda l, j: (l, 0, j)),
        ],
        out_specs=pl.BlockSpec((1, 8, tn), lambda l, j: (l, 0, j)),
        out_shape=jax.ShapeDtypeStruct((depth, 8, n), F32),
        compiler_params=_cparams("parallel", "parallel"),
        name="mod_vectors",
    )(cond, w_mod, b_mod.reshape(depth, 1, n))


def _proj_kernel(x_ref, sh_ref, sc_ref, g_ref, w_ref, wa_ref, wa2_ref, ba_ref,
                 z_ref, la_ref, a_scr, *, n_ctx_tiles, tiles_per_batch, ctx_row):
    i = pl.program_id(0)
    j = pl.program_id(1)

    @pl.when(j == 0)
    def _():
        row = _mod_row(i, n_ctx_tiles, tiles_per_batch, ctx_row)
        scale = 1.0 + sc_ref[pl.ds(row, 1), :]
        shift = sh_ref[pl.ds(row, 1), :]

        def body(r, carry):
            rows = pl.ds(pl.multiple_of(r * ROW_TILE, ROW_TILE), ROW_TILE)
            x = x_ref[rows, :]
            y = x * lax.rsqrt(jnp.mean(x * x, axis=-1, keepdims=True) + EPS) * g_ref[...]
            hb = (y * scale + shift).astype(BF16)
            a_scr[rows, :] = hb
            a_lr = _dot(hb, wa_ref[...])
            pre = _dot(a_lr.astype(BF16), wa2_ref[...]) + ba_ref[...]
            la_ref[rows, :] = -(jnp.maximum(-pre, 0.0) + jnp.log1p(jnp.exp(-jnp.abs(pre)))) * (1.0 / GLA_TAU)
            return carry

        lax.fori_loop(0, x_ref.shape[0] // ROW_TILE, body, 0)

    z_ref[...] = _dot(a_scr[...], w_ref[0]).astype(BF16)


def _project(x, mod, g, w_main, w_a, w_a2, b_a, *, layer, tm, tn, n_ctx_tiles, tiles_per_batch, ctx_row):
    t, d = x.shape
    n = w_main.shape[2]
    kern = functools.partial(_proj_kernel, n_ctx_tiles=n_ctx_tiles,
                             tiles_per_batch=tiles_per_batch, ctx_row=ctx_row)
    return pl.pallas_call(
        kern,
        grid=(t // tm, n // tn),
        in_specs=[
            pl.BlockSpec((tm, d), lambda i, j: (i, 0)),
            pl.BlockSpec((8, d), lambda i, j: (0, 0)),
            pl.BlockSpec((8, d), lambda i, j: (0, 1)),
            pl.BlockSpec((1, d), lambda i, j: (0, 0)),
            pl.BlockSpec((1, d, tn), lambda i, j: (layer, 0, j)),
            pl.BlockSpec((d, LANES), lambda i, j: (0, 0)),
            pl.BlockSpec((LANES, 2 * W_Q), lambda i, j: (0, 0)),
            pl.BlockSpec((1, 2 * W_Q), lambda i, j: (0, 0)),
        ],
        out_specs=[
            pl.BlockSpec((tm, tn), lambda i, j: (i, j)),
            pl.BlockSpec((tm, 2 * W_Q), lambda i, j: (i, 0)),
        ],
        out_shape=[
            jax.ShapeDtypeStruct((t, n), BF16),
            jax.ShapeDtypeStruct((t, 2 * W_Q), F32),
        ],
        scratch_shapes=[pltpu.VMEM((tm, d), BF16)],
        compiler_params=_cparams("parallel", "arbitrary"),
        name="norm_mod_project",
    )(x, mod, mod, g, w_main, w_a, w_a2, b_a)


def _gla_pairs(rev):
    c, sub = GLA_CHUNK, GLA_SUB
    half = c // 2
    if rev:
        return ((0, half, half, c, half - 1),
                (0, sub, sub, half, sub - 1),
                (half, half + sub, half + sub, c, half + sub - 1))
    return ((half, c, 0, half, half),
            (sub, half, 0, sub, sub),
            (half + sub, c, half, half + sub, half + sub))


def _place_rows(x, r0, n):
    parts = []
    if r0:
        parts.append(jnp.zeros((r0, x.shape[1]), x.dtype))
    parts.append(x)
    rest = n - r0 - x.shape[0]
    if rest:
        parts.append(jnp.zeros((rest, x.shape[1]), x.dtype))
    return jnp.concatenate(parts, axis=0)


def _gla_chunk(q_ref, k_ref, v_ref, la_ref, e_ref, o_ref, s_ref, cum_ref, kc_ref, r0, h, rev):
    c, sub = GLA_CHUNK, GLA_SUB
    nsub = c // sub
    rows = pl.ds(r0, c)
    kcols = slice(h * GLA_DK, (h + 1) * GLA_DK)
    vcols = slice(h * GLA_DV, (h + 1) * GLA_DV)
    q = q_ref[rows, kcols].astype(F32) * (GLA_DK ** -0.5)
    k = k_ref[rows, kcols].astype(F32)
    v = v_ref[rows, vcols]
    la = la_ref[rows, kcols]

    ri = lax.broadcasted_iota(jnp.int32, (c, c), 0)
    ci = lax.broadcasted_iota(jnp.int32, (c, c), 1)
    before = (ci >= ri) if rev else (ci <= ri)
    tri = jnp.where(before, 1.0, 0.0).astype(BF16)
    la_hi = la.astype(BF16)
    rem = la - la_hi.astype(F32)
    la_mid = rem.astype(BF16)
    la_lo = (rem - la_mid.astype(F32)).astype(BF16)
    cum = (_dot(tri, la_hi) + _dot(tri, la_mid) + _dot(tri, la_lo)) * LOG2E
    cum_ref[...] = cum
    kc_ref[...] = k
    yield
    last = 0 if rev else c - 1
    total = cum[last:last + 1, :]

    s = s_ref[...]
    o = _dot((q * jnp.exp2(cum)).astype(BF16), s.astype(BF16))

    qparts, kparts = [], []
    for l0, l1, e0, e1, ref in _gla_pairs(rev):
        cr = cum[ref:ref + 1, :]
        qparts.append(_place_rows(q[l0:l1] * jnp.exp2(cum[l0:l1] - cr), l0, c))
        kparts.append(_place_rows(k[e0:e1] * jnp.exp2(cr - cum[e0:e1]), e0, c))
    s_off = lax.dot_general(jnp.concatenate(qparts, axis=1).astype(BF16),
                            jnp.concatenate(kparts, axis=1).astype(BF16),
                            (((1,), (1,)), ((), ())), preferred_element_type=F32)

    hs = sub // 2

    def halves(x, which):
        return jnp.concatenate([x[a * sub + w * hs:a * sub + (w + 1) * hs] for a in range(nsub) for w in which],
                               axis=0)

    parts = []
    for jj in range(sub):
        if rev:
            which = (0, 1) if jj >= hs else (0,)
        else:
            which = (0, 1) if jj < hs else (1,)
        nrow = hs * len(which)
        cj = jnp.concatenate(
            [jnp.broadcast_to(cum_ref[a * sub + jj:a * sub + jj + 1, :], (nrow, GLA_DK)) for a in range(nsub)],
            axis=0)
        kj = jnp.concatenate(
            [jnp.broadcast_to(kc_ref[a * sub + jj:a * sub + jj + 1, :], (nrow, GLA_DK)) for a in range(nsub)],
            axis=0)
        prod = halves(q, which) * jnp.exp2(jnp.minimum(halves(cum, which) - cj, 0.0)) * kj
        if len(which) == 1:
            zero = jnp.zeros((hs, GLA_DK), F32)
            pieces = []
            for a in range(nsub):
                live = prod[a * hs:(a + 1) * hs]
                pieces += [live, zero] if which[0] == 0 else [zero, live]
            prod = jnp.concatenate(pieces, axis=0)
        parts.append(prod.astype(BF16))
    diag = _dot(jnp.concatenate(parts, axis=1), e_ref[...])
    yield

    lag = (ci - ri) if rev else (ri - ci)
    room = (sub - 1 - ri % sub) if rev else (ri % sub)
    in_diag = lag.astype(jnp.uint32) <= room.astype(jnp.uint32)
    p = jnp.where(in_diag, diag, 0.0) + s_off
    o_ref[rows, vcols] = o + _dot(p.astype(BF16), v)

    k_out = (k * jnp.exp2(total - cum)).astype(BF16)
    upd = lax.dot_general(k_out, v, (((0,), (0,)), ((), ())), preferred_element_type=F32)
    dmat = jnp.broadcast_to(jnp.exp2(total), (GLA_DK, GLA_DK)).T
    for half in range(GLA_DV // GLA_DK):
        cols = slice(half * GLA_DK, (half + 1) * GLA_DK)
        s_ref[:, cols] = s[:, cols] * dmat + upd[:, cols]


def _gla_kernel(qf, kf, vf, laf, qb, kb, vb, lab, e_ref, of_ref, ob_ref, *scratch):
    n = 2 * GLA_HEADS
    states, cums, keys = scratch[:n], scratch[n:2 * n], scratch[2 * n:]

    @pl.when(pl.program_id(1) == 0)
    def _():
        for s_ref in states:
            s_ref[...] = jnp.zeros_like(s_ref)

    nchunk = ROW_TILE // GLA_CHUNK

    def body(c, carry):
        r_f = pl.multiple_of(c * GLA_CHUNK, GLA_CHUNK)
        r_b = pl.multiple_of((nchunk - 1 - c) * GLA_CHUNK, GLA_CHUNK)
        units = []
        for h in range(GLA_HEADS):
            f, b = h, GLA_HEADS + h
            units.append(_gla_chunk(qf, kf, vf, laf, e_ref, of_ref, states[f], cums[f], keys[f], r_f, h, False))
            units.append(_gla_chunk(qb, kb, vb, lab, e_ref, ob_ref, states[b], cums[b], keys[b], r_b, h, True))
        for _ in range(GLA_STAGES):
            for u in units:
                next(u, None)
        return carry

    lax.fori_loop(0, nchunk, body, 0)


def _gla(z, la, *, batch, lc, l, col_q):
    t_rows = z.shape[0]
    nctx = lc // ROW_TILE
    nlat = l // ROW_TILE
    steps = nctx + nlat
    qb0 = col_q // W_Q
    kb0 = (col_q + W_Q) // W_Q
    vb0 = (col_q + 2 * W_Q) // W_V

    def fwd_tile(b, t):
        return jnp.where(t < nctx, b * nctx + t, batch * nctx + b * nlat + (t - nctx))

    def bwd_tile(b, t):
        return jnp.where(t < nctx, b * nctx + (nctx - 1 - t), batch * nctx + b * nlat + (steps - 1 - t))

    def spec(width, tile_fn, col):
        return pl.BlockSpec((ROW_TILE, width), lambda b, t: (tile_fn(b, t), col))

    kk = jnp.arange(GLA_SUB * GLA_DK, dtype=jnp.int32)[:, None] // GLA_DK
    cc = jnp.arange(GLA_CHUNK, dtype=jnp.int32)[None, :] % GLA_SUB
    e_mat = (kk == cc).astype(BF16)

    o_shape = jax.ShapeDtypeStruct((t_rows, W_V), F32)
    n_units = 2 * GLA_HEADS
    state = pltpu.VMEM((GLA_DK, GLA_DV), F32)
    chunk = pltpu.VMEM((GLA_CHUNK, GLA_DK), F32)
    return pl.pallas_call(
        _gla_kernel,
        grid=(batch, steps),
        in_specs=[
            spec(W_Q, fwd_tile, qb0), spec(W_Q, fwd_tile, kb0), spec(W_V, fwd_tile, vb0), spec(W_Q, fwd_tile, 0),
            spec(W_Q, bwd_tile, qb0), spec(W_Q, bwd_tile, kb0), spec(W_V, bwd_tile, vb0), spec(W_Q, bwd_tile, 1),
            pl.BlockSpec((GLA_SUB * GLA_DK, GLA_CHUNK), lambda b, t: (0, 0)),
        ],
        out_specs=[spec(W_V, fwd_tile, 0), spec(W_V, bwd_tile, 0)],
        out_shape=[o_shape, o_shape],
        scratch_shapes=[state] * n_units + [chunk] * (2 * n_units),
        compiler_params=_cparams("parallel", "arbitrary"),
        name="gla_scan",
    )(z, z, z, la, z, z, z, la, e_mat)


def _dft_tables(n):
    scale = n ** -0.5
    p = jnp.arange(n, dtype=jnp.int32)[:, None]

    def trig(cols, period):
        ang = ((p * cols[None, :]) % period).astype(F32) * (2.0 * math.pi / period)
        return jnp.cos(ang), jnp.sin(ang)

    n2 = 1
    while n2 * n2 < n:
        n2 *= 2
    if n % n2 or n2 == 1:
        c, s = trig(jnp.arange(n, dtype=jnp.int32), n)
        return (c * scale).astype(BF16), (s * scale).astype(BF16)
    n1 = n // n2
    ca, sa = trig(jnp.arange(n1, dtype=jnp.int32), n1)
    cb, sb = trig(jnp.arange(n2, dtype=jnp.int32), n)
    ca, sa, cb, sb = ca[:, :, None], sa[:, :, None], cb[:, None, :] * scale, sb[:, None, :] * scale
    return ((ca * cb - sa * sb).reshape(n, n).astype(BF16), (sa * cb + ca * sb).reshape(n, n).astype(BF16))


def _ft_chan_kernel(f_ref, cs_ref, xc_ref, xs_ref):
    for g in range(FT_GROUPS):
        cols = slice(g * FT_CH, (g + 1) * FT_CH)
        y = _dot(f_ref[:, cols], cs_ref[...])
        xc_ref[:, cols] = y[:, :FT_CH].astype(BF16)
        xs_ref[:, cols] = y[:, FT_CH:].astype(BF16)


def _ft_channel(z, cs, *, col_f, row0, n_rows):
    tm = math.gcd(math.gcd(row0, n_rows), 1024)
    shape = jax.ShapeDtypeStruct((n_rows, W_FT), BF16)
    return pl.pallas_call(
        _ft_chan_kernel,
        grid=(n_rows // tm,),
        in_specs=[pl.BlockSpec((tm, W_FT), lambda i: (row0 // tm + i, col_f // W_FT)),
                  pl.BlockSpec((FT_CH, 2 * FT_CH), lambda i: (0, 0))],
        out_specs=[pl.BlockSpec((tm, W_FT), lambda i: (i, 0))] * 2,
        out_shape=[shape, shape],
        compiler_params=_cparams("parallel"),
        name="fourier_channels",
    )(z, cs)


def _ft_pos_kernel(c_ref, s_ref, xc_ref, xs_ref, o_ref):
    o_ref[...] = _dot(c_ref[...], xc_ref[...]) - _dot(s_ref[...], xs_ref[...])


def _ft_positions(cos_t, sin_t, xc, xs, *, batch, seq):
    nt = seq // ROW_TILE
    x_spec = pl.BlockSpec((seq, W_FT), lambda b, j: (b, 0))
    t_spec = pl.BlockSpec((ROW_TILE, seq), lambda b, j: (j, 0))
    return pl.pallas_call(
        _ft_pos_kernel,
        grid=(batch, nt),
        in_specs=[t_spec, t_spec, x_spec, x_spec],
        out_specs=pl.BlockSpec((ROW_TILE, W_FT), lambda b, j: (b * nt + j, 0)),
        out_shape=jax.ShapeDtypeStruct((batch * seq, W_FT), F32),
        compiler_params=_cparams("parallel", "parallel"),
        name="fourier_positions",
    )(cos_t, sin_t, xc, xs)


FT_RADIX = 16
FT_TILE = 16


def _ft_stage1_kernel(xc_ref, xs_ref, k_ref, tc_ref, ts_ref, o_ref):
    rows = FT_RADIX * FT_TILE
    zin = jnp.concatenate([xc_ref[0].reshape(rows, W_FT), xs_ref[0].reshape(rows, W_FT)], axis=0)
    a = _dot(k_ref[...], zin)
    re, im = a[:rows], a[rows:]
    tc, ts = tc_ref[0], ts_ref[0]
    o_ref[0, 0] = (re * tc + im * ts).astype(BF16).reshape(FT_RADIX, FT_TILE, W_FT)
    o_ref[0, 1] = (im * tc - re * ts).astype(BF16).reshape(FT_RADIX, FT_TILE, W_FT)


def _ft_stage2_kernel(b_ref, t_ref, o_ref):
    rhs = jnp.concatenate([b_ref[0, 0, 0], b_ref[0, 1, 0]], axis=0)
    o_ref[0, 0] = _dot(t_ref[...], rhs)


def _ft_long_tables(seq):
    r, n1 = FT_RADIX, seq // FT_RADIX

    def trig(rows, cols, period):
        ang = ((rows[:, None] * cols[None, :]) % period).astype(F32) * (2.0 * math.pi / period)
        return jnp.cos(ang), jnp.sin(ang)

    ir = jnp.arange(r, dtype=jnp.int32)
    i1 = jnp.arange(n1, dtype=jnp.int32)
    c1, s1 = trig(ir, ir, r)
    t1 = jnp.concatenate([jnp.concatenate([c1, -s1], axis=1), jnp.concatenate([-s1, -c1], axis=1)], axis=0)
    k1 = jnp.kron(t1 * r ** -0.5, jnp.eye(FT_TILE, dtype=F32)).astype(BF16)
    tc, ts = trig(ir, i1, seq)

    def per_tile(t):
        return t.reshape(r, n1 // FT_TILE, FT_TILE).transpose(1, 0, 2).reshape(n1 // FT_TILE, r * FT_TILE, 1)

    c2, s2 = trig(i1, i1, n1)
    t2 = (jnp.concatenate([c2, s2], axis=1) * n1 ** -0.5).astype(BF16)
    return k1, per_tile(tc), per_tile(ts), t2


def _ft_positions_long(tables, xc, xs, *, batch, seq):
    k1, tc, ts, t2 = tables
    r, n1 = FT_RADIX, seq // FT_RADIX
    nt = n1 // FT_TILE
    rows = r * FT_TILE
    x4 = (batch, r, n1, W_FT)
    x_spec = pl.BlockSpec((1, r, FT_TILE, W_FT), lambda b, t: (b, 0, t, 0))
    tw_spec = pl.BlockSpec((1, rows, 1), lambda b, t: (t, 0, 0))
    stage1 = pl.pallas_call(
        _ft_stage1_kernel,
        grid=(batch, nt),
        in_specs=[x_spec, x_spec, pl.BlockSpec((2 * rows, 2 * rows), lambda b, t: (0, 0)), tw_spec, tw_spec],
        out_specs=pl.BlockSpec((1, 2, r, FT_TILE, W_FT), lambda b, t: (b, 0, 0, t, 0)),
        out_shape=jax.ShapeDtypeStruct((batch, 2, r, n1, W_FT), BF16),
        compiler_params=_cparams("parallel", "parallel"),
        name="fourier_stage1",
    )(xc.reshape(x4), xs.reshape(x4), k1, tc, ts)
    stage2 = pl.pallas_call(
        _ft_stage2_kernel,
        grid=(batch, r),
        in_specs=[pl.BlockSpec((1, 2, 1, n1, W_FT), lambda b, p: (b, 0, p, 0, 0)),
                  pl.BlockSpec((n1, 2 * n1), lambda b, p: (0, 0))],
        out_specs=pl.BlockSpec((1, 1, n1, W_FT), lambda b, p: (b, p, 0, 0)),
        out_shape=jax.ShapeDtypeStruct((batch, r, n1, W_FT), F32),
        compiler_params=_cparams("parallel", "parallel"),
        name="fourier_stage2",
    )(stage1, t2)
    return stage2.transpose(0, 2, 1, 3).reshape(batch * seq, W_FT)


def _branch_kernel(gg_ref, gs_ref, gf_ref, of_ref, ob_ref, r_ref, su_ref, sv_ref, yc_ref, yl_ref,
                   gn_ref, ws_ref, bs_ref, pg_ref, ps_ref, pf_ref, m_ref, *, n_ctx_tiles):
    o = of_ref[...] + ob_ref[...]
    heads = []
    for h in range(GLA_HEADS):
        oh = o[:, h * GLA_DV:(h + 1) * GLA_DV]
        heads.append(oh * lax.rsqrt(jnp.mean(oh * oh, axis=-1, keepdims=True) + EPS))
    r = r_ref[...].astype(F32)
    y_gla = (jnp.concatenate(heads, axis=-1) * gn_ref[...]) * (r * _sigmoid(r))
    acc = _sigmoid(gg_ref[...].astype(F32)) * _dot(y_gla.astype(BF16), pg_ref[...])

    u = _gelu_tanh(su_ref[...].astype(F32))
    v = _gelu_tanh(sv_ref[...].astype(F32))
    groups = []
    for g in range(SG_GROUPS):
        vg = v[:, g * SG_CH:(g + 1) * SG_CH]
        mu = jnp.mean(vg, axis=-1, keepdims=True)
        var = jnp.mean(jnp.square(vg - mu), axis=-1, keepdims=True)
        vn = ((vg - mu) * lax.rsqrt(var + EPS)).astype(BF16)
        chunks = []
        for c in range(v.shape[0] // SG_CHUNK):
            chunks.append(_dot(ws_ref[g], vn[c * SG_CHUNK:(c + 1) * SG_CHUNK, :]) + bs_ref[g])
        groups.append(jnp.concatenate(chunks, axis=0))
    y_sg = u * jnp.concatenate(groups, axis=-1)
    acc = acc + _sigmoid(gs_ref[...].astype(F32)) * _dot(y_sg.astype(BF16), ps_ref[...])

    y_ft = jnp.where(pl.program_id(0) < n_ctx_tiles, yc_ref[...], yl_ref[...])
    acc = acc + _sigmoid(gf_ref[...].astype(F32)) * _dot(y_ft.astype(BF16), pf_ref[...])
    m_ref[...] = acc.astype(BF16)


def _branches(z, o_f, o_b, y_ft, gla_norm, w_s, b_s, p_gla, p_sg, p_ft, *, cols, tm):
    t_rows = z.shape[0]
    d = p_gla.shape[1]
    n_ctx_tiles = y_ft[0].shape[0] // tm
    kern = functools.partial(_branch_kernel, n_ctx_tiles=n_ctx_tiles)

    def zspec(width, col0):
        return pl.BlockSpec((tm, width), lambda i: (i, col0 // width))

    def rows(width):
        return pl.BlockSpec((tm, width), lambda i: (i, 0))

    def const(shape):
        return pl.BlockSpec(shape, lambda i: (0,) * len(shape))

    return pl.pallas_call(
        kern,
        grid=(t_rows // tm,),
        in_specs=[
            zspec(d, cols["gates"]), zspec(d, cols["gates"] + d), zspec(d, cols["gates"] + 2 * d),
            rows(W_V), rows(W_V), zspec(W_V, cols["r"]), zspec(W_SG, cols["su"]), zspec(W_SG, cols["sv"]),
            pl.BlockSpec((tm, W_FT), lambda i: (jnp.minimum(i, n_ctx_tiles - 1), 0)),
            pl.BlockSpec((tm, W_FT), lambda i: (jnp.maximum(i - n_ctx_tiles, 0), 0)),
            const((1, W_V)), const((SG_GROUPS, SG_CHUNK, SG_CHUNK)), const((SG_GROUPS, SG_CHUNK, SG_CH)),
            const((W_V, d)), const((W_SG, d)), const((W_FT, d)),
        ],
        out_specs=rows(d),
        out_shape=jax.ShapeDtypeStruct((t_rows, d), BF16),
        compiler_params=_cparams("parallel"),
        name="branch_merge",
    )(z, z, z, o_f, o_b, z, z, z, y_ft[0], y_ft[1], gla_norm, w_s, b_s, p_gla, p_sg, p_ft)


def _outproj_kernel(m_ref, x_ref, w_ref, g1_ref, n2_ref, sh_ref, sc_ref, wr_ref, br_ref,
                    xo_ref, h_ref, lg_ref, *, n_ctx_tiles, tiles_per_batch, ctx_row):
    row = _mod_row(pl.program_id(0), n_ctx_tiles, tiles_per_batch, ctx_row)
    gate1 = g1_ref[pl.ds(row, 1), :]
    scale2 = 1.0 + sc_ref[pl.ds(row, 1), :]
    shift2 = sh_ref[pl.ds(row, 1), :]
    w = wr_ref[...]
    w_hi = w.astype(BF16)
    w_lo = (w - w_hi.astype(F32)).astype(BF16)
    for r0 in range(0, x_ref.shape[0], ROW_TILE):
        rows = slice(r0, r0 + ROW_TILE)
        x = x_ref[rows, :] + gate1 * _dot(m_ref[rows, :], w_ref[...])
        xo_ref[rows, :] = x
        y = x * lax.rsqrt(jnp.mean(x * x, axis=-1, keepdims=True) + EPS) * n2_ref[...]
        h = y * scale2 + shift2
        h_hi = h.astype(BF16)
        h_ref[rows, :] = _pack_bf16_pairs(h_hi)
        h_lo = (h - h_hi.astype(F32)).astype(BF16)
        lg = _dot(h_hi, w_hi) + (_dot(h_hi, w_lo) + _dot(h_lo, w_hi)) + br_ref[...]
        lg_ref[:, rows] = lg.T


def _out_project(merged, x, w_out, mod, norm2, w_router, b_router, *, tm, n_ctx_tiles, tiles_per_batch, ctx_row):
    t_rows, d = x.shape
    kern = functools.partial(_outproj_kernel, n_ctx_tiles=n_ctx_tiles,
                             tiles_per_batch=tiles_per_batch, ctx_row=ctx_row)

    def rows(width):
        return pl.BlockSpec((tm, width), lambda i: (i, 0))

    def modspec(k):
        return pl.BlockSpec((8, d), lambda i: (0, k))

    return pl.pallas_call(
        kern,
        grid=(t_rows // tm,),
        in_specs=[rows(d), rows(d), pl.BlockSpec((d, d), lambda i: (0, 0)),
                  modspec(2), pl.BlockSpec((1, d), lambda i: (0, 0)), modspec(3), modspec(4),
                  pl.BlockSpec((d, LANES), lambda i: (0, 0)), pl.BlockSpec((1, LANES), lambda i: (0, 0))],
        out_specs=[rows(d), rows(d // 2), pl.BlockSpec((LANES, tm), lambda i: (0, i))],
        out_shape=[jax.ShapeDtypeStruct((t_rows, d), F32), jax.ShapeDtypeStruct((t_rows, d // 2), jnp.uint32),
                   jax.ShapeDtypeStruct((LANES, t_rows), F32)],
        compiler_params=_cparams("parallel"),
        name="out_project",
    )(merged, x, w_out, mod, norm2, mod, mod, w_router, b_router)


def _expert_kernel(be_ref, jb_ref, nblk_ref, ord_ref, nxt_ref, nu_ref,
                   x_ref, wgu_hbm, bgu_ref, wd_hbm, bd_ref, y_ref,
                   wgu_bf, wd_bf, stage, sems, *, layer):
    i = pl.program_id(0)
    n_gu_chunks = wgu_bf.shape[1] // W_CHUNK
    n_chunks = n_gu_chunks + wd_bf.shape[1] // W_CHUNK

    def chunk_copy(e, q):
        if q < n_gu_chunks:
            src = wgu_hbm.at[layer, e, pl.ds(q * W_CHUNK, W_CHUNK), :]
        else:
            src = wd_hbm.at[layer, e, pl.ds((q - n_gu_chunks) * W_CHUNK, W_CHUNK), :]
        return pltpu.make_async_copy(src, stage.at[q % 2], sems.at[q % 2])

    def land(e, q, slot):
        chunk_copy(e, q).wait()
        w = stage[q % 2].astype(BF16)
        if q < n_gu_chunks:
            wgu_bf[slot, pl.ds(q * W_CHUNK, W_CHUNK), :] = w
        else:
            wd_bf[slot, pl.ds((q - n_gu_chunks) * W_CHUNK, W_CHUNK), :] = w
        if q + 2 < n_chunks:
            chunk_copy(e, q + 2).start()

    active = i < nu_ref[0]
    e = be_ref[i]
    j = jb_ref[i]
    n = nblk_ref[i]
    slot = ord_ref[i] % 2
    nxt = nxt_ref[i]

    @pl.when(i == 0)
    def _():
        chunk_copy(e, 0).start()
        chunk_copy(e, 1).start()
        for q in range(n_chunks):
            land(e, q, slot)

    prefetch = jnp.logical_and(active, nxt >= 0)

    @pl.when(jnp.logical_and(prefetch, j == 0))
    def _():
        chunk_copy(nxt, 0).start()
        chunk_copy(nxt, 1).start()

    @pl.when(active)
    def _():
        gu = _dot(_unpack_bf16_pairs(x_ref[...]), wgu_bf[slot]) + bgu_ref[0, 0]
        de = gu.shape[1] // 2
        gate = jnp.minimum(gu[:, :de], SWIGLU_LIMIT)
        up = jnp.clip(gu[:, de:], -SWIGLU_LIMIT, SWIGLU_LIMIT)
        hdn = (up + 1.0) * (gate * _sigmoid(SWIGLU_ALPHA * gate))
        y_ref[...] = _dot(hdn.astype(BF16), wd_bf[slot]) + bd_ref[0, 0]

    @pl.when(jnp.logical_not(active))
    def _():
        y_ref[...] = jnp.zeros_like(y_ref)

    lo = (n_chunks * j) // n
    hi = (n_chunks * (j + 1)) // n
    for q in range(n_chunks):
        @pl.when(jnp.logical_and(prefetch, jnp.logical_and(lo <= q, q < hi)))
        def _(q=q):
            land(nxt, q, 1 - slot)


def _experts(xs, sched, layer, w_gu, b_gu, w_down, b_down):
    rows_total = xs.shape[0]
    nb = rows_total // MOE_BLOCK
    depth, ne, d, n_gu = w_gu.shape
    assert xs.shape[1] * 2 == d and xs.dtype == jnp.uint32
    de = w_down.shape[2]
    assert n_gu == d and d % W_CHUNK == 0 and de % W_CHUNK == 0

    def blk(i, *s):
        return jnp.minimum(i, s[-1][0] - 1)

    def bspec(n):
        return pl.BlockSpec((1, 1, 1, n), lambda i, *s: (layer, s[0][blk(i, *s)], 0, 0))

    grid_spec = pltpu.PrefetchScalarGridSpec(
        num_scalar_prefetch=len(sched),
        grid=(nb,),
        in_specs=[
            pl.BlockSpec((MOE_BLOCK, d // 2), lambda i, *s: (blk(i, *s), 0)),
            pl.BlockSpec(memory_space=pl.ANY), bspec(n_gu),
            pl.BlockSpec(memory_space=pl.ANY), bspec(d),
        ],
        out_specs=pl.BlockSpec((MOE_BLOCK, d), lambda i, *s: (i, 0)),
        scratch_shapes=[pltpu.VMEM((2, d, n_gu), BF16), pltpu.VMEM((2, de, d), BF16),
                        pltpu.VMEM((2, W_CHUNK, d), F32), pltpu.SemaphoreType.DMA((2,))],
    )
    return pl.pallas_call(
        functools.partial(_expert_kernel, layer=layer),
        grid_spec=grid_spec,
        out_shape=jax.ShapeDtypeStruct((rows_total, d), F32),
        compiler_params=_cparams("arbitrary"),
        name="moe_experts",
    )(*sched, xs, w_gu, b_gu.reshape(depth, ne, 1, n_gu), w_down, b_down.reshape(depth, ne, 1, d))


def _route(logits_t, n_tok):
    experts = jnp.arange(N_EXPERTS, dtype=jnp.int32)
    lg = logits_t
    e_iota = lax.broadcasted_iota(jnp.int32, lg.shape, 0)
    vals, idxs = [], []
    for _ in range(TOP_K):
        mx = jnp.max(lg, axis=0)
        ix = jnp.min(jnp.where(lg == mx[None, :], e_iota, N_EXPERTS), axis=0)
        vals.append(mx)
        idxs.append(ix)
        lg = jnp.where(e_iota == ix[None, :], -jnp.inf, lg)
    top_val = jnp.stack(vals)
    top_idx = jnp.stack(idxs)
    probs = jax.nn.softmax(top_val, axis=0)

    m = n_tok * TOP_K
    onehot = (top_idx[:, :, None] == experts[None, None, :]).astype(jnp.int32)
    per_tok = jnp.sum(onehot, axis=0)
    csum = jnp.cumsum(per_tok, axis=0)
    counts = csum[-1]
    pad_counts = (counts + MOE_BLOCK - 1) // MOE_BLOCK * MOE_BLOCK
    pad_end = jnp.cumsum(pad_counts)
    pad_start = pad_end - pad_counts
    base = csum - per_tok + pad_start[None, :]
    pos = jnp.sum(base[None, :, :] * onehot, axis=2)

    nb = (m + MOE_BLOCK - 1) // MOE_BLOCK + N_EXPERTS
    idx_bits = (m - 1).bit_length()
    assert (2 * N_EXPERTS + 2) << idx_bits < 2 ** 31
    assign = lax.broadcasted_iota(jnp.int32, (TOP_K, n_tok), 1) * TOP_K + \
        lax.broadcasted_iota(jnp.int32, (TOP_K, n_tok), 0)
    real_keys = (((2 * top_idx) << idx_bits) | assign).reshape(m)
    n_fill = nb * MOE_BLOCK - m
    assert n_fill % N_EXPERTS == 0 and n_fill // N_EXPERTS >= MOE_BLOCK - 1
    fill_i = lax.broadcasted_iota(jnp.int32, (N_EXPERTS, n_fill // N_EXPERTS), 1)
    fill_keys = jnp.where(fill_i < (pad_counts - counts)[:, None],
                          ((2 * experts + 1) << idx_bits)[:, None], (2 * N_EXPERTS + 1) << idx_bits)
    keys = lax.sort(jnp.concatenate([real_keys, fill_keys.reshape(n_fill)]))
    is_real = ((keys >> idx_bits) & 1) == 0
    row_tok = jnp.where(is_real, (keys & ((1 << idx_bits) - 1)) // TOP_K,
                        jnp.arange(nb * MOE_BLOCK, dtype=jnp.int32) % n_tok)

    blk_start = jnp.arange(nb, dtype=jnp.int32) * MOE_BLOCK
    blk_e = jnp.minimum(jnp.sum((pad_end[None, :] <= blk_start[:, None]).astype(jnp.int32), axis=1),
                        N_EXPERTS - 1)
    n_used = (pad_end[-1] // MOE_BLOCK).astype(jnp.int32).reshape(1)
    has_blocks = pad_counts > 0
    order_e = jnp.cumsum(has_blocks.astype(jnp.int32)) - 1
    later = jnp.logical_and(has_blocks[None, :], experts[None, :] > experts[:, None])
    next_e = jnp.min(jnp.where(later, experts[None, :], N_EXPERTS), axis=1)
    next_e = jnp.where(next_e == N_EXPERTS, -1, next_e)
    blk_is = blk_e[:, None] == experts[None, :]

    def per_block(table):
        return jnp.sum(jnp.where(blk_is, table[None, :], 0), axis=1)

    blk_j = (blk_start - per_block(pad_start)) // MOE_BLOCK
    blk_n = jnp.maximum(per_block(pad_counts) // MOE_BLOCK, 1)
    sched = (blk_e, blk_j, blk_n, per_block(order_e), per_block(next_e), n_used)
    return probs, pos, row_tok, sched


def _rows(x, idx):
    return x.at[idx].get(mode="promise_in_bounds")


def _moe(h2, logits_t, layer, w_gu, b_gu, w_down, b_down):
    n_tok = h2.shape[0]
    probs, pos, row_tok, sched = _route(logits_t, n_tok)
    xs = _rows(h2, row_tok)
    yb = _experts(xs, sched, layer, w_gu, b_gu, w_down, b_down)
    y = _rows(yb, pos.reshape(-1)).reshape(TOP_K, n_tok, -1)
    return y, probs.T


def _combine_kernel(x_ref, y0_ref, y1_ref, y2_ref, y3_ref, p_ref, g2_ref, nf_ref, o_ref,
                    *, tile0, final, n_ctx_tiles, tiles_per_batch, ctx_row):
    row = _mod_row(pl.program_id(0) + tile0, n_ctx_tiles, tiles_per_batch, ctx_row)
    p = p_ref[...]
    y = (p[:, 0:1] * y0_ref[0] + p[:, 1:2] * y1_ref[0]) + (p[:, 2:3] * y2_ref[0] + p[:, 3:4] * y3_ref[0])
    x = x_ref[...] + g2_ref[pl.ds(row, 1), :] * y
    if final:
        x = x * lax.rsqrt(jnp.mean(x * x, axis=-1, keepdims=True) + EPS) * nf_ref[...]
    o_ref[...] = x


def _combine(x, y, probs, mod, norm_f, *, tile0, final, n_ctx_tiles, tiles_per_batch, ctx_row):
    assert y.shape[0] == TOP_K == 4
    _, n_rows, d = y.shape
    tm = ROW_TILE
    kern = functools.partial(_combine_kernel, tile0=tile0, final=final, n_ctx_tiles=n_ctx_tiles,
                             tiles_per_batch=tiles_per_batch, ctx_row=ctx_row)

    def yspec(k):
        return pl.BlockSpec((1, tm, d), lambda i: (k, i, 0))

    return pl.pallas_call(
        kern,
        grid=(n_rows // tm,),
        in_specs=[pl.BlockSpec((tm, d), lambda i: (tile0 + i, 0)),
                  yspec(0), yspec(1), yspec(2), yspec(3),
                  pl.BlockSpec((tm, TOP_K), lambda i: (i, 0)),
                  pl.BlockSpec((8, d), lambda i: (0, 5)),
                  pl.BlockSpec((1, d), lambda i: (0, 0))],
        out_specs=pl.BlockSpec((tm, d), lambda i: (i, 0)),
        out_shape=jax.ShapeDtypeStruct((n_rows, d), F32),
        compiler_params=_cparams("parallel"),
        name="moe_combine",
    )(x, y, y, y, y, probs, mod, norm_f)


def _embed_kernel(ctx_ref, x_ref, pos_ref, o_ref, *, n_ctx_tiles):
    j = pl.program_id(0)
    b = pl.program_id(1)

    @pl.when(jnp.logical_and(j == 0, b < n_ctx_tiles))
    def _():
        o_ref[...] = ctx_ref[...]

    @pl.when(j > 0)
    def _():
        o_ref[...] = x_ref[...] + pos_ref[...]


def _embed(ctx2, x2, pos, *, batch):
    tm = ROW_TILE
    d = x2.shape[1]
    l = pos.shape[0]
    n_ctx_tiles = ctx2.shape[0] // tm
    nt = l // tm
    assert n_ctx_tiles <= batch
    return pl.pallas_call(
        functools.partial(_embed_kernel, n_ctx_tiles=n_ctx_tiles),
        grid=(1 + nt, batch),
        in_specs=[pl.BlockSpec((tm, d), lambda j, b: (jnp.minimum(b, n_ctx_tiles - 1), 0)),
                  pl.BlockSpec((tm, d), lambda j, b: (b * nt + jnp.maximum(j - 1, 0), 0)),
                  pl.BlockSpec((tm, d), lambda j, b: (jnp.maximum(j - 1, 0), 0))],
        out_specs=pl.BlockSpec(
            (tm, d), lambda j, b: (jnp.where(j == 0, jnp.minimum(b, n_ctx_tiles - 1),
                                             n_ctx_tiles + b * nt + j - 1), 0)),
        out_shape=jax.ShapeDtypeStruct((ctx2.shape[0] + x2.shape[0], d), F32),
        compiler_params=_cparams("arbitrary", "arbitrary"),
        name="embed_tokens",
    )(ctx2, x2, pos)


def _grid_pos_embed(n, d):
    rows = n // GRID_W
    row = jnp.broadcast_to(jnp.arange(rows, dtype=F32)[:, None], (rows, GRID_W)).reshape(-1)
    col = jnp.broadcast_to(jnp.arange(GRID_W, dtype=F32)[None, :], (rows, GRID_W)).reshape(-1)
    quarter = d // 4
    omega = 1.0 / (10000.0 ** (jnp.arange(quarter, dtype=F32) / quarter))

    def enc(p):
        a = p[:, None] * omega
        return jnp.concatenate([jnp.sin(a), jnp.cos(a)], axis=-1)

    return jnp.concatenate([enc(row), enc(col)], axis=-1)


def kernel(x, c, ctx, c_ctx, w_mod, b_mod, norm1, w_in, w_a2, b_a, gla_norm, w_s, b_s, p_gla, p_sg, p_ft,
           w_out, norm2, w_router, b_router, w_gu, b_gu, w_down, b_down, norm_f):
    batch, l, d = x.shape
    lc = ctx.shape[1]
    depth = w_mod.shape[0]
    n_ctx_rows = batch * lc
    t_rows = n_ctx_rows + batch * l
    assert lc % ROW_TILE == 0 and l % ROW_TILE == 0 and batch < 8

    tm1 = math.gcd(math.gcd(n_ctx_rows, l), 1024)
    assert tm1 % ROW_TILE == 0
    proj_tiles = dict(n_ctx_tiles=n_ctx_rows // tm1, tiles_per_batch=l // tm1, ctx_row=batch)
    row_tiles = dict(n_ctx_tiles=n_ctx_rows // ROW_TILE, tiles_per_batch=l // ROW_TILE, ctx_row=batch)
    tm2 = ROW_TILE
    tm3 = math.gcd(tm1, 2 * ROW_TILE)
    big_tiles = dict(n_ctx_tiles=n_ctx_rows // tm3, tiles_per_batch=l // tm3, ctx_row=batch)

    o_q = 0
    o_k = o_q + W_Q
    o_v = o_k + W_Q
    o_r = o_v + W_V
    o_a = o_r + W_V
    o_su = o_a + 2 * GLA_RANK
    o_sv = o_su + W_SG
    o_f = o_sv + W_SG
    o_g = o_f + W_FT
    n_gates = 3 * d
    cols = dict(gates=0, q=n_gates, r=n_gates + 2 * W_Q + W_V, su=n_gates + 2 * W_Q + 2 * W_V)
    cols["sv"] = cols["su"] + W_SG
    cols["f"] = cols["sv"] + W_SG
    n_main = cols["f"] + W_FT

    xt = _embed(ctx.reshape(n_ctx_rows, d), x.reshape(batch * l, d), _grid_pos_embed(l, d), batch=batch)

    cond = jnp.zeros((8, d), F32).at[:batch].set(c).at[batch].set(c_ctx)
    mods = _mod_vectors(cond, w_mod, b_mod)

    cc, sc = _dft_tables(FT_CH)
    chan_tab = jnp.concatenate([cc, sc], axis=1)
    assert l % (FT_RADIX * FT_TILE) == 0
    ft_tables = _ft_long_tables(l)
    cos_c, sin_c = _dft_tables(lc)

    w_main = jnp.concatenate(
        [w_in[:, :, o_g:], w_in[:, :, o_q:o_a], w_in[:, :, o_su:o_g]], axis=2).astype(BF16)
    w_alr = jnp.pad(w_in[:, :, o_a:o_su], ((0, 0), (0, 0), (0, LANES - 2 * GLA_RANK))).astype(BF16)

    for i in range(depth):
        last = i == depth - 1
        w_a = w_alr[i]
        wa2 = jnp.zeros((LANES, 2 * W_Q), F32)
        wa2 = wa2.at[:GLA_RANK, :W_Q].set(w_a2[i, 0]).at[GLA_RANK:2 * GLA_RANK, W_Q:].set(w_a2[i, 1])
        mod = mods[i]

        z, la = _project(xt, mod, norm1[i][None], w_main, w_a, wa2.astype(BF16), b_a[i].reshape(1, 2 * W_Q),
                         layer=i, tm=tm1, tn=1536, **proj_tiles)
        assert z.shape[1] == n_main

        o_fw, o_bw = _gla(z, la, batch=batch, lc=lc, l=l, col_q=cols["q"])

        xc_c, xs_c = _ft_channel(z, chan_tab, col_f=cols["f"], row0=0, n_rows=n_ctx_rows)
        xc_l, xs_l = _ft_channel(z, chan_tab, col_f=cols["f"], row0=n_ctx_rows, n_rows=batch * l)
        y_ft = (_ft_positions(cos_c, sin_c, xc_c, xs_c, batch=batch, seq=lc),
                _ft_positions_long(ft_tables, xc_l, xs_l, batch=batch, seq=l))

        bs_b = jnp.broadcast_to(b_s[i][:, :, None], (SG_GROUPS, SG_CHUNK, SG_CH))
        merged = _branches(z, o_fw, o_bw, y_ft, gla_norm[i][None], w_s[i].astype(BF16), bs_b,
                           p_gla[i].astype(BF16), p_sg[i].astype(BF16), p_ft[i].astype(BF16), cols=cols, tm=tm2)

        w_r = jnp.pad(w_router[i], ((0, 0), (0, LANES - N_EXPERTS)))
        b_r = jnp.pad(b_router[i], (0, LANES - N_EXPERTS)).reshape(1, LANES)
        xt, h2, logits_t = _out_project(merged, xt, w_out[i].astype(BF16), mod, norm2[i][None], w_r, b_r,
                                      tm=tm3, **big_tiles)

        row0 = n_ctx_rows if last else 0
        y, probs = _moe(h2[row0:], logits_t[:N_EXPERTS, row0:], i, w_gu, b_gu, w_down, b_down)
        xt = _combine(xt, y, probs, mod, norm_f[None], tile0=row0 // ROW_TILE, final=last, **row_tiles)

    return xt.reshape(batch, l, d)
```

```python
import functools
import math

import jax
import jax.numpy as jnp
from jax import lax
from jax.experimental import pallas as pl
from jax.experimental.pallas import tpu as pltpu

F32 = jnp.float32
BF16 = jnp.bfloat16

EPS = 1e-6
GRID_W = 64
GLA_HEADS = 4
GLA_DK = 128
GLA_DV = 256
GLA_RANK = 16
GLA_TAU = 16.0
GLA_CHUNK = 64
GLA_SUB = 16
GLA_STAGES = 3
SG_GROUPS = 4
SG_CH = 128
SG_CHUNK = 128
FT_GROUPS = 4
FT_CH = 128
N_EXPERTS = 32
TOP_K = 4
SWIGLU_LIMIT = 7.0
SWIGLU_ALPHA = 1.702
MOE_BLOCK = 256
W_CHUNK = 512

LOG2E = math.log2(math.e)
LANES = 128
ROW_TILE = 256
VMEM_LIMIT = 56 * 1024 * 1024

W_Q = GLA_HEADS * GLA_DK
W_V = GLA_HEADS * GLA_DV
W_SG = SG_GROUPS * SG_CH
W_FT = FT_GROUPS * FT_CH


def _cparams(*sem):
    return pltpu.CompilerParams(dimension_semantics=sem, vmem_limit_bytes=VMEM_LIMIT)


def _sigmoid(x):
    return 1.0 / (1.0 + jnp.exp(-x))


def _gelu_tanh(x):
    return 0.5 * x * (1.0 + jnp.tanh(math.sqrt(2.0 / math.pi) * (x + 0.044715 * (x * x * x))))


def _dot(a, b):
    return jnp.dot(a, b, preferred_element_type=F32)


def _pack_bf16_pairs(x):
    bits = lax.bitcast_convert_type(x.astype(F32), jnp.uint32)
    half = x.shape[1] // 2
    return (bits[:, :half] >> 16) | (bits[:, half:] & jnp.uint32(0xFFFF0000))


def _unpack_bf16_pairs(u):
    lo = lax.bitcast_convert_type(u << 16, F32)
    hi = lax.bitcast_convert_type(u & jnp.uint32(0xFFFF0000), F32)
    return jnp.concatenate([lo, hi], axis=1).astype(BF16)


def _mod_row(tile, n_ctx_tiles, tiles_per_batch, ctx_row):
    return jnp.where(tile < n_ctx_tiles, ctx_row, (tile - n_ctx_tiles) // tiles_per_batch)


def _mod_kernel(c_ref, w_ref, b_ref, o_ref):
    c = c_ref[...]
    s = (c * _sigmoid(c)).astype(BF16)
    o_ref[0] = _dot(s, w_ref[0].astype(BF16)) + b_ref[0]


def _mod_vectors(cond, w_mod, b_mod):
    depth, d, n = w_mod.shape
    tn = 1024
    return pl.pallas_call(
        _mod_kernel,
        grid=(depth, n // tn),
        in_specs=[
            pl.BlockSpec((8, d), lambda l, j: (0, 0)),
            pl.BlockSpec((1, d, tn), lambda l, j: (l, 0, j)),
            pl.BlockSpec((1, 1, tn), lambda l, j: (l, 0, j)),
        ],
        out_specs=pl.BlockSpec((1, 8, tn), lambda l, j: (l, 0, j)),
        out_shape=jax.ShapeDtypeStruct((depth, 8, n), F32),
        compiler_params=_cparams("parallel", "parallel"),
        name="mod_vectors",
    )(cond, w_mod, b_mod.reshape(depth, 1, n))


def _proj_kernel(x_ref, sh_ref, sc_ref, g_ref, w_ref, wa_ref, wa2_ref, ba_ref,
                 z_ref, la_ref, a_scr, *, n_ctx_tiles, tiles_per_batch, ctx_row):
    i = pl.program_id(0)
    j = pl.program_id(1)

    @pl.when(j == 0)
    def _():
        row = _mod_row(i, n_ctx_tiles, tiles_per_batch, ctx_row)
        scale = 1.0 + sc_ref[pl.ds(row, 1), :]
        shift = sh_ref[pl.ds(row, 1), :]

        def body(r, carry):
            rows = pl.ds(pl.multiple_of(r * ROW_TILE, ROW_TILE), ROW_TILE)
            x = x_ref[rows, :]
            y = x * lax.rsqrt(jnp.mean(x * x, axis=-1, keepdims=True) + EPS) * g_ref[...]
            hb = (y * scale + shift).astype(BF16)
            a_scr[rows, :] = hb
            a_lr = _dot(hb, wa_ref[...])
            pre = _dot(a_lr.astype(BF16), wa2_ref[...]) + ba_ref[...]
            la_ref[rows, :] = -(jnp.maximum(-pre, 0.0) + jnp.log1p(jnp.exp(-jnp.abs(pre)))) * (1.0 / GLA_TAU)
            return carry

        lax.fori_loop(0, x_ref.shape[0] // ROW_TILE, body, 0)

    z_ref[...] = _dot(a_scr[...], w_ref[0]).astype(BF16)


def _project(x, mod, g, w_main, w_a, w_a2, b_a, *, layer, tm, tn, n_ctx_tiles, tiles_per_batch, ctx_row):
    t, d = x.shape
    n = w_main.shape[2]
    kern = functools.partial(_proj_kernel, n_ctx_tiles=n_ctx_tiles,
                             tiles_per_batch=tiles_per_batch, ctx_row=ctx_row)
    return pl.pallas_call(
        kern,
        grid=(t // tm, n // tn),
        in_specs=[
            pl.BlockSpec((tm, d), lambda i, j: (i, 0)),
            pl.BlockSpec((8, d), lambda i, j: (0, 0)),
            pl.BlockSpec((8, d), lambda i, j: (0, 1)),
            pl.BlockSpec((1, d), lambda i, j: (0, 0)),
            pl.BlockSpec((1, d, tn), lambda i, j: (layer, 0, j)),
            pl.BlockSpec((d, LANES), lambda i, j: (0, 0)),
            pl.BlockSpec((LANES, 2 * W_Q), lambda i, j: (0, 0)),
            pl.BlockSpec((1, 2 * W_Q), lambda i, j: (0, 0)),
        ],
        out_specs=[
            pl.BlockSpec((tm, tn), lambda i, j: (i, j)),
            pl.BlockSpec((tm, 2 * W_Q), lambda i, j: (i, 0)),
        ],
        out_shape=[
            jax.ShapeDtypeStruct((t, n), BF16),
            jax.ShapeDtypeStruct((t, 2 * W_Q), F32),
        ],
        scratch_shapes=[pltpu.VMEM((tm, d), BF16)],
        compiler_params=_cparams("parallel", "arbitrary"),
        name="norm_mod_project",
    )(x, mod, mod, g, w_main, w_a, w_a2, b_a)


def _gla_pairs(rev):
    c, sub = GLA_CHUNK, GLA_SUB
    half = c // 2
    if rev:
        return ((0, half, half, c, half - 1),
                (0, sub, sub, half, sub - 1),
                (half, half + sub, half + sub, c, half + sub - 1))
    return ((half, c, 0, half, half),
            (sub, half, 0, sub, sub),
            (half + sub, c, half, half + sub, half + sub))


def _place_rows(x, r0, n):
    parts = []
    if r0:
        parts.append(jnp.zeros((r0, x.shape[1]), x.dtype))
    parts.append(x)
    rest = n - r0 - x.shape[0]
    if rest:
        parts.append(jnp.zeros((rest, x.shape[1]), x.dtype))
    return jnp.concatenate(parts, axis=0)


def _gla_chunk(q_ref, k_ref, v_ref, la_ref, e_ref, o_ref, s_ref, cum_ref, kc_ref, r0, h, rev):
    c, sub = GLA_CHUNK, GLA_SUB
    nsub = c // sub
    rows = pl.ds(r0, c)
    kcols = slice(h * GLA_DK, (h + 1) * GLA_DK)
    vcols = slice(h * GLA_DV, (h + 1) * GLA_DV)
    q = q_ref[rows, kcols].astype(F32) * (GLA_DK ** -0.5)
    k = k_ref[rows, kcols].astype(F32)
    v = v_ref[rows, vcols]
    la = la_ref[rows, kcols]

    ri = lax.broadcasted_iota(jnp.int32, (c, c), 0)
    ci = lax.broadcasted_iota(jnp.int32, (c, c), 1)
    before = (ci >= ri) if rev else (ci <= ri)
    tri = jnp.where(before, 1.0, 0.0).astype(BF16)
    la_hi = la.astype(BF16)
    rem = la - la_hi.astype(F32)
    la_mid = rem.astype(BF16)
    la_lo = (rem - la_mid.astype(F32)).astype(BF16)
    cum = (_dot(tri, la_hi) + _dot(tri, la_mid) + _dot(tri, la_lo)) * LOG2E
    cum_ref[...] = cum
    kc_ref[...] = k
    yield
    last = 0 if rev else c - 1
    total = cum[last:last + 1, :]

    s = s_ref[...]
    o = _dot((q * jnp.exp2(cum)).astype(BF16), s.astype(BF16))

    qparts, kparts = [], []
    for l0, l1, e0, e1, ref in _gla_pairs(rev):
        cr = cum[ref:ref + 1, :]
        qparts.append(_place_rows(q[l0:l1] * jnp.exp2(cum[l0:l1] - cr), l0, c))
        kparts.append(_place_rows(k[e0:e1] * jnp.exp2(cr - cum[e0:e1]), e0, c))
    s_off = lax.dot_general(jnp.concatenate(qparts, axis=1).astype(BF16),
                            jnp.concatenate(kparts, axis=1).astype(BF16),
                            (((1,), (1,)), ((), ())), preferred_element_type=F32)

    hs = sub // 2

    def halves(x, which):
        return jnp.concatenate([x[a * sub + w * hs:a * sub + (w + 1) * hs] for a in range(nsub) for w in which],
                               axis=0)

    parts = []
    for jj in range(sub):
        if rev:
            which = (0, 1) if jj >= hs else (0,)
        else:
            which = (0, 1) if jj < hs else (1,)
        nrow = hs * len(which)
        cj = jnp.concatenate(
            [jnp.broadcast_to(cum_ref[a * sub + jj:a * sub + jj + 1, :], (nrow, GLA_DK)) for a in range(nsub)],
            axis=0)
        kj = jnp.concatenate(
            [jnp.broadcast_to(kc_ref[a * sub + jj:a * sub + jj + 1, :], (nrow, GLA_DK)) for a in range(nsub)],
            axis=0)
        prod = halves(q, which) * jnp.exp2(jnp.minimum(halves(cum, which) - cj, 0.0)) * kj
        if len(which) == 1:
            zero = jnp.zeros((hs, GLA_DK), F32)
            pieces = []
            for a in range(nsub):
                live = prod[a * hs:(a + 1) * hs]
                pieces += [live, zero] if which[0] == 0 else [zero, live]
            prod = jnp.concatenate(pieces, axis=0)
        parts.append(prod.astype(BF16))
    diag = _dot(jnp.concatenate(parts, axis=1), e_ref[...])
    yield

    lag = (ci - ri) if rev else (ri - ci)
    room = (sub - 1 - ri % sub) if rev else (ri % sub)
    in_diag = lag.astype(jnp.uint32) <= room.astype(jnp.uint32)
    p = jnp.where(in_diag, diag, 0.0) + s_off
    o_ref[rows, vcols] = o + _dot(p.astype(BF16), v)

    k_out = (k * jnp.exp2(total - cum)).astype(BF16)
    upd = lax.dot_general(k_out, v, (((0,), (0,)), ((), ())), preferred_element_type=F32)
    dmat = jnp.broadcast_to(jnp.exp2(total), (GLA_DK, GLA_DK)).T
    for half in range(GLA_DV // GLA_DK):
        cols = slice(half * GLA_DK, (half + 1) * GLA_DK)
        s_ref[:, cols] = s[:, cols] * dmat + upd[:, cols]


def _gla_kernel(qf, kf, vf, laf, qb, kb, vb, lab, e_ref, of_ref, ob_ref, *scratch):
    n = 2 * GLA_HEADS
    states, cums, keys = scratch[:n], scratch[n:2 * n], scratch[2 * n:]

    @pl.when(pl.program_id(1) == 0)
    def _():
        for s_ref in states:
            s_ref[...] = jnp.zeros_like(s_ref)

    nchunk = ROW_TILE // GLA_CHUNK

    def body(c, carry):
        r_f = pl.multiple_of(c * GLA_CHUNK, GLA_CHUNK)
        r_b = pl.multiple_of((nchunk - 1 - c) * GLA_CHUNK, GLA_CHUNK)
        units = []
        for h in range(GLA_HEADS):
            f, b = h, GLA_HEADS + h
            units.append(_gla_chunk(qf, kf, vf, laf, e_ref, of_ref, states[f], cums[f], keys[f], r_f, h, False))
            units.append(_gla_chunk(qb, kb, vb, lab, e_ref, ob_ref, states[b], cums[b], keys[b], r_b, h, True))
        for _ in range(GLA_STAGES):
            for u in units:
                next(u, None)
        return carry

    lax.fori_loop(0, nchunk, body, 0)


def _gla(z, la, *, batch, lc, l, col_q):
    t_rows = z.shape[0]
    nctx = lc // ROW_TILE
    nlat = l // ROW_TILE
    steps = nctx + nlat
    qb0 = col_q // W_Q
    kb0 = (col_q + W_Q) // W_Q
    vb0 = (col_q + 2 * W_Q) // W_V

    def fwd_tile(b, t):
        return jnp.where(t < nctx, b * nctx + t, batch * nctx + b * nlat + (t - nctx))

    def bwd_tile(b, t):
        return jnp.where(t < nctx, b * nctx + (nctx - 1 - t), batch * nctx + b * nlat + (steps - 1 - t))

    def spec(width, tile_fn, col):
        return pl.BlockSpec((ROW_TILE, width), lambda b, t: (tile_fn(b, t), col))

    kk = jnp.arange(GLA_SUB * GLA_DK, dtype=jnp.int32)[:, None] // GLA_DK
    cc = jnp.arange(GLA_CHUNK, dtype=jnp.int32)[None, :] % GLA_SUB
    e_mat = (kk == cc).astype(BF16)

    o_shape = jax.ShapeDtypeStruct((t_rows, W_V), F32)
    n_units = 2 * GLA_HEADS
    state = pltpu.VMEM((GLA_DK, GLA_DV), F32)
    chunk = pltpu.VMEM((GLA_CHUNK, GLA_DK), F32)
    return pl.pallas_call(
        _gla_kernel,
        grid=(batch, steps),
        in_specs=[
            spec(W_Q, fwd_tile, qb0), spec(W_Q, fwd_tile, kb0), spec(W_V, fwd_tile, vb0), spec(W_Q, fwd_tile, 0),
            spec(W_Q, bwd_tile, qb0), spec(W_Q, bwd_tile, kb0), spec(W_V, bwd_tile, vb0), spec(W_Q, bwd_tile, 1),
            pl.BlockSpec((GLA_SUB * GLA_DK, GLA_CHUNK), lambda b, t: (0, 0)),
        ],
        out_specs=[spec(W_V, fwd_tile, 0), spec(W_V, bwd_tile, 0)],
        out_shape=[o_shape, o_shape],
        scratch_shapes=[state] * n_units + [chunk] * (2 * n_units),
        compiler_params=_cparams("parallel", "arbitrary"),
        name="gla_scan",
    )(z, z, z, la, z, z, z, la, e_mat)


def _dft_tables(n):
    scale = n ** -0.5
    p = jnp.arange(n, dtype=jnp.int32)[:, None]

    def trig(cols, period):
        ang = ((p * cols[None, :]) % period).astype(F32) * (2.0 * math.pi / period)
        return jnp.cos(ang), jnp.sin(ang)

    n2 = 1
    while n2 * n2 < n:
        n2 *= 2
    if n % n2 or n2 == 1:
        c, s = trig(jnp.arange(n, dtype=jnp.int32), n)
        return (c * scale).astype(BF16), (s * scale).astype(BF16)
    n1 = n // n2
    ca, sa = trig(jnp.arange(n1, dtype=jnp.int32), n1)
    cb, sb = trig(jnp.arange(n2, dtype=jnp.int32), n)
    ca, sa, cb, sb = ca[:, :, None], sa[:, :, None], cb[:, None, :] * scale, sb[:, None, :] * scale
    return ((ca * cb - sa * sb).reshape(n, n).astype(BF16), (sa * cb + ca * sb).reshape(n, n).astype(BF16))


def _ft_chan_kernel(f_ref, cs_ref, xc_ref, xs_ref):
    for g in range(FT_GROUPS):
        cols = slice(g * FT_CH, (g + 1) * FT_CH)
        y = _dot(f_ref[:, cols], cs_ref[...])
        xc_ref[:, cols] = y[:, :FT_CH].astype(BF16)
        xs_ref[:, cols] = y[:, FT_CH:].astype(BF16)


def _ft_channel(z, cs, *, col_f, row0, n_rows):
    tm = math.gcd(math.gcd(row0, n_rows), 1024)
    shape = jax.ShapeDtypeStruct((n_rows, W_FT), BF16)
    return pl.pallas_call(
        _ft_chan_kernel,
        grid=(n_rows // tm,),
        in_specs=[pl.BlockSpec((tm, W_FT), lambda i: (row0 // tm + i, col_f // W_FT)),
                  pl.BlockSpec((FT_CH, 2 * FT_CH), lambda i: (0, 0))],
        out_specs=[pl.BlockSpec((tm, W_FT), lambda i: (i, 0))] * 2,
        out_shape=[shape, shape],
        compiler_params=_cparams("parallel"),
        name="fourier_channels",
    )(z, cs)


def _ft_pos_kernel(c_ref, s_ref, xc_ref, xs_ref, o_ref):
    o_ref[...] = _dot(c_ref[...], xc_ref[...]) - _dot(s_ref[...], xs_ref[...])


def _ft_positions(cos_t, sin_t, xc, xs, *, batch, seq):
    nt = seq // ROW_TILE
    x_spec = pl.BlockSpec((seq, W_FT), lambda b, j: (b, 0))
    t_spec = pl.BlockSpec((ROW_TILE, seq), lambda b, j: (j, 0))
    return pl.pallas_call(
        _ft_pos_kernel,
        grid=(batch, nt),
        in_specs=[t_spec, t_spec, x_spec, x_spec],
        out_specs=pl.BlockSpec((ROW_TILE, W_FT), lambda b, j: (b * nt + j, 0)),
        out_shape=jax.ShapeDtypeStruct((batch * seq, W_FT), F32),
        compiler_params=_cparams("parallel", "parallel"),
        name="fourier_positions",
    )(cos_t, sin_t, xc, xs)


FT_RADIX = 16
FT_TILE = 16


def _ft_stage1_kernel(xc_ref, xs_ref, k_ref, tc_ref, ts_ref, o_ref):
    rows = FT_RADIX * FT_TILE
    zin = jnp.concatenate([xc_ref[0].reshape(rows, W_FT), xs_ref[0].reshape(rows, W_FT)], axis=0)
    a = _dot(k_ref[...], zin)
    re, im = a[:rows], a[rows:]
    tc, ts = tc_ref[0], ts_ref[0]
    o_ref[0, 0] = (re * tc + im * ts).astype(BF16).reshape(FT_RADIX, FT_TILE, W_FT)
    o_ref[0, 1] = (im * tc - re * ts).astype(BF16).reshape(FT_RADIX, FT_TILE, W_FT)


def _ft_stage2_kernel(b_ref, t_ref, o_ref):
    rhs = jnp.concatenate([b_ref[0, 0, 0], b_ref[0, 1, 0]], axis=0)
    o_ref[0, 0] = _dot(t_ref[...], rhs)


def _ft_long_tables(seq):
    r, n1 = FT_RADIX, seq // FT_RADIX

    def trig(rows, cols, period):
        ang = ((rows[:, None] * cols[None, :]) % period).astype(F32) * (2.0 * math.pi / period)
        return jnp.cos(ang), jnp.sin(ang)

    ir = jnp.arange(r, dtype=jnp.int32)
    i1 = jnp.arange(n1, dtype=jnp.int32)
    c1, s1 = trig(ir, ir, r)
    t1 = jnp.concatenate([jnp.concatenate([c1, -s1], axis=1), jnp.concatenate([-s1, -c1], axis=1)], axis=0)
    k1 = jnp.kron(t1 * r ** -0.5, jnp.eye(FT_TILE, dtype=F32)).astype(BF16)
    tc, ts = trig(ir, i1, seq)

    def per_tile(t):
        return t.reshape(r, n1 // FT_TILE, FT_TILE).transpose(1, 0, 2).reshape(n1 // FT_TILE, r * FT_TILE, 1)

    c2, s2 = trig(i1, i1, n1)
    t2 = (jnp.concatenate([c2, s2], axis=1) * n1 ** -0.5).astype(BF16)
    return k1, per_tile(tc), per_tile(ts), t2


def _ft_positions_long(tables, xc, xs, *, batch, seq):
    k1, tc, ts, t2 = tables
    r, n1 = FT_RADIX, seq // FT_RADIX
    nt = n1 // FT_TILE
    rows = r * FT_TILE
    x4 = (batch, r, n1, W_FT)
    x_spec = pl.BlockSpec((1, r, FT_TILE, W_FT), lambda b, t: (b, 0, t, 0))
    tw_spec = pl.BlockSpec((1, rows, 1), lambda b, t: (t, 0, 0))
    stage1 = pl.pallas_call(
        _ft_stage1_kernel,
        grid=(batch, nt),
        in_specs=[x_spec, x_spec, pl.BlockSpec((2 * rows, 2 * rows), lambda b, t: (0, 0)), tw_spec, tw_spec],
        out_specs=pl.BlockSpec((1, 2, r, FT_TILE, W_FT), lambda b, t: (b, 0, 0, t, 0)),
        out_shape=jax.ShapeDtypeStruct((batch, 2, r, n1, W_FT), BF16),
        compiler_params=_cparams("parallel", "parallel"),
        name="fourier_stage1",
    )(xc.reshape(x4), xs.reshape(x4), k1, tc, ts)
    stage2 = pl.pallas_call(
        _ft_stage2_kernel,
        grid=(batch, r),
        in_specs=[pl.BlockSpec((1, 2, 1, n1, W_FT), lambda b, p: (b, 0, p, 0, 0)),
                  pl.BlockSpec((n1, 2 * n1), lambda b, p: (0, 0))],
        out_specs=pl.BlockSpec((1, 1, n1, W_FT), lambda b, p: (b, p, 0, 0)),
        out_shape=jax.ShapeDtypeStruct((batch, r, n1, W_FT), F32),
        compiler_params=_cparams("parallel", "parallel"),
        name="fourier_stage2",
    )(stage1, t2)
    return stage2.transpose(0, 2, 1, 3).reshape(batch * seq, W_FT)


def _branch_kernel(gg_ref, gs_ref, gf_ref, of_ref, ob_ref, r_ref, su_ref, sv_ref, yc_ref, yl_ref,
                   gn_ref, ws_ref, bs_ref, pg_ref, ps_ref, pf_ref, m_ref, *, n_ctx_tiles):
    o = of_ref[...] + ob_ref[...]
    heads = []
    for h in range(GLA_HEADS):
        oh = o[:, h * GLA_DV:(h + 1) * GLA_DV]
        heads.append(oh * lax.rsqrt(jnp.mean(oh * oh, axis=-1, keepdims=True) + EPS))
    r = r_ref[...].astype(F32)
    y_gla = (jnp.concatenate(heads, axis=-1) * gn_ref[...]) * (r * _sigmoid(r))
    acc = _sigmoid(gg_ref[...].astype(F32)) * _dot(y_gla.astype(BF16), pg_ref[...])

    u = _gelu_tanh(su_ref[...].astype(F32))
    v = _gelu_tanh(sv_ref[...].astype(F32))
    groups = []
    for g in range(SG_GROUPS):
        vg = v[:, g * SG_CH:(g + 1) * SG_CH]
        mu = jnp.mean(vg, axis=-1, keepdims=True)
        var = jnp.mean(jnp.square(vg - mu), axis=-1, keepdims=True)
        vn = ((vg - mu) * lax.rsqrt(var + EPS)).astype(BF16)
        chunks = []
        for c in range(v.shape[0] // SG_CHUNK):
            chunks.append(_dot(ws_ref[g], vn[c * SG_CHUNK:(c + 1) * SG_CHUNK, :]) + bs_ref[g])
        groups.append(jnp.concatenate(chunks, axis=0))
    y_sg = u * jnp.concatenate(groups, axis=-1)
    acc = acc + _sigmoid(gs_ref[...].astype(F32)) * _dot(y_sg.astype(BF16), ps_ref[...])

    y_ft = jnp.where(pl.program_id(0) < n_ctx_tiles, yc_ref[...], yl_ref[...])
    acc = acc + _sigmoid(gf_ref[...].astype(F32)) * _dot(y_ft.astype(BF16), pf_ref[...])
    m_ref[...] = acc.astype(BF16)


def _branches(z, o_f, o_b, y_ft, gla_norm, w_s, b_s, p_gla, p_sg, p_ft, *, cols, tm):
    t_rows = z.shape[0]
    d = p_gla.shape[1]
    n_ctx_tiles = y_ft[0].shape[0] // tm
    kern = functools.partial(_branch_kernel, n_ctx_tiles=n_ctx_tiles)

    def zspec(width, col0):
        return pl.BlockSpec((tm, width), lambda i: (i, col0 // width))

    def rows(width):
        return pl.BlockSpec((tm, width), lambda i: (i, 0))

    def const(shape):
        return pl.BlockSpec(shape, lambda i: (0,) * len(shape))

    return pl.pallas_call(
        kern,
        grid=(t_rows // tm,),
        in_specs=[
            zspec(d, cols["gates"]), zspec(d, cols["gates"] + d), zspec(d, cols["gates"] + 2 * d),
            rows(W_V), rows(W_V), zspec(W_V, cols["r"]), zspec(W_SG, cols["su"]), zspec(W_SG, cols["sv"]),
            pl.BlockSpec((tm, W_FT), lambda i: (jnp.minimum(i, n_ctx_tiles - 1), 0)),
            pl.BlockSpec((tm, W_FT), lambda i: (jnp.maximum(i - n_ctx_tiles, 0), 0)),
            const((1, W_V)), const((SG_GROUPS, SG_CHUNK, SG_CHUNK)), const((SG_GROUPS, SG_CHUNK, SG_CH)),
            const((W_V, d)), const((W_SG, d)), const((W_FT, d)),
        ],
        out_specs=rows(d),
        out_shape=jax.ShapeDtypeStruct((t_rows, d), BF16),
        compiler_params=_cparams("parallel"),
        name="branch_merge",
    )(z, z, z, o_f, o_b, z, z, z, y_ft[0], y_ft[1], gla_norm, w_s, b_s, p_gla, p_sg, p_ft)


def _outproj_kernel(m_ref, x_ref, w_ref, g1_ref, n2_ref, sh_ref, sc_ref, wr_ref, br_ref,
                    xo_ref, h_ref, lg_ref, *, n_ctx_tiles, tiles_per_batch, ctx_row):
    row = _mod_row(pl.program_id(0), n_ctx_tiles, tiles_per_batch, ctx_row)
    gate1 = g1_ref[pl.ds(row, 1), :]
    scale2 = 1.0 + sc_ref[pl.ds(row, 1), :]
    shift2 = sh_ref[pl.ds(row, 1), :]
    w = wr_ref[...]
    w_hi = w.astype(BF16)
    w_lo = (w - w_hi.astype(F32)).astype(BF16)
    for r0 in range(0, x_ref.shape[0], ROW_TILE):
        rows = slice(r0, r0 + ROW_TILE)
        x = x_ref[rows, :] + gate1 * _dot(m_ref[rows, :], w_ref[...])
        xo_ref[rows, :] = x
        y = x * lax.rsqrt(jnp.mean(x * x, axis=-1, keepdims=True) + EPS) * n2_ref[...]
        h = y * scale2 + shift2
        h_hi = h.astype(BF16)
        h_ref[rows, :] = _pack_bf16_pairs(h_hi)
        h_lo = (h - h_hi.astype(F32)).astype(BF16)
        lg = _dot(h_hi, w_hi) + (_dot(h_hi, w_lo) + _dot(h_lo, w_hi)) + br_ref[...]
        lg_ref[:, rows] = lg.T


def _out_project(merged, x, w_out, mod, norm2, w_router, b_router, *, tm, n_ctx_tiles, tiles_per_batch, ctx_row):
    t_rows, d = x.shape
    kern = functools.partial(_outproj_kernel, n_ctx_tiles=n_ctx_tiles,
                             tiles_per_batch=tiles_per_batch, ctx_row=ctx_row)

    def rows(width):
        return pl.BlockSpec((tm, width), lambda i: (i, 0))

    def modspec(k):
        return pl.BlockSpec((8, d), lambda i: (0, k))

    return pl.pallas_call(
        kern,
        grid=(t_rows // tm,),
        in_specs=[rows(d), rows(d), pl.BlockSpec((d, d), lambda i: (0, 0)),
                  modspec(2), pl.BlockSpec((1, d), lambda i: (0, 0)), modspec(3), modspec(4),
                  pl.BlockSpec((d, LANES), lambda i: (0, 0)), pl.BlockSpec((1, LANES), lambda i: (0, 0))],
        out_specs=[rows(d), rows(d // 2), pl.BlockSpec((LANES, tm), lambda i: (0, i))],
        out_shape=[jax.ShapeDtypeStruct((t_rows, d), F32), jax.ShapeDtypeStruct((t_rows, d // 2), jnp.uint32),
                   jax.ShapeDtypeStruct((LANES, t_rows), F32)],
        compiler_params=_cparams("parallel"),
        name="out_project",
    )(merged, x, w_out, mod, norm2, mod, mod, w_router, b_router)


def _expert_kernel(be_ref, jb_ref, nblk_ref, ord_ref, nxt_ref, nu_ref,
                   x_ref, wgu_hbm, bgu_ref, wd_hbm, bd_ref, y_ref,
                   wgu_bf, wd_bf, stage, sems, *, layer):
    i = pl.program_id(0)
    n_gu_chunks = wgu_bf.shape[1] // W_CHUNK
    n_chunks = n_gu_chunks + wd_bf.shape[1] // W_CHUNK

    def chunk_copy(e, q):
        if q < n_gu_chunks:
            src = wgu_hbm.at[layer, e, pl.ds(q * W_CHUNK, W_CHUNK), :]
        else:
            src = wd_hbm.at[layer, e, pl.ds((q - n_gu_chunks) * W_CHUNK, W_CHUNK), :]
        return pltpu.make_async_copy(src, stage.at[q % 2], sems.at[q % 2])

    def land(e, q, slot):
        chunk_copy(e, q).wait()
        w = stage[q % 2].astype(BF16)
        if q < n_gu_chunks:
            wgu_bf[slot, pl.ds(q * W_CHUNK, W_CHUNK), :] = w
        else:
            wd_bf[slot, pl.ds((q - n_gu_chunks) * W_CHUNK, W_CHUNK), :] = w
        if q + 2 < n_chunks:
            chunk_copy(e, q + 2).start()

    active = i < nu_ref[0]
    e = be_ref[i]
    j = jb_ref[i]
    n = nblk_ref[i]
    slot = ord_ref[i] % 2
    nxt = nxt_ref[i]

    @pl.when(i == 0)
    def _():
        chunk_copy(e, 0).start()
        chunk_copy(e, 1).start()
        for q in range(n_chunks):
            land(e, q, slot)

    prefetch = jnp.logical_and(active, nxt >= 0)

    @pl.when(jnp.logical_and(prefetch, j == 0))
    def _():
        chunk_copy(nxt, 0).start()
        chunk_copy(nxt, 1).start()

    @pl.when(active)
    def _():
        gu = _dot(_unpack_bf16_pairs(x_ref[...]), wgu_bf[slot]) + bgu_ref[0, 0]
        de = gu.shape[1] // 2
        gate = jnp.minimum(gu[:, :de], SWIGLU_LIMIT)
        up = jnp.clip(gu[:, de:], -SWIGLU_LIMIT, SWIGLU_LIMIT)
        hdn = (up + 1.0) * (gate * _sigmoid(SWIGLU_ALPHA * gate))
        y = _dot(hdn.astype(BF16), wd_bf[slot]) + bd_ref[0, 0]
        y_ref[...] = _pack_bf16_pairs(y.astype(BF16))

    @pl.when(jnp.logical_not(active))
    def _():
        y_ref[...] = jnp.zeros_like(y_ref)

    lo = (n_chunks * j) // n
    hi = (n_chunks * (j + 1)) // n
    for q in range(n_chunks):
        @pl.when(jnp.logical_and(prefetch, jnp.logical_and(lo <= q, q < hi)))
        def _(q=q):
            land(nxt, q, 1 - slot)


def _experts(xs, sched, layer, w_gu, b_gu, w_down, b_down):
    rows_total = xs.shape[0]
    nb = rows_total // MOE_BLOCK
    depth, ne, d, n_gu = w_gu.shape
    assert xs.shape[1] * 2 == d and xs.dtype == jnp.uint32
    de = w_down.shape[2]
    assert n_gu == d and d % W_CHUNK == 0 and de % W_CHUNK == 0

    def blk(i, *s):
        return jnp.minimum(i, s[-1][0] - 1)

    def bspec(n):
        return pl.BlockSpec((1, 1, 1, n), lambda i, *s: (layer, s[0][blk(i, *s)], 0, 0))

    grid_spec = pltpu.PrefetchScalarGridSpec(
        num_scalar_prefetch=len(sched),
        grid=(nb,),
        in_specs=[
            pl.BlockSpec((MOE_BLOCK, d // 2), lambda i, *s: (blk(i, *s), 0)),
            pl.BlockSpec(memory_space=pl.ANY), bspec(n_gu),
            pl.BlockSpec(memory_space=pl.ANY), bspec(d),
        ],
        out_specs=pl.BlockSpec((MOE_BLOCK, d // 2), lambda i, *s: (i, 0)),
        scratch_shapes=[pltpu.VMEM((2, d, n_gu), BF16), pltpu.VMEM((2, de, d), BF16),
                        pltpu.VMEM((2, W_CHUNK, d), F32), pltpu.SemaphoreType.DMA((2,))],
    )
    return pl.pallas_call(
        functools.partial(_expert_kernel, layer=layer),
        grid_spec=grid_spec,
        out_shape=jax.ShapeDtypeStruct((rows_total, d // 2), jnp.uint32),
        compiler_params=_cparams("arbitrary"),
        name="moe_experts",
    )(*sched, xs, w_gu, b_gu.reshape(depth, ne, 1, n_gu), w_down, b_down.reshape(depth, ne, 1, d))


def _route(logits_t, n_tok):
    experts = jnp.arange(N_EXPERTS, dtype=jnp.int32)
    lg = logits_t
    e_iota = lax.broadcasted_iota(jnp.int32, lg.shape, 0)
    vals, idxs = [], []
    for _ in range(TOP_K):
        mx = jnp.max(lg, axis=0)
        ix = jnp.min(jnp.where(lg == mx[None, :], e_iota, N_EXPERTS), axis=0)
        vals.append(mx)
        idxs.append(ix)
        lg = jnp.where(e_iota == ix[None, :], -jnp.inf, lg)
    top_val = jnp.stack(vals)
    top_idx = jnp.stack(idxs)
    probs = jax.nn.softmax(top_val, axis=0)

    m = n_tok * TOP_K
    onehot = (top_idx[:, :, None] == experts[None, None, :]).astype(jnp.int32)
    per_tok = jnp.sum(onehot, axis=0)
    csum = jnp.cumsum(per_tok, axis=0)
    counts = csum[-1]
    pad_counts = (counts + MOE_BLOCK - 1) // MOE_BLOCK * MOE_BLOCK
    pad_end = jnp.cumsum(pad_counts)
    pad_start = pad_end - pad_counts
    base = csum - per_tok + pad_start[None, :]
    pos = jnp.sum(base[None, :, :] * onehot, axis=2)

    nb = (m + MOE_BLOCK - 1) // MOE_BLOCK + N_EXPERTS
    idx_bits = (m - 1).bit_length()
    assert (2 * N_EXPERTS + 2) << idx_bits < 2 ** 31
    assign = lax.broadcasted_iota(jnp.int32, (TOP_K, n_tok), 1) * TOP_K + \
        lax.broadcasted_iota(jnp.int32, (TOP_K, n_tok), 0)
    real_keys = (((2 * top_idx) << idx_bits) | assign).reshape(m)
    n_fill = nb * MOE_BLOCK - m
    assert n_fill % N_EXPERTS == 0 and n_fill // N_EXPERTS >= MOE_BLOCK - 1
    fill_i = lax.broadcasted_iota(jnp.int32, (N_EXPERTS, n_fill // N_EXPERTS), 1)
    fill_keys = jnp.where(fill_i < (pad_counts - counts)[:, None],
                          ((2 * experts + 1) << idx_bits)[:, None], (2 * N_EXPERTS + 1) << idx_bits)
    keys = lax.sort(jnp.concatenate([real_keys, fill_keys.reshape(n_fill)]))
    is_real = ((keys >> idx_bits) & 1) == 0
    row_tok = jnp.where(is_real, (keys & ((1 << idx_bits) - 1)) // TOP_K,
                        jnp.arange(nb * MOE_BLOCK, dtype=jnp.int32) % n_tok)

    blk_start = jnp.arange(nb, dtype=jnp.int32) * MOE_BLOCK
    blk_e = jnp.minimum(jnp.sum((pad_end[None, :] <= blk_start[:, None]).astype(jnp.int32), axis=1),
                        N_EXPERTS - 1)
    n_used = (pad_end[-1] // MOE_BLOCK).astype(jnp.int32).reshape(1)
    has_blocks = pad_counts > 0
    order_e = jnp.cumsum(has_blocks.astype(jnp.int32)) - 1
    later = jnp.logical_and(has_blocks[None, :], experts[None, :] > experts[:, None])
    next_e = jnp.min(jnp.where(later, experts[None, :], N_EXPERTS), axis=1)
    next_e = jnp.where(next_e == N_EXPERTS, -1, next_e)
    blk_is = blk_e[:, None] == experts[None, :]

    def per_block(table):
        return jnp.sum(jnp.where(blk_is, table[None, :], 0), axis=1)

    blk_j = (blk_start - per_block(pad_start)) // MOE_BLOCK
    blk_n = jnp.maximum(per_block(pad_counts) // MOE_BLOCK, 1)
    sched = (blk_e, blk_j, blk_n, per_block(order_e), per_block(next_e), n_used)
    return probs, pos, row_tok, sched


def _rows(x, idx):
    return x.at[idx].get(mode="promise_in_bounds")


def _moe(h2, tok0, logits_t, layer, w_gu, b_gu, w_down, b_down):
    n_tok = logits_t.shape[1]
    probs, pos, row_tok, sched = _route(logits_t, n_tok)
    xs = _rows(h2, row_tok + tok0)
    yb = _experts(xs, sched, layer, w_gu, b_gu, w_down, b_down)
    y = _rows(yb, pos.reshape(-1)).reshape(TOP_K, n_tok, -1)
    return y, probs.T


def _combine_kernel(x_ref, y0_ref, y1_ref, y2_ref, y3_ref, p_ref, g2_ref, nf_ref, o_ref,
                    *, tile0, final, n_ctx_tiles, tiles_per_batch, ctx_row):
    row = _mod_row(pl.program_id(0) + tile0, n_ctx_tiles, tiles_per_batch, ctx_row)
    p = p_ref[...]

    def expert_out(y_ref):
        u = y_ref[0]
        return jnp.concatenate([lax.bitcast_convert_type(u << 16, F32),
                                lax.bitcast_convert_type(u & jnp.uint32(0xFFFF0000), F32)], axis=1)

    y = (p[:, 0:1] * expert_out(y0_ref) + p[:, 1:2] * expert_out(y1_ref)) + \
        (p[:, 2:3] * expert_out(y2_ref) + p[:, 3:4] * expert_out(y3_ref))
    x = x_ref[...] + g2_ref[pl.ds(row, 1), :] * y
    if final:
        x = x * lax.rsqrt(jnp.mean(x * x, axis=-1, keepdims=True) + EPS) * nf_ref[...]
    o_ref[...] = x


def _combine(x, y, probs, mod, norm_f, *, tile0, final, n_ctx_tiles, tiles_per_batch, ctx_row):
    assert y.shape[0] == TOP_K == 4 and y.dtype == jnp.uint32
    n_rows = y.shape[1]
    d = x.shape[1]
    tm = ROW_TILE
    kern = functools.partial(_combine_kernel, tile0=tile0, final=final, n_ctx_tiles=n_ctx_tiles,
                             tiles_per_batch=tiles_per_batch, ctx_row=ctx_row)

    def yspec(k):
        return pl.BlockSpec((1, tm, d // 2), lambda i: (k, i, 0))

    return pl.pallas_call(
        kern,
        grid=(n_rows // tm,),
        in_specs=[pl.BlockSpec((tm, d), lambda i: (tile0 + i, 0)),
                  yspec(0), yspec(1), yspec(2), yspec(3),
                  pl.BlockSpec((tm, TOP_K), lambda i: (i, 0)),
                  pl.BlockSpec((8, d), lambda i: (0, 5)),
                  pl.BlockSpec((1, d), lambda i: (0, 0))],
        out_specs=pl.BlockSpec((tm, d), lambda i: (i, 0)),
        out_shape=jax.ShapeDtypeStruct((n_rows, d), F32),
        compiler_params=_cparams("parallel"),
        name="moe_combine",
    )(x, y, y, y, y, probs, mod, norm_f)


def _embed_kernel(ctx_ref, x_ref, pos_ref, o_ref, *, n_ctx_tiles):
    j = pl.program_id(0)
    b = pl.program_id(1)

    @pl.when(jnp.logical_and(j == 0, b < n_ctx_tiles))
    def _():
        o_ref[...] = ctx_ref[...]

    @pl.when(j > 0)
    def _():
        o_ref[...] = x_ref[...] + pos_ref[...]


def _embed(ctx2, x2, pos, *, batch):
    tm = ROW_TILE
    d = x2.shape[1]
    l = pos.shape[0]
    n_ctx_tiles = ctx2.shape[0] // tm
    nt = l // tm
    assert n_ctx_tiles <= batch
    return pl.pallas_call(
        functools.partial(_embed_kernel, n_ctx_tiles=n_ctx_tiles),
        grid=(1 + nt, batch),
        in_specs=[pl.BlockSpec((tm, d), lambda j, b: (jnp.minimum(b, n_ctx_tiles - 1), 0)),
                  pl.BlockSpec((tm, d), lambda j, b: (b * nt + jnp.maximum(j - 1, 0), 0)),
                  pl.BlockSpec((tm, d), lambda j, b: (jnp.maximum(j - 1, 0), 0))],
        out_specs=pl.BlockSpec(
            (tm, d), lambda j, b: (jnp.where(j == 0, jnp.minimum(b, n_ctx_tiles - 1),
                                             n_ctx_tiles + b * nt + j - 1), 0)),
        out_shape=jax.ShapeDtypeStruct((ctx2.shape[0] + x2.shape[0], d), F32),
        compiler_params=_cparams("arbitrary", "arbitrary"),
        name="embed_tokens",
    )(ctx2, x2, pos)


def _grid_pos_embed(n, d):
    rows = n // GRID_W
    quarter = d // 4
    omega = 1.0 / (10000.0 ** (jnp.arange(quarter, dtype=F32) / quarter))

    def enc(count):
        a = jnp.arange(count, dtype=F32)[:, None] * omega
        return jnp.concatenate([jnp.sin(a), jnp.cos(a)], axis=-1)

    row_part = jnp.repeat(enc(rows), GRID_W, axis=0)
    col_part = jnp.tile(enc(GRID_W), (rows, 1))
    return jnp.concatenate([row_part, col_part], axis=-1)


def kernel(x, c, ctx, c_ctx, w_mod, b_mod, norm1, w_in, w_a2, b_a, gla_norm, w_s, b_s, p_gla, p_sg, p_ft,
           w_out, norm2, w_router, b_router, w_gu, b_gu, w_down, b_down, norm_f):
    batch, l, d = x.shape
    lc = ctx.shape[1]
    depth = w_mod.shape[0]
    n_ctx_rows = batch * lc
    t_rows = n_ctx_rows + batch * l
    assert lc % ROW_TILE == 0 and l % ROW_TILE == 0 and batch < 8

    tm1 = math.gcd(math.gcd(n_ctx_rows, l), 1024)
    assert tm1 % ROW_TILE == 0
    proj_tiles = dict(n_ctx_tiles=n_ctx_rows // tm1, tiles_per_batch=l // tm1, ctx_row=batch)
    row_tiles = dict(n_ctx_tiles=n_ctx_rows // ROW_TILE, tiles_per_batch=l // ROW_TILE, ctx_row=batch)
    tm2 = ROW_TILE
    tm3 = math.gcd(tm1, 2 * ROW_TILE)
    big_tiles = dict(n_ctx_tiles=n_ctx_rows // tm3, tiles_per_batch=l // tm3, ctx_row=batch)

    o_q = 0
    o_k = o_q + W_Q
    o_v = o_k + W_Q
    o_r = o_v + W_V
    o_a = o_r + W_V
    o_su = o_a + 2 * GLA_RANK
    o_sv = o_su + W_SG
    o_f = o_sv + W_SG
    o_g = o_f + W_FT
    n_gates = 3 * d
    cols = dict(gates=0, q=n_gates, r=n_gates + 2 * W_Q + W_V, su=n_gates + 2 * W_Q + 2 * W_V)
    cols["sv"] = cols["su"] + W_SG
    cols["f"] = cols["sv"] + W_SG
    n_main = cols["f"] + W_FT

    xt = _embed(ctx.reshape(n_ctx_rows, d), x.reshape(batch * l, d), _grid_pos_embed(l, d), batch=batch)

    cond = jnp.zeros((8, d), F32).at[:batch].set(c).at[batch].set(c_ctx)
    mods = _mod_vectors(cond, w_mod, b_mod)

    cc, sc = _dft_tables(FT_CH)
    chan_tab = jnp.concatenate([cc, sc], axis=1)
    assert l % (FT_RADIX * FT_TILE) == 0
    ft_tables = _ft_long_tables(l)
    cos_c, sin_c = _dft_tables(lc)

    w_main = jnp.concatenate(
        [w_in[:, :, o_g:], w_in[:, :, o_q:o_a], w_in[:, :, o_su:o_g]], axis=2).astype(BF16)
    w_alr = jnp.pad(w_in[:, :, o_a:o_su], ((0, 0), (0, 0), (0, LANES - 2 * GLA_RANK))).astype(BF16)

    for i in range(depth):
        last = i == depth - 1
        w_a = w_alr[i]
        wa2 = jnp.zeros((LANES, 2 * W_Q), F32)
        wa2 = wa2.at[:GLA_RANK, :W_Q].set(w_a2[i, 0]).at[GLA_RANK:2 * GLA_RANK, W_Q:].set(w_a2[i, 1])
        mod = mods[i]

        z, la = _project(xt, mod, norm1[i][None], w_main, w_a, wa2.astype(BF16), b_a[i].reshape(1, 2 * W_Q),
                         layer=i, tm=tm1, tn=1536, **proj_tiles)
        assert z.shape[1] == n_main

        o_fw, o_bw = _gla(z, la, batch=batch, lc=lc, l=l, col_q=cols["q"])

        xc_c, xs_c = _ft_channel(z, chan_tab, col_f=cols["f"], row0=0, n_rows=n_ctx_rows)
        xc_l, xs_l = _ft_channel(z, chan_tab, col_f=cols["f"], row0=n_ctx_rows, n_rows=batch * l)
        y_ft = (_ft_positions(cos_c, sin_c, xc_c, xs_c, batch=batch, seq=lc),
                _ft_positions_long(ft_tables, xc_l, xs_l, batch=batch, seq=l))

        bs_b = jnp.broadcast_to(b_s[i][:, :, None], (SG_GROUPS, SG_CHUNK, SG_CH))
        merged = _branches(z, o_fw, o_bw, y_ft, gla_norm[i][None], w_s[i].astype(BF16), bs_b,
                           p_gla[i].astype(BF16), p_sg[i].astype(BF16), p_ft[i].astype(BF16), cols=cols, tm=tm2)

        w_r = jnp.pad(w_router[i], ((0, 0), (0, LANES - N_EXPERTS)))
        b_r = jnp.pad(b_router[i], (0, LANES - N_EXPERTS)).reshape(1, LANES)
        xt, h2, logits_t = _out_project(merged, xt, w_out[i].astype(BF16), mod, norm2[i][None], w_r, b_r,
                                      tm=tm3, **big_tiles)

        row0 = n_ctx_rows if last else 0
        y, probs = _moe(h2, row0, logits_t[:N_EXPERTS, row0:], i, w_gu, b_gu, w_down, b_down)
        xt = _combine(xt, y, probs, mod, norm_f[None], tile0=row0 // ROW_TILE, final=last, **row_tiles)

    return xt.reshape(batch, l, d)
```

```python
import functools
import math

import jax
import jax.numpy as jnp
from jax import lax
from jax.experimental import pallas as pl
from jax.experimental.pallas import tpu as pltpu

F32 = jnp.float32
BF16 = jnp.bfloat16

EPS = 1e-6
GRID_W = 64
GLA_HEADS = 4
GLA_DK = 128
GLA_DV = 256
GLA_RANK = 16
GLA_TAU = 16.0
GLA_CHUNK = 64
GLA_SUB = 8
GLA_STAGES = 3
SG_GROUPS = 4
SG_CH = 128
SG_CHUNK = 128
FT_GROUPS = 4
FT_CH = 128
N_EXPERTS = 32
TOP_K = 4
SWIGLU_LIMIT = 7.0
SWIGLU_ALPHA = 1.702
MOE_BLOCK = 256
W_CHUNK = 512

LOG2E = math.log2(math.e)
LANES = 128
ROW_TILE = 256
OUT_SUB_ROWS = 256
VMEM_LIMIT = 56 * 1024 * 1024

W_Q = GLA_HEADS * GLA_DK
W_V = GLA_HEADS * GLA_DV
W_SG = SG_GROUPS * SG_CH
W_FT = FT_GROUPS * FT_CH


def _cparams(*sem):
    return pltpu.CompilerParams(dimension_semantics=sem, vmem_limit_bytes=VMEM_LIMIT)


def _sigmoid(x):
    return 1.0 / (1.0 + jnp.exp(-x))


def _gelu_tanh(x):
    return 0.5 * x * (1.0 + jnp.tanh(math.sqrt(2.0 / math.pi) * (x + 0.044715 * (x * x * x))))


def _dot(a, b):
    return jnp.dot(a, b, preferred_element_type=F32)


def _pack_bf16_pairs(x):
    bits = lax.bitcast_convert_type(x.astype(F32), jnp.uint32)
    half = x.shape[1] // 2
    return (bits[:, :half] >> 16) | (bits[:, half:] & jnp.uint32(0xFFFF0000))


def _unpack_bf16_pairs(u):
    lo = lax.bitcast_convert_type(u << 16, F32)
    hi = lax.bitcast_convert_type(u & jnp.uint32(0xFFFF0000), F32)
    return jnp.concatenate([lo, hi], axis=1).astype(BF16)


def _mod_row(tile, n_ctx_tiles, tiles_per_batch, ctx_row):
    return jnp.where(tile < n_ctx_tiles, ctx_row, (tile - n_ctx_tiles) // tiles_per_batch)


def _mod_kernel(c_ref, w_ref, b_ref, o_ref):
    c = c_ref[...]
    s = (c * _sigmoid(c)).astype(BF16)
    o_ref[0] = _dot(s, w_ref[0].astype(BF16)) + b_ref[0]


def _mod_vectors(cond, w_mod, b_mod):
    depth, d, n = w_mod.shape
    tn = 1024
    return pl.pallas_call(
        _mod_kernel,
        grid=(depth, n // tn),
        in_specs=[
            pl.BlockSpec((8, d), lambda l, j: (0, 0)),
            pl.BlockSpec((1, d, tn), lambda l, j: (l, 0, j)),
            pl.BlockSpec((1, 1, tn), lambda l, j: (l, 0, j)),
        ],
        out_specs=pl.BlockSpec((1, 8, tn), lambda l, j: (l, 0, j)),
        out_shape=jax.ShapeDtypeStruct((depth, 8, n), F32),
        compiler_params=_cparams("parallel", "parallel"),
        name="mod_vectors",
    )(cond, w_mod, b_mod.reshape(depth, 1, n))


def _proj_kernel(x_ref, sh_ref, sc_ref, g_ref, w_ref, wa_ref, wa2_ref, ba_ref,
                 z_ref, la_ref, a_scr, *, n_ctx_tiles, tiles_per_batch, ctx_row):
    i = pl.program_id(0)
    j = pl.program_id(1)

    @pl.when(j == 0)
    def _():
        row = _mod_row(i, n_ctx_tiles, tiles_per_batch, ctx_row)
        scale = 1.0 + sc_ref[pl.ds(row, 1), :]
        shift = sh_ref[pl.ds(row, 1), :]

        def body(r, carry):
            rows = pl.ds(pl.multiple_of(r * ROW_TILE, ROW_TILE), ROW_TILE)
            x = x_ref[rows, :]
            y = x * lax.rsqrt(jnp.mean(x * x, axis=-1, keepdims=True) + EPS) * g_ref[...]
            hb = (y * scale + shift).astype(BF16)
            a_scr[rows, :] = hb
            a_lr = _dot(hb, wa_ref[...])
            pre = _dot(a_lr.astype(BF16), wa2_ref[...]) + ba_ref[...]
            la_ref[rows, :] = -(jnp.maximum(-pre, 0.0) + jnp.log1p(jnp.exp(-jnp.abs(pre)))) * (1.0 / GLA_TAU)
            return carry

        lax.fori_loop(0, x_ref.shape[0] // ROW_TILE, body, 0)

    z_ref[...] = _dot(a_scr[...], w_ref[0]).astype(BF16)


def _project(x, mod, g, w_main, w_a, w_a2, b_a, *, layer, tm, tn, n_ctx_tiles, tiles_per_batch, ctx_row):
    t, d = x.shape
    n = w_main.shape[2]
    kern = functools.partial(_proj_kernel, n_ctx_tiles=n_ctx_tiles,
                             tiles_per_batch=tiles_per_batch, ctx_row=ctx_row)
    return pl.pallas_call(
        kern,
        grid=(t // tm, n // tn),
        in_specs=[
            pl.BlockSpec((tm, d), lambda i, j: (i, 0)),
            pl.BlockSpec((8, d), lambda i, j: (0, 0)),
            pl.BlockSpec((8, d), lambda i, j: (0, 1)),
            pl.BlockSpec((1, d), lambda i, j: (0, 0)),
            pl.BlockSpec((1, d, tn), lambda i, j: (layer, 0, j)),
            pl.BlockSpec((d, LANES), lambda i, j: (0, 0)),
            pl.BlockSpec((LANES, 2 * W_Q), lambda i, j: (0, 0)),
            pl.BlockSpec((1, 2 * W_Q), lambda i, j: (0, 0)),
        ],
        out_specs=[
            pl.BlockSpec((tm, tn), lambda i, j: (i, j)),
            pl.BlockSpec((tm, 2 * W_Q), lambda i, j: (i, 0)),
        ],
        out_shape=[
            jax.ShapeDtypeStruct((t, n), BF16),
            jax.ShapeDtypeStruct((t, 2 * W_Q), F32),
        ],
        scratch_shapes=[pltpu.VMEM((tm, d), BF16)],
        compiler_params=_cparams("parallel", "arbitrary"),
        name="norm_mod_project",
    )(x, mod, mod, g, w_main, w_a, w_a2, b_a)


def _gla_pairs(rev):
    c = GLA_CHUNK
    pairs = []
    s = c // 2
    while s >= GLA_SUB:
        for b0 in range(0, c, 2 * s):
            if rev:
                pairs.append((b0, b0 + s, b0 + s, b0 + 2 * s, b0 + s - 1))
            else:
                pairs.append((b0 + s, b0 + 2 * s, b0, b0 + s, b0 + s))
        s //= 2
    return tuple(pairs)


def _place_rows(x, r0, n):
    parts = []
    if r0:
        parts.append(jnp.zeros((r0, x.shape[1]), x.dtype))
    parts.append(x)
    rest = n - r0 - x.shape[0]
    if rest:
        parts.append(jnp.zeros((rest, x.shape[1]), x.dtype))
    return jnp.concatenate(parts, axis=0)


def _gla_chunk(q_ref, k_ref, v_ref, la_ref, e_ref, o_ref, s_ref, cum_ref, kc_ref, r0, h, rev):
    c, sub = GLA_CHUNK, GLA_SUB
    nsub = c // sub
    rows = pl.ds(r0, c)
    kcols = slice(h * GLA_DK, (h + 1) * GLA_DK)
    vcols = slice(h * GLA_DV, (h + 1) * GLA_DV)
    q = q_ref[rows, kcols].astype(F32) * (GLA_DK ** -0.5)
    k = k_ref[rows, kcols].astype(F32)
    v = v_ref[rows, vcols]
    la = la_ref[rows, kcols]

    ri = lax.broadcasted_iota(jnp.int32, (c, c), 0)
    ci = lax.broadcasted_iota(jnp.int32, (c, c), 1)
    before = (ci >= ri) if rev else (ci <= ri)
    tri = jnp.where(before, 1.0, 0.0).astype(BF16)
    la_hi = la.astype(BF16)
    rem = la - la_hi.astype(F32)
    la_mid = rem.astype(BF16)
    la_lo = (rem - la_mid.astype(F32)).astype(BF16)
    cum = (_dot(tri, la_hi) + _dot(tri, la_mid) + _dot(tri, la_lo)) * LOG2E
    cum_ref[...] = cum
    kc_ref[...] = k
    yield
    last = 0 if rev else c - 1
    total = cum[last:last + 1, :]

    s = s_ref[...]
    o = _dot((q * jnp.exp2(cum)).astype(BF16), s.astype(BF16))

    qparts, kparts = [], []
    for l0, l1, e0, e1, ref in _gla_pairs(rev):
        cr = cum[ref:ref + 1, :]
        qparts.append(_place_rows(q[l0:l1] * jnp.exp2(cum[l0:l1] - cr), l0, c))
        kparts.append(_place_rows(k[e0:e1] * jnp.exp2(cr - cum[e0:e1]), e0, c))
    s_off = lax.dot_general(jnp.concatenate(qparts, axis=1).astype(BF16),
                            jnp.concatenate(kparts, axis=1).astype(BF16),
                            (((1,), (1,)), ((), ())), preferred_element_type=F32)

    hs = sub // 2
    skip_halves = hs % 8 == 0

    def halves(x, which):
        if len(which) == 2:
            return x
        return jnp.concatenate([x[a * sub + w * hs:a * sub + (w + 1) * hs] for a in range(nsub) for w in which],
                               axis=0)

    parts = []
    for jj in range(sub):
        if not skip_halves:
            which = (0, 1)
        elif rev:
            which = (0, 1) if jj >= hs else (0,)
        else:
            which = (0, 1) if jj < hs else (1,)
        nrow = hs * len(which)
        cj = jnp.concatenate(
            [jnp.broadcast_to(cum_ref[a * sub + jj:a * sub + jj + 1, :], (nrow, GLA_DK)) for a in range(nsub)],
            axis=0)
        kj = jnp.concatenate(
            [jnp.broadcast_to(kc_ref[a * sub + jj:a * sub + jj + 1, :], (nrow, GLA_DK)) for a in range(nsub)],
            axis=0)
        prod = halves(q, which) * jnp.exp2(jnp.minimum(halves(cum, which) - cj, 0.0)) * kj
        if len(which) == 1:
            zero = jnp.zeros((hs, GLA_DK), F32)
            pieces = []
            for a in range(nsub):
                live = prod[a * hs:(a + 1) * hs]
                pieces += [live, zero] if which[0] == 0 else [zero, live]
            prod = jnp.concatenate(pieces, axis=0)
        parts.append(prod.astype(BF16))
    diag = _dot(jnp.concatenate(parts, axis=1), e_ref[...])
    yield

    lag = (ci - ri) if rev else (ri - ci)
    room = (sub - 1 - ri % sub) if rev else (ri % sub)
    in_diag = lag.astype(jnp.uint32) <= room.astype(jnp.uint32)
    p = jnp.where(in_diag, diag, 0.0) + s_off
    o_ref[rows, vcols] = o + _dot(p.astype(BF16), v)

    k_out = (k * jnp.exp2(total - cum)).astype(BF16)
    upd = lax.dot_general(k_out, v, (((0,), (0,)), ((), ())), preferred_element_type=F32)
    dmat = jnp.broadcast_to(jnp.exp2(total), (GLA_DK, GLA_DK)).T
    for half in range(GLA_DV // GLA_DK):
        cols = slice(half * GLA_DK, (half + 1) * GLA_DK)
        s_ref[:, cols] = s[:, cols] * dmat + upd[:, cols]


def _gla_kernel(qf, kf, vf, laf, qb, kb, vb, lab, e_ref, of_ref, ob_ref, *scratch):
    n = 2 * GLA_HEADS
    states, cums, keys = scratch[:n], scratch[n:2 * n], scratch[2 * n:]

    @pl.when(pl.program_id(1) == 0)
    def _():
        for s_ref in states:
            s_ref[...] = jnp.zeros_like(s_ref)

    nchunk = ROW_TILE // GLA_CHUNK

    def body(c, carry):
        r_f = pl.multiple_of(c * GLA_CHUNK, GLA_CHUNK)
        r_b = pl.multiple_of((nchunk - 1 - c) * GLA_CHUNK, GLA_CHUNK)
        units = []
        for h in range(GLA_HEADS):
            f, b = h, GLA_HEADS + h
            units.append(_gla_chunk(qf, kf, vf, laf, e_ref, of_ref, states[f], cums[f], keys[f], r_f, h, False))
            units.append(_gla_chunk(qb, kb, vb, lab, e_ref, ob_ref, states[b], cums[b], keys[b], r_b, h, True))
        for _ in range(GLA_STAGES):
            for u in units:
                next(u, None)
        return carry

    lax.fori_loop(0, nchunk, body, 0)


def _gla(z, la, *, batch, lc, l, col_q):
    t_rows = z.shape[0]
    nctx = lc // ROW_TILE
    nlat = l // ROW_TILE
    steps = nctx + nlat
    qb0 = col_q // W_Q
    kb0 = (col_q + W_Q) // W_Q
    vb0 = (col_q + 2 * W_Q) // W_V

    def fwd_tile(b, t):
        return jnp.where(t < nctx, b * nctx + t, batch * nctx + b * nlat + (t - nctx))

    def bwd_tile(b, t):
        return jnp.where(t < nctx, b * nctx + (nctx - 1 - t), batch * nctx + b * nlat + (steps - 1 - t))

    def spec(width, tile_fn, col):
        return pl.BlockSpec((ROW_TILE, width), lambda b, t: (tile_fn(b, t), col))

    kk = jnp.arange(GLA_SUB * GLA_DK, dtype=jnp.int32)[:, None] // GLA_DK
    cc = jnp.arange(GLA_CHUNK, dtype=jnp.int32)[None, :] % GLA_SUB
    e_mat = (kk == cc).astype(BF16)

    o_shape = jax.ShapeDtypeStruct((t_rows, W_V), F32)
    n_units = 2 * GLA_HEADS
    state = pltpu.VMEM((GLA_DK, GLA_DV), F32)
    chunk = pltpu.VMEM((GLA_CHUNK, GLA_DK), F32)
    return pl.pallas_call(
        _gla_kernel,
        grid=(batch, steps),
        in_specs=[
            spec(W_Q, fwd_tile, qb0), spec(W_Q, fwd_tile, kb0), spec(W_V, fwd_tile, vb0), spec(W_Q, fwd_tile, 0),
            spec(W_Q, bwd_tile, qb0), spec(W_Q, bwd_tile, kb0), spec(W_V, bwd_tile, vb0), spec(W_Q, bwd_tile, 1),
            pl.BlockSpec((GLA_SUB * GLA_DK, GLA_CHUNK), lambda b, t: (0, 0)),
        ],
        out_specs=[spec(W_V, fwd_tile, 0), spec(W_V, bwd_tile, 0)],
        out_shape=[o_shape, o_shape],
        scratch_shapes=[state] * n_units + [chunk] * (2 * n_units),
        compiler_params=_cparams("parallel", "arbitrary"),
        name="gla_scan",
    )(z, z, z, la, z, z, z, la, e_mat)


def _dft_tables(n):
    scale = n ** -0.5
    p = jnp.arange(n, dtype=jnp.int32)[:, None]

    def trig(cols, period):
        ang = ((p * cols[None, :]) % period).astype(F32) * (2.0 * math.pi / period)
        return jnp.cos(ang), jnp.sin(ang)

    n2 = 1
    while n2 * n2 < n:
        n2 *= 2
    if n % n2 or n2 == 1:
        c, s = trig(jnp.arange(n, dtype=jnp.int32), n)
        return (c * scale).astype(BF16), (s * scale).astype(BF16)
    n1 = n // n2
    ca, sa = trig(jnp.arange(n1, dtype=jnp.int32), n1)
    cb, sb = trig(jnp.arange(n2, dtype=jnp.int32), n)
    ca, sa, cb, sb = ca[:, :, None], sa[:, :, None], cb[:, None, :] * scale, sb[:, None, :] * scale
    return ((ca * cb - sa * sb).reshape(n, n).astype(BF16), (sa * cb + ca * sb).reshape(n, n).astype(BF16))


def _ft_chan_kernel(f_ref, cs_ref, xc_ref, xs_ref):
    for g in range(FT_GROUPS):
        cols = slice(g * FT_CH, (g + 1) * FT_CH)
        y = _dot(f_ref[:, cols], cs_ref[...])
        xc_ref[:, cols] = y[:, :FT_CH].astype(BF16)
        xs_ref[:, cols] = y[:, FT_CH:].astype(BF16)


def _ft_channel(z, cs, *, col_f, row0, n_rows):
    tm = math.gcd(math.gcd(row0, n_rows), 1024)
    shape = jax.ShapeDtypeStruct((n_rows, W_FT), BF16)
    return pl.pallas_call(
        _ft_chan_kernel,
        grid=(n_rows // tm,),
        in_specs=[pl.BlockSpec((tm, W_FT), lambda i: (row0 // tm + i, col_f // W_FT)),
                  pl.BlockSpec((FT_CH, 2 * FT_CH), lambda i: (0, 0))],
        out_specs=[pl.BlockSpec((tm, W_FT), lambda i: (i, 0))] * 2,
        out_shape=[shape, shape],
        compiler_params=_cparams("parallel"),
        name="fourier_channels",
    )(z, cs)


def _ft_pos_kernel(c_ref, s_ref, xc_ref, xs_ref, o_ref):
    o_ref[...] = _dot(c_ref[...], xc_ref[...]) - _dot(s_ref[...], xs_ref[...])


def _ft_positions(cos_t, sin_t, xc, xs, *, batch, seq):
    nt = seq // ROW_TILE
    x_spec = pl.BlockSpec((seq, W_FT), lambda b, j: (b, 0))
    t_spec = pl.BlockSpec((ROW_TILE, seq), lambda b, j: (j, 0))
    return pl.pallas_call(
        _ft_pos_kernel,
        grid=(batch, nt),
        in_specs=[t_spec, t_spec, x_spec, x_spec],
        out_specs=pl.BlockSpec((ROW_TILE, W_FT), lambda b, j: (b * nt + j, 0)),
        out_shape=jax.ShapeDtypeStruct((batch * seq, W_FT), F32),
        compiler_params=_cparams("parallel", "parallel"),
        name="fourier_positions",
    )(cos_t, sin_t, xc, xs)


FT_RADIX = 16
FT_TILE = 16


def _ft_stage1_kernel(xc_ref, xs_ref, k_ref, tc_ref, ts_ref, o_ref):
    rows = FT_RADIX * FT_TILE
    zin = jnp.concatenate([xc_ref[0].reshape(rows, W_FT), xs_ref[0].reshape(rows, W_FT)], axis=0)
    a = _dot(k_ref[...], zin)
    re, im = a[:rows], a[rows:]
    tc, ts = tc_ref[0], ts_ref[0]
    o_ref[0, 0] = (re * tc + im * ts).astype(BF16).reshape(FT_RADIX, FT_TILE, W_FT)
    o_ref[0, 1] = (im * tc - re * ts).astype(BF16).reshape(FT_RADIX, FT_TILE, W_FT)


def _ft_stage2_kernel(b_ref, t_ref, o_ref):
    rhs = jnp.concatenate([b_ref[0, 0, 0], b_ref[0, 1, 0]], axis=0)
    o_ref[0, 0] = _dot(t_ref[...], rhs)


def _ft_long_tables(seq):
    r, n1 = FT_RADIX, seq // FT_RADIX

    def trig(rows, cols, period):
        ang = ((rows[:, None] * cols[None, :]) % period).astype(F32) * (2.0 * math.pi / period)
        return jnp.cos(ang), jnp.sin(ang)

    ir = jnp.arange(r, dtype=jnp.int32)
    i1 = jnp.arange(n1, dtype=jnp.int32)
    c1, s1 = trig(ir, ir, r)
    t1 = jnp.concatenate([jnp.concatenate([c1, -s1], axis=1), jnp.concatenate([-s1, -c1], axis=1)], axis=0)
    k1 = jnp.kron(t1 * r ** -0.5, jnp.eye(FT_TILE, dtype=F32)).astype(BF16)
    tc, ts = trig(ir, i1, seq)

    def per_tile(t):
        return t.reshape(r, n1 // FT_TILE, FT_TILE).transpose(1, 0, 2).reshape(n1 // FT_TILE, r * FT_TILE, 1)

    c2, s2 = trig(i1, i1, n1)
    t2 = (jnp.concatenate([c2, s2], axis=1) * n1 ** -0.5).astype(BF16)
    return k1, per_tile(tc), per_tile(ts), t2


def _ft_positions_long(tables, xc, xs, *, batch, seq):
    k1, tc, ts, t2 = tables
    r, n1 = FT_RADIX, seq // FT_RADIX
    nt = n1 // FT_TILE
    rows = r * FT_TILE
    x4 = (batch, r, n1, W_FT)
    x_spec = pl.BlockSpec((1, r, FT_TILE, W_FT), lambda b, t: (b, 0, t, 0))
    tw_spec = pl.BlockSpec((1, rows, 1), lambda b, t: (t, 0, 0))
    stage1 = pl.pallas_call(
        _ft_stage1_kernel,
        grid=(batch, nt),
        in_specs=[x_spec, x_spec, pl.BlockSpec((2 * rows, 2 * rows), lambda b, t: (0, 0)), tw_spec, tw_spec],
        out_specs=pl.BlockSpec((1, 2, r, FT_TILE, W_FT), lambda b, t: (b, 0, 0, t, 0)),
        out_shape=jax.ShapeDtypeStruct((batch, 2, r, n1, W_FT), BF16),
        compiler_params=_cparams("parallel", "parallel"),
        name="fourier_stage1",
    )(xc.reshape(x4), xs.reshape(x4), k1, tc, ts)
    stage2 = pl.pallas_call(
        _ft_stage2_kernel,
        grid=(batch, r),
        in_specs=[pl.BlockSpec((1, 2, 1, n1, W_FT), lambda b, p: (b, 0, p, 0, 0)),
                  pl.BlockSpec((n1, 2 * n1), lambda b, p: (0, 0))],
        out_specs=pl.BlockSpec((1, 1, n1, W_FT), lambda b, p: (b, p, 0, 0)),
        out_shape=jax.ShapeDtypeStruct((batch, r, n1, W_FT), F32),
        compiler_params=_cparams("parallel", "parallel"),
        name="fourier_stage2",
    )(stage1, t2)
    return stage2.transpose(0, 2, 1, 3).reshape(batch * seq, W_FT)


def _branch_kernel(gg_ref, gs_ref, gf_ref, of_ref, ob_ref, r_ref, su_ref, sv_ref, yc_ref, yl_ref,
                   gn_ref, ws_ref, bs_ref, pg_ref, ps_ref, pf_ref, m_ref, *, n_ctx_tiles):
    o = of_ref[...] + ob_ref[...]
    heads = []
    for h in range(GLA_HEADS):
        oh = o[:, h * GLA_DV:(h + 1) * GLA_DV]
        heads.append(oh * lax.rsqrt(jnp.mean(oh * oh, axis=-1, keepdims=True) + EPS))
    r = r_ref[...].astype(F32)
    y_gla = (jnp.concatenate(heads, axis=-1) * gn_ref[...]) * (r * _sigmoid(r))
    acc = _sigmoid(gg_ref[...].astype(F32)) * _dot(y_gla.astype(BF16), pg_ref[...])

    u = _gelu_tanh(su_ref[...].astype(F32))
    v = _gelu_tanh(sv_ref[...].astype(F32))
    groups = []
    for g in range(SG_GROUPS):
        vg = v[:, g * SG_CH:(g + 1) * SG_CH]
        mu = jnp.mean(vg, axis=-1, keepdims=True)
        var = jnp.mean(jnp.square(vg - mu), axis=-1, keepdims=True)
        vn = ((vg - mu) * lax.rsqrt(var + EPS)).astype(BF16)
        chunks = []
        for c in range(v.shape[0] // SG_CHUNK):
            chunks.append(_dot(ws_ref[g], vn[c * SG_CHUNK:(c + 1) * SG_CHUNK, :]) + bs_ref[g])
        groups.append(jnp.concatenate(chunks, axis=0))
    y_sg = u * jnp.concatenate(groups, axis=-1)
    acc = acc + _sigmoid(gs_ref[...].astype(F32)) * _dot(y_sg.astype(BF16), ps_ref[...])

    y_ft = jnp.where(pl.program_id(0) < n_ctx_tiles, yc_ref[...], yl_ref[...])
    acc = acc + _sigmoid(gf_ref[...].astype(F32)) * _dot(y_ft.astype(BF16), pf_ref[...])
    m_ref[...] = acc.astype(BF16)


def _branches(z, o_f, o_b, y_ft, gla_norm, w_s, b_s, p_gla, p_sg, p_ft, *, cols, tm):
    t_rows = z.shape[0]
    d = p_gla.shape[1]
    n_ctx_tiles = y_ft[0].shape[0] // tm
    kern = functools.partial(_branch_kernel, n_ctx_tiles=n_ctx_tiles)

    def zspec(width, col0):
        return pl.BlockSpec((tm, width), lambda i: (i, col0 // width))

    def rows(width):
        return pl.BlockSpec((tm, width), lambda i: (i, 0))

    def const(shape):
        return pl.BlockSpec(shape, lambda i: (0,) * len(shape))

    return pl.pallas_call(
        kern,
        grid=(t_rows // tm,),
        in_specs=[
            zspec(d, cols["gates"]), zspec(d, cols["gates"] + d), zspec(d, cols["gates"] + 2 * d),
            rows(W_V), rows(W_V), zspec(W_V, cols["r"]), zspec(W_SG, cols["su"]), zspec(W_SG, cols["sv"]),
            pl.BlockSpec((tm, W_FT), lambda i: (jnp.minimum(i, n_ctx_tiles - 1), 0)),
            pl.BlockSpec((tm, W_FT), lambda i: (jnp.maximum(i - n_ctx_tiles, 0), 0)),
            const((1, W_V)), const((SG_GROUPS, SG_CHUNK, SG_CHUNK)), const((SG_GROUPS, SG_CHUNK, SG_CH)),
            const((W_V, d)), const((W_SG, d)), const((W_FT, d)),
        ],
        out_specs=rows(d),
        out_shape=jax.ShapeDtypeStruct((t_rows, d), BF16),
        compiler_params=_cparams("parallel"),
        name="branch_merge",
    )(z, z, z, o_f, o_b, z, z, z, y_ft[0], y_ft[1], gla_norm, w_s, b_s, p_gla, p_sg, p_ft)


def _outproj_kernel(m_ref, x_ref, w_ref, g1_ref, n2_ref, sh_ref, sc_ref, wr_ref, br_ref,
                    xo_ref, h_ref, lg_ref, *, n_ctx_tiles, tiles_per_batch, ctx_row):
    row = _mod_row(pl.program_id(0), n_ctx_tiles, tiles_per_batch, ctx_row)
    gate1 = g1_ref[pl.ds(row, 1), :]
    scale2 = 1.0 + sc_ref[pl.ds(row, 1), :]
    shift2 = sh_ref[pl.ds(row, 1), :]
    w = wr_ref[...]
    w_hi = w.astype(BF16)
    w_lo = (w - w_hi.astype(F32)).astype(BF16)
    w_both = jnp.concatenate([w_hi, w_lo], axis=0)
    contract_last = (((1,), (1,)), ((), ()))
    sub = OUT_SUB_ROWS
    starts = list(range(0, x_ref.shape[0], sub))
    proj = {starts[0]: _dot(m_ref[0:sub, :], w_ref[...])}
    for k, r0 in enumerate(starts):
        rows = slice(r0, r0 + sub)
        if k + 1 < len(starts):
            nxt = starts[k + 1]
            proj[nxt] = _dot(m_ref[nxt:nxt + sub, :], w_ref[...])
        x = x_ref[rows, :] + gate1 * proj.pop(r0)
        xo_ref[rows, :] = x
        y = x * lax.rsqrt(jnp.mean(x * x, axis=-1, keepdims=True) + EPS) * n2_ref[...]
        h = y * scale2 + shift2
        h_hi = h.astype(BF16)
        h_ref[rows, :] = _pack_bf16_pairs(h_hi)
        h_lo = (h - h_hi.astype(F32)).astype(BF16)
        hi_part = lax.dot_general(w_both, h_hi, contract_last, preferred_element_type=F32)
        lo_part = lax.dot_general(w_hi, h_lo, contract_last, preferred_element_type=F32)
        lg_ref[:, rows] = hi_part[:LANES] + (hi_part[LANES:] + lo_part) + br_ref[...]


def _out_project(merged, x, w_out, mod, norm2, w_router, b_router, *, tm, n_ctx_tiles, tiles_per_batch, ctx_row):
    t_rows, d = x.shape
    kern = functools.partial(_outproj_kernel, n_ctx_tiles=n_ctx_tiles,
                             tiles_per_batch=tiles_per_batch, ctx_row=ctx_row)

    def rows(width):
        return pl.BlockSpec((tm, width), lambda i: (i, 0))

    def modspec(k):
        return pl.BlockSpec((8, d), lambda i: (0, k))

    return pl.pallas_call(
        kern,
        grid=(t_rows // tm,),
        in_specs=[rows(d), rows(d), pl.BlockSpec((d, d), lambda i: (0, 0)),
                  modspec(2), pl.BlockSpec((1, d), lambda i: (0, 0)), modspec(3), modspec(4),
                  pl.BlockSpec((LANES, d), lambda i: (0, 0)), pl.BlockSpec((LANES, 1), lambda i: (0, 0))],
        out_specs=[rows(d), rows(d // 2), pl.BlockSpec((LANES, tm), lambda i: (0, i))],
        out_shape=[jax.ShapeDtypeStruct((t_rows, d), F32), jax.ShapeDtypeStruct((t_rows, d // 2), jnp.uint32),
                   jax.ShapeDtypeStruct((LANES, t_rows), F32)],
        compiler_params=_cparams("parallel"),
        name="out_project",
    )(merged, x, w_out, mod, norm2, mod, mod, w_router, b_router)


def _expert_kernel(be_ref, jb_ref, nblk_ref, ord_ref, nxt_ref, nu_ref,
                   x_ref, wgu_hbm, bgu_ref, wd_hbm, bd_ref, y_ref,
                   wgu_bf, wd_bf, stage, sems, *, layer):
    i = pl.program_id(0)
    n_gu_chunks = wgu_bf.shape[1] // W_CHUNK
    n_chunks = n_gu_chunks + wd_bf.shape[1] // W_CHUNK

    def chunk_copy(e, q):
        if q < n_gu_chunks:
            src = wgu_hbm.at[layer, e, pl.ds(q * W_CHUNK, W_CHUNK), :]
        else:
            src = wd_hbm.at[layer, e, pl.ds((q - n_gu_chunks) * W_CHUNK, W_CHUNK), :]
        return pltpu.make_async_copy(src, stage.at[q % 2], sems.at[q % 2])

    def land(e, q, slot):
        chunk_copy(e, q).wait()
        w = stage[q % 2].astype(BF16)
        if q < n_gu_chunks:
            wgu_bf[slot, pl.ds(q * W_CHUNK, W_CHUNK), :] = w
        else:
            wd_bf[slot, pl.ds((q - n_gu_chunks) * W_CHUNK, W_CHUNK), :] = w
        if q + 2 < n_chunks:
            chunk_copy(e, q + 2).start()

    active = i < nu_ref[0]
    e = be_ref[i]
    j = jb_ref[i]
    n = nblk_ref[i]
    slot = ord_ref[i] % 2
    nxt = nxt_ref[i]

    @pl.when(i == 0)
    def _():
        chunk_copy(e, 0).start()
        chunk_copy(e, 1).start()
        for q in range(n_chunks):
            land(e, q, slot)

    prefetch = jnp.logical_and(active, nxt >= 0)

    @pl.when(jnp.logical_and(prefetch, j == 0))
    def _():
        chunk_copy(nxt, 0).start()
        chunk_copy(nxt, 1).start()

    @pl.when(active)
    def _():
        gu = _dot(_unpack_bf16_pairs(x_ref[...]), wgu_bf[slot]) + bgu_ref[0, 0]
        de = gu.shape[1] // 2
        gate = jnp.minimum(gu[:, :de], SWIGLU_LIMIT)
        up = jnp.clip(gu[:, de:], -SWIGLU_LIMIT, SWIGLU_LIMIT)
        hdn = (up + 1.0) * (gate * _sigmoid(SWIGLU_ALPHA * gate))
        y = _dot(hdn.astype(BF16), wd_bf[slot]) + bd_ref[0, 0]
        y_ref[...] = _pack_bf16_pairs(y.astype(BF16))

    @pl.when(jnp.logical_not(active))
    def _():
        y_ref[...] = jnp.zeros_like(y_ref)

    lo = (n_chunks * j) // n
    hi = (n_chunks * (j + 1)) // n
    for q in range(n_chunks):
        @pl.when(jnp.logical_and(prefetch, jnp.logical_and(lo <= q, q < hi)))
        def _(q=q):
            land(nxt, q, 1 - slot)


def _experts(xs, sched, layer, w_gu, b_gu, w_down, b_down):
    rows_total = xs.shape[0]
    nb = rows_total // MOE_BLOCK
    depth, ne, d, n_gu = w_gu.shape
    assert xs.shape[1] * 2 == d and xs.dtype == jnp.uint32
    de = w_down.shape[2]
    assert n_gu == d and d % W_CHUNK == 0 and de % W_CHUNK == 0

    def blk(i, *s):
        return jnp.minimum(i, s[-1][0] - 1)

    def bspec(n):
        return pl.BlockSpec((1, 1, 1, n), lambda i, *s: (layer, s[0][blk(i, *s)], 0, 0))

    grid_spec = pltpu.PrefetchScalarGridSpec(
        num_scalar_prefetch=len(sched),
        grid=(nb,),
        in_specs=[
            pl.BlockSpec((MOE_BLOCK, d // 2), lambda i, *s: (blk(i, *s), 0)),
            pl.BlockSpec(memory_space=pl.ANY), bspec(n_gu),
            pl.BlockSpec(memory_space=pl.ANY), bspec(d),
        ],
        out_specs=pl.BlockSpec((MOE_BLOCK, d // 2), lambda i, *s: (i, 0)),
        scratch_shapes=[pltpu.VMEM((2, d, n_gu), BF16), pltpu.VMEM((2, de, d), BF16),
                        pltpu.VMEM((2, W_CHUNK, d), F32), pltpu.SemaphoreType.DMA((2,))],
    )
    return pl.pallas_call(
        functools.partial(_expert_kernel, layer=layer),
        grid_spec=grid_spec,
        out_shape=jax.ShapeDtypeStruct((rows_total, d // 2), jnp.uint32),
        compiler_params=_cparams("arbitrary"),
        name="moe_experts",
    )(*sched, xs, w_gu, b_gu.reshape(depth, ne, 1, n_gu), w_down, b_down.reshape(depth, ne, 1, d))


def _route(logits_t, n_tok):
    experts = jnp.arange(N_EXPERTS, dtype=jnp.int32)
    lg = logits_t
    e_iota = lax.broadcasted_iota(jnp.int32, lg.shape, 0)
    vals, idxs = [], []
    for _ in range(TOP_K):
        mx = jnp.max(lg, axis=0)
        ix = jnp.min(jnp.where(lg == mx[None, :], e_iota, N_EXPERTS), axis=0)
        vals.append(mx)
        idxs.append(ix)
        lg = jnp.where(e_iota == ix[None, :], -jnp.inf, lg)
    top_val = jnp.stack(vals)
    top_idx = jnp.stack(idxs)
    probs = jax.nn.softmax(top_val, axis=0)

    m = n_tok * TOP_K
    onehot = (top_idx[:, :, None] == experts[None, None, :]).astype(jnp.int32)
    per_tok = jnp.sum(onehot, axis=0)
    csum = jnp.cumsum(per_tok, axis=0)
    counts = csum[-1]
    pad_counts = (counts + MOE_BLOCK - 1) // MOE_BLOCK * MOE_BLOCK
    pad_end = jnp.cumsum(pad_counts)
    pad_start = pad_end - pad_counts
    base = csum - per_tok + pad_start[None, :]
    pos = jnp.sum(base[None, :, :] * onehot, axis=2)

    nb = (m + MOE_BLOCK - 1) // MOE_BLOCK + N_EXPERTS
    idx_bits = (m - 1).bit_length()
    assert (2 * N_EXPERTS + 2) << idx_bits < 2 ** 31
    assign = lax.broadcasted_iota(jnp.int32, (TOP_K, n_tok), 1) * TOP_K + \
        lax.broadcasted_iota(jnp.int32, (TOP_K, n_tok), 0)
    real_keys = (((2 * top_idx) << idx_bits) | assign).reshape(m)
    n_fill = nb * MOE_BLOCK - m
    assert n_fill % N_EXPERTS == 0 and n_fill // N_EXPERTS >= MOE_BLOCK - 1
    fill_i = lax.broadcasted_iota(jnp.int32, (N_EXPERTS, n_fill // N_EXPERTS), 1)
    fill_keys = jnp.where(fill_i < (pad_counts - counts)[:, None],
                          ((2 * experts + 1) << idx_bits)[:, None], (2 * N_EXPERTS + 1) << idx_bits)
    keys = lax.sort(jnp.concatenate([real_keys, fill_keys.reshape(n_fill)]))
    is_real = ((keys >> idx_bits) & 1) == 0
    row_tok = jnp.where(is_real, (keys & ((1 << idx_bits) - 1)) // TOP_K,
                        jnp.arange(nb * MOE_BLOCK, dtype=jnp.int32) % n_tok)

    blk_start = jnp.arange(nb, dtype=jnp.int32) * MOE_BLOCK
    blk_e = jnp.minimum(jnp.sum((pad_end[None, :] <= blk_start[:, None]).astype(jnp.int32), axis=1),
                        N_EXPERTS - 1)
    n_used = (pad_end[-1] // MOE_BLOCK).astype(jnp.int32).reshape(1)
    has_blocks = pad_counts > 0
    order_e = jnp.cumsum(has_blocks.astype(jnp.int32)) - 1
    later = jnp.logical_and(has_blocks[None, :], experts[None, :] > experts[:, None])
    next_e = jnp.min(jnp.where(later, experts[None, :], N_EXPERTS), axis=1)
    next_e = jnp.where(next_e == N_EXPERTS, -1, next_e)
    blk_is = blk_e[:, None] == experts[None, :]

    def per_block(table):
        return jnp.sum(jnp.where(blk_is, table[None, :], 0), axis=1)

    blk_j = (blk_start - per_block(pad_start)) // MOE_BLOCK
    blk_n = jnp.maximum(per_block(pad_counts) // MOE_BLOCK, 1)
    sched = (blk_e, blk_j, blk_n, per_block(order_e), per_block(next_e), n_used)
    return probs, pos, row_tok, sched


def _rows(x, idx):
    return x.at[idx].get(mode="promise_in_bounds")


def _moe(h2, tok0, logits_t, layer, w_gu, b_gu, w_down, b_down):
    n_tok = logits_t.shape[1]
    probs, pos, row_tok, sched = _route(logits_t, n_tok)
    xs = _rows(h2, row_tok + tok0)
    yb = _experts(xs, sched, layer, w_gu, b_gu, w_down, b_down)
    y = _rows(yb, pos.reshape(-1)).reshape(TOP_K, n_tok, -1)
    return y, probs.T


def _combine_kernel(x_ref, y0_ref, y1_ref, y2_ref, y3_ref, p_ref, g2_ref, nf_ref, o_ref,
                    *, tile0, final, n_ctx_tiles, tiles_per_batch, ctx_row):
    row = _mod_row(pl.program_id(0) + tile0, n_ctx_tiles, tiles_per_batch, ctx_row)
    p = p_ref[...]

    def expert_out(y_ref):
        u = y_ref[0]
        return jnp.concatenate([lax.bitcast_convert_type(u << 16, F32),
                                lax.bitcast_convert_type(u & jnp.uint32(0xFFFF0000), F32)], axis=1)

    y = (p[:, 0:1] * expert_out(y0_ref) + p[:, 1:2] * expert_out(y1_ref)) + \
        (p[:, 2:3] * expert_out(y2_ref) + p[:, 3:4] * expert_out(y3_ref))
    x = x_ref[...] + g2_ref[pl.ds(row, 1), :] * y
    if final:
        x = x * lax.rsqrt(jnp.mean(x * x, axis=-1, keepdims=True) + EPS) * nf_ref[...]
    o_ref[...] = x


def _combine(x, y, probs, mod, norm_f, *, tile0, final, n_ctx_tiles, tiles_per_batch, ctx_row):
    assert y.shape[0] == TOP_K == 4 and y.dtype == jnp.uint32
    n_rows = y.shape[1]
    d = x.shape[1]
    tm = ROW_TILE
    kern = functools.partial(_combine_kernel, tile0=tile0, final=final, n_ctx_tiles=n_ctx_tiles,
                             tiles_per_batch=tiles_per_batch, ctx_row=ctx_row)

    def yspec(k):
        return pl.BlockSpec((1, tm, d // 2), lambda i: (k, i, 0))

    return pl.pallas_call(
        kern,
        grid=(n_rows // tm,),
        in_specs=[pl.BlockSpec((tm, d), lambda i: (tile0 + i, 0)),
                  yspec(0), yspec(1), yspec(2), yspec(3),
                  pl.BlockSpec((tm, TOP_K), lambda i: (i, 0)),
                  pl.BlockSpec((8, d), lambda i: (0, 5)),
                  pl.BlockSpec((1, d), lambda i: (0, 0))],
        out_specs=pl.BlockSpec((tm, d), lambda i: (i, 0)),
        out_shape=jax.ShapeDtypeStruct((n_rows, d), F32),
        compiler_params=_cparams("parallel"),
        name="moe_combine",
    )(x, y, y, y, y, probs, mod, norm_f)


def _embed_kernel(ctx_ref, x_ref, pos_ref, o_ref, *, n_ctx_tiles):
    j = pl.program_id(0)
    b = pl.program_id(1)

    @pl.when(jnp.logical_and(j == 0, b < n_ctx_tiles))
    def _():
        o_ref[...] = ctx_ref[...]

    @pl.when(j > 0)
    def _():
        o_ref[...] = x_ref[...] + pos_ref[...]


def _embed(ctx2, x2, pos, *, batch):
    tm = ROW_TILE
    d = x2.shape[1]
    l = pos.shape[0]
    n_ctx_tiles = ctx2.shape[0] // tm
    nt = l // tm
    assert n_ctx_tiles <= batch
    return pl.pallas_call(
        functools.partial(_embed_kernel, n_ctx_tiles=n_ctx_tiles),
        grid=(1 + nt, batch),
        in_specs=[pl.BlockSpec((tm, d), lambda j, b: (jnp.minimum(b, n_ctx_tiles - 1), 0)),
                  pl.BlockSpec((tm, d), lambda j, b: (b * nt + jnp.maximum(j - 1, 0), 0)),
                  pl.BlockSpec((tm, d), lambda j, b: (jnp.maximum(j - 1, 0), 0))],
        out_specs=pl.BlockSpec(
            (tm, d), lambda j, b: (jnp.where(j == 0, jnp.minimum(b, n_ctx_tiles - 1),
                                             n_ctx_tiles + b * nt + j - 1), 0)),
        out_shape=jax.ShapeDtypeStruct((ctx2.shape[0] + x2.shape[0], d), F32),
        compiler_params=_cparams("arbitrary", "arbitrary"),
        name="embed_tokens",
    )(ctx2, x2, pos)


def _grid_pos_embed(n, d):
    rows = n // GRID_W
    quarter = d // 4
    omega = 1.0 / (10000.0 ** (jnp.arange(quarter, dtype=F32) / quarter))

    def enc(count):
        a = jnp.arange(count, dtype=F32)[:, None] * omega
        return jnp.concatenate([jnp.sin(a), jnp.cos(a)], axis=-1)

    row_part = jnp.repeat(enc(rows), GRID_W, axis=0)
    col_part = jnp.tile(enc(GRID_W), (rows, 1))
    return jnp.concatenate([row_part, col_part], axis=-1)


def kernel(x, c, ctx, c_ctx, w_mod, b_mod, norm1, w_in, w_a2, b_a, gla_norm, w_s, b_s, p_gla, p_sg, p_ft,
           w_out, norm2, w_router, b_router, w_gu, b_gu, w_down, b_down, norm_f):
    batch, l, d = x.shape
    lc = ctx.shape[1]
    depth = w_mod.shape[0]
    n_ctx_rows = batch * lc
    t_rows = n_ctx_rows + batch * l
    assert lc % ROW_TILE == 0 and l % ROW_TILE == 0 and batch < 8

    tm1 = math.gcd(math.gcd(n_ctx_rows, l), 1024)
    assert tm1 % ROW_TILE == 0
    proj_tiles = dict(n_ctx_tiles=n_ctx_rows // tm1, tiles_per_batch=l // tm1, ctx_row=batch)
    row_tiles = dict(n_ctx_tiles=n_ctx_rows // ROW_TILE, tiles_per_batch=l // ROW_TILE, ctx_row=batch)
    tm2 = ROW_TILE
    tm3 = math.gcd(tm1, 2 * ROW_TILE)
    big_tiles = dict(n_ctx_tiles=n_ctx_rows // tm3, tiles_per_batch=l // tm3, ctx_row=batch)

    o_q = 0
    o_k = o_q + W_Q
    o_v = o_k + W_Q
    o_r = o_v + W_V
    o_a = o_r + W_V
    o_su = o_a + 2 * GLA_RANK
    o_sv = o_su + W_SG
    o_f = o_sv + W_SG
    o_g = o_f + W_FT
    n_gates = 3 * d
    cols = dict(gates=0, q=n_gates, r=n_gates + 2 * W_Q + W_V, su=n_gates + 2 * W_Q + 2 * W_V)
    cols["sv"] = cols["su"] + W_SG
    cols["f"] = cols["sv"] + W_SG
    n_main = cols["f"] + W_FT

    xt = _embed(ctx.reshape(n_ctx_rows, d), x.reshape(batch * l, d), _grid_pos_embed(l, d), batch=batch)

    cond = jnp.zeros((8, d), F32).at[:batch].set(c).at[batch].set(c_ctx)
    mods = _mod_vectors(cond, w_mod, b_mod)

    cc, sc = _dft_tables(FT_CH)
    chan_tab = jnp.concatenate([cc, sc], axis=1)
    assert l % (FT_RADIX * FT_TILE) == 0
    ft_tables = _ft_long_tables(l)
    cos_c, sin_c = _dft_tables(lc)

    w_main = jnp.concatenate(
        [w_in[:, :, o_g:], w_in[:, :, o_q:o_a], w_in[:, :, o_su:o_g]], axis=2).astype(BF16)
    w_alr = jnp.pad(w_in[:, :, o_a:o_su], ((0, 0), (0, 0), (0, LANES - 2 * GLA_RANK))).astype(BF16)

    for i in range(depth):
        last = i == depth - 1
        w_a = w_alr[i]
        wa2 = jnp.zeros((LANES, 2 * W_Q), F32)
        wa2 = wa2.at[:GLA_RANK, :W_Q].set(w_a2[i, 0]).at[GLA_RANK:2 * GLA_RANK, W_Q:].set(w_a2[i, 1])
        mod = mods[i]

        z, la = _project(xt, mod, norm1[i][None], w_main, w_a, wa2.astype(BF16), b_a[i].reshape(1, 2 * W_Q),
                         layer=i, tm=tm1, tn=1536, **proj_tiles)
        assert z.shape[1] == n_main

        o_fw, o_bw = _gla(z, la, batch=batch, lc=lc, l=l, col_q=cols["q"])

        xc_c, xs_c = _ft_channel(z, chan_tab, col_f=cols["f"], row0=0, n_rows=n_ctx_rows)
        xc_l, xs_l = _ft_channel(z, chan_tab, col_f=cols["f"], row0=n_ctx_rows, n_rows=batch * l)
        y_ft = (_ft_positions(cos_c, sin_c, xc_c, xs_c, batch=batch, seq=lc),
                _ft_positions_long(ft_tables, xc_l, xs_l, batch=batch, seq=l))

        bs_b = jnp.broadcast_to(b_s[i][:, :, None], (SG_GROUPS, SG_CHUNK, SG_CH))
        merged = _branches(z, o_fw, o_bw, y_ft, gla_norm[i][None], w_s[i].astype(BF16), bs_b,
                           p_gla[i].astype(BF16), p_sg[i].astype(BF16), p_ft[i].astype(BF16), cols=cols, tm=tm2)

        w_r = jnp.pad(w_router[i].T, ((0, LANES - N_EXPERTS), (0, 0)))
        b_r = jnp.pad(b_router[i], (0, LANES - N_EXPERTS)).reshape(LANES, 1)
        xt, h2, logits_t = _out_project(merged, xt, w_out[i].astype(BF16), mod, norm2[i][None], w_r, b_r,
                                      tm=tm3, **big_tiles)

        row0 = n_ctx_rows if last else 0
        y, probs = _moe(h2, row0, logits_t[:N_EXPERTS, row0:], i, w_gu, b_gu, w_down, b_down)
        xt = _combine(xt, y, probs, mod, norm_f[None], tile0=row0 // ROW_TILE, final=last, **row_tiles)

    return xt.reshape(batch, l, d)
```

```python
import functools
import math

import jax
import jax.numpy as jnp
from jax import lax
from jax.experimental import pallas as pl
from jax.experimental.pallas import tpu as pltpu

F32 = jnp.float32
BF16 = jnp.bfloat16

EPS = 1e-6
GRID_W = 64
GLA_HEADS = 4
GLA_DK = 128
GLA_DV = 256
GLA_RANK = 16
GLA_TAU = 16.0
GLA_CHUNK = 64
GLA_SUB = 8
GLA_STAGES = 3
SG_GROUPS = 4
SG_CH = 128
SG_CHUNK = 128
FT_GROUPS = 4
FT_CH = 128
N_EXPERTS = 32
TOP_K = 4
SWIGLU_LIMIT = 7.0
SWIGLU_ALPHA = 1.702
MOE_BLOCK = 256
W_CHUNK = 512

LOG2E = math.log2(math.e)
LANES = 128
ROW_TILE = 256
OUT_SUB_ROWS = 256
VMEM_LIMIT = 56 * 1024 * 1024

W_Q = GLA_HEADS * GLA_DK
W_V = GLA_HEADS * GLA_DV
W_SG = SG_GROUPS * SG_CH
W_FT = FT_GROUPS * FT_CH


def _cparams(*sem):
    return pltpu.CompilerParams(dimension_semantics=sem, vmem_limit_bytes=VMEM_LIMIT)


def _sigmoid(x):
    return 1.0 / (1.0 + jnp.exp(-x))


def _half_gated(half_gate, proj):
    return proj + jnp.tanh(half_gate.astype(F32)) * proj


def _gelu_tanh(x):
    return 0.5 * x * (1.0 + jnp.tanh(math.sqrt(2.0 / math.pi) * (x + 0.044715 * (x * x * x))))


def _dot(a, b):
    return jnp.dot(a, b, preferred_element_type=F32)


def _pack_bf16_pairs(x):
    bits = lax.bitcast_convert_type(x.astype(F32), jnp.uint32)
    half = x.shape[1] // 2
    return (bits[:, :half] >> 16) | (bits[:, half:] & jnp.uint32(0xFFFF0000))


def _unpack_bf16_pairs(u):
    lo = lax.bitcast_convert_type(u << 16, F32)
    hi = lax.bitcast_convert_type(u & jnp.uint32(0xFFFF0000), F32)
    return jnp.concatenate([lo, hi], axis=1).astype(BF16)


def _mod_row(tile, n_ctx_tiles, tiles_per_batch, ctx_row):
    return jnp.where(tile < n_ctx_tiles, ctx_row, (tile - n_ctx_tiles) // tiles_per_batch)


def _mod_kernel(c_ref, w_ref, b_ref, o_ref):
    c = c_ref[...]
    s = (c * _sigmoid(c)).astype(BF16)
    o_ref[0] = _dot(s, w_ref[0].astype(BF16)) + b_ref[0]


def _mod_vectors(cond, w_mod, b_mod):
    depth, d, n = w_mod.shape
    tn = 1024
    return pl.pallas_call(
        _mod_kernel,
        grid=(depth, n // tn),
        in_specs=[
            pl.BlockSpec((8, d), lambda l, j: (0, 0)),
            pl.BlockSpec((1, d, tn), lambda l, j: (l, 0, j)),
            pl.BlockSpec((1, 1, tn), lambda l, j: (l, 0, j)),
        ],
        out_specs=pl.BlockSpec((1, 8, tn), lambda l, j: (l, 0, j)),
        out_shape=jax.ShapeDtypeStruct((depth, 8, n), F32),
        compiler_params=_cparams("parallel", "parallel"),
        name="mod_vectors",
    )(cond, w_mod, b_mod.reshape(depth, 1, n))


def _proj_kernel(x_ref, sh_ref, sc_ref, g_ref, w_ref, wa_ref, wa2_ref, ba_ref,
                 z_ref, la_ref, a_scr, *, n_ctx_tiles, tiles_per_batch, ctx_row):
    i = pl.program_id(0)
    j = pl.program_id(1)

    @pl.when(j == 0)
    def _():
        row = _mod_row(i, n_ctx_tiles, tiles_per_batch, ctx_row)
        gain = g_ref[...] * (1.0 + sc_ref[pl.ds(row, 1), :])
        shift = sh_ref[pl.ds(row, 1), :]

        def body(r, carry):
            rows = pl.ds(pl.multiple_of(r * ROW_TILE, ROW_TILE), ROW_TILE)
            x = x_ref[rows, :]
            y = x * lax.rsqrt(jnp.mean(x * x, axis=-1, keepdims=True) + EPS)
            hb = (y * gain + shift).astype(BF16)
            a_scr[rows, :] = hb
            a_lr = _dot(hb, wa_ref[...])
            pre = _dot(a_lr.astype(BF16), wa2_ref[...]) + ba_ref[...]
            la_ref[rows, :] = -(jnp.maximum(-pre, 0.0) + jnp.log(1.0 + jnp.exp(-jnp.abs(pre)))) * (1.0 / GLA_TAU)
            return carry

        lax.fori_loop(0, x_ref.shape[0] // ROW_TILE, body, 0)

    z_ref[...] = _dot(a_scr[...], w_ref[0]).astype(BF16)


def _project(x, mod, g, w_main, w_a, w_a2, b_a, *, layer, tm, tn, n_ctx_tiles, tiles_per_batch, ctx_row):
    t, d = x.shape
    n = w_main.shape[2]
    kern = functools.partial(_proj_kernel, n_ctx_tiles=n_ctx_tiles,
                             tiles_per_batch=tiles_per_batch, ctx_row=ctx_row)
    return pl.pallas_call(
        kern,
        grid=(t // tm, n // tn),
        in_specs=[
            pl.BlockSpec((tm, d), lambda i, j: (i, 0)),
            pl.BlockSpec((8, d), lambda i, j: (0, 0)),
            pl.BlockSpec((8, d), lambda i, j: (0, 1)),
            pl.BlockSpec((1, d), lambda i, j: (0, 0)),
            pl.BlockSpec((1, d, tn), lambda i, j: (layer, 0, j)),
            pl.BlockSpec((d, LANES), lambda i, j: (0, 0)),
            pl.BlockSpec((LANES, 2 * W_Q), lambda i, j: (0, 0)),
            pl.BlockSpec((1, 2 * W_Q), lambda i, j: (0, 0)),
        ],
        out_specs=[
            pl.BlockSpec((tm, tn), lambda i, j: (i, j)),
            pl.BlockSpec((tm, 2 * W_Q), lambda i, j: (i, 0)),
        ],
        out_shape=[
            jax.ShapeDtypeStruct((t, n), BF16),
            jax.ShapeDtypeStruct((t, 2 * W_Q), F32),
        ],
        scratch_shapes=[pltpu.VMEM((tm, d), BF16)],
        compiler_params=_cparams("parallel", "arbitrary"),
        name="norm_mod_project",
    )(x, mod, mod, g, w_main, w_a, w_a2, b_a)


def _gla_pairs(rev):
    c = GLA_CHUNK
    pairs = []
    s = c // 2
    while s >= GLA_SUB:
        for b0 in range(0, c, 2 * s):
            if rev:
                pairs.append((b0, b0 + s, b0 + s, b0 + 2 * s, b0 + s - 1))
            else:
                pairs.append((b0 + s, b0 + 2 * s, b0, b0 + s, b0 + s))
        s //= 2
    return tuple(pairs)


def _place_rows(x, r0, n):
    parts = []
    if r0:
        parts.append(jnp.zeros((r0, x.shape[1]), x.dtype))
    parts.append(x)
    rest = n - r0 - x.shape[0]
    if rest:
        parts.append(jnp.zeros((rest, x.shape[1]), x.dtype))
    return jnp.concatenate(parts, axis=0)


def _gla_chunk(q_ref, k_ref, v_ref, la_ref, e_ref, o_ref, s_ref, cum_ref, kc_ref, r0, h, rev):
    c, sub = GLA_CHUNK, GLA_SUB
    nsub = c // sub
    rows = pl.ds(r0, c)
    kcols = slice(h * GLA_DK, (h + 1) * GLA_DK)
    vcols = slice(h * GLA_DV, (h + 1) * GLA_DV)
    q = q_ref[rows, kcols].astype(F32) * (GLA_DK ** -0.5)
    k = k_ref[rows, kcols].astype(F32)
    v = v_ref[rows, vcols]
    la = la_ref[rows, kcols]

    ri = lax.broadcasted_iota(jnp.int32, (c, c), 0)
    ci = lax.broadcasted_iota(jnp.int32, (c, c), 1)
    before = (ci >= ri) if rev else (ci <= ri)
    tri = jnp.where(before, 1.0, 0.0).astype(BF16)
    la_hi = la.astype(BF16)
    rem = la - la_hi.astype(F32)
    la_mid = rem.astype(BF16)
    la_lo = (rem - la_mid.astype(F32)).astype(BF16)
    cum = (_dot(tri, la_hi) + _dot(tri, la_mid) + _dot(tri, la_lo)) * LOG2E
    cum_ref[...] = cum
    kc_ref[...] = k
    yield
    last = 0 if rev else c - 1
    total = cum[last:last + 1, :]

    s = s_ref[...]
    o = _dot((q * jnp.exp2(cum)).astype(BF16), s.astype(BF16))

    qparts, kparts = [], []
    for l0, l1, e0, e1, ref in _gla_pairs(rev):
        cr = cum[ref:ref + 1, :]
        qparts.append(_place_rows(q[l0:l1] * jnp.exp2(cum[l0:l1] - cr), l0, c))
        kparts.append(_place_rows(k[e0:e1] * jnp.exp2(cr - cum[e0:e1]), e0, c))
    s_off = lax.dot_general(jnp.concatenate(qparts, axis=1).astype(BF16),
                            jnp.concatenate(kparts, axis=1).astype(BF16),
                            (((1,), (1,)), ((), ())), preferred_element_type=F32)

    hs = sub // 2
    skip_halves = hs % 8 == 0

    def halves(x, which):
        if len(which) == 2:
            return x
        return jnp.concatenate([x[a * sub + w * hs:a * sub + (w + 1) * hs] for a in range(nsub) for w in which],
                               axis=0)

    parts = []
    for jj in range(sub):
        if not skip_halves:
            which = (0, 1)
        elif rev:
            which = (0, 1) if jj >= hs else (0,)
        else:
            which = (0, 1) if jj < hs else (1,)
        nrow = hs * len(which)
        cj = jnp.concatenate(
            [jnp.broadcast_to(cum_ref[a * sub + jj:a * sub + jj + 1, :], (nrow, GLA_DK)) for a in range(nsub)],
            axis=0)
        kj = jnp.concatenate(
            [jnp.broadcast_to(kc_ref[a * sub + jj:a * sub + jj + 1, :], (nrow, GLA_DK)) for a in range(nsub)],
            axis=0)
        prod = halves(q, which) * jnp.exp2(jnp.minimum(halves(cum, which) - cj, 0.0)) * kj
        if len(which) == 1:
            zero = jnp.zeros((hs, GLA_DK), F32)
            pieces = []
            for a in range(nsub):
                live = prod[a * hs:(a + 1) * hs]
                pieces += [live, zero] if which[0] == 0 else [zero, live]
            prod = jnp.concatenate(pieces, axis=0)
        parts.append(prod.astype(BF16))
    diag = _dot(jnp.concatenate(parts, axis=1), e_ref[...])
    yield

    lag = (ci - ri) if rev else (ri - ci)
    room = (sub - 1 - ri % sub) if rev else (ri % sub)
    in_diag = lag.astype(jnp.uint32) <= room.astype(jnp.uint32)
    p = jnp.where(in_diag, diag, 0.0) + s_off
    o_ref[rows, vcols] = o + _dot(p.astype(BF16), v)

    k_out = (k * jnp.exp2(total - cum)).astype(BF16)
    upd = lax.dot_general(k_out, v, (((0,), (0,)), ((), ())), preferred_element_type=F32)
    dmat = jnp.broadcast_to(jnp.exp2(total), (GLA_DK, GLA_DK)).T
    for half in range(GLA_DV // GLA_DK):
        cols = slice(half * GLA_DK, (half + 1) * GLA_DK)
        s_ref[:, cols] = s[:, cols] * dmat + upd[:, cols]


def _gla_kernel(qf, kf, vf, laf, qb, kb, vb, lab, e_ref, of_ref, ob_ref, *scratch):
    n = 2 * GLA_HEADS
    states, cums, keys = scratch[:n], scratch[n:2 * n], scratch[2 * n:]

    @pl.when(pl.program_id(1) == 0)
    def _():
        for s_ref in states:
            s_ref[...] = jnp.zeros_like(s_ref)

    nchunk = ROW_TILE // GLA_CHUNK

    def body(c, carry):
        r_f = pl.multiple_of(c * GLA_CHUNK, GLA_CHUNK)
        r_b = pl.multiple_of((nchunk - 1 - c) * GLA_CHUNK, GLA_CHUNK)
        units = []
        for h in range(GLA_HEADS):
            f, b = h, GLA_HEADS + h
            units.append(_gla_chunk(qf, kf, vf, laf, e_ref, of_ref, states[f], cums[f], keys[f], r_f, h, False))
            units.append(_gla_chunk(qb, kb, vb, lab, e_ref, ob_ref, states[b], cums[b], keys[b], r_b, h, True))
        for _ in range(GLA_STAGES):
            for u in units:
                next(u, None)
        return carry

    lax.fori_loop(0, nchunk, body, 0)


def _gla(z, la, *, batch, lc, l, col_q):
    t_rows = z.shape[0]
    nctx = lc // ROW_TILE
    nlat = l // ROW_TILE
    steps = nctx + nlat
    qb0 = col_q // W_Q
    kb0 = (col_q + W_Q) // W_Q
    vb0 = (col_q + 2 * W_Q) // W_V

    def fwd_tile(b, t):
        return jnp.where(t < nctx, b * nctx + t, batch * nctx + b * nlat + (t - nctx))

    def bwd_tile(b, t):
        return jnp.where(t < nctx, b * nctx + (nctx - 1 - t), batch * nctx + b * nlat + (steps - 1 - t))

    def spec(width, tile_fn, col):
        return pl.BlockSpec((ROW_TILE, width), lambda b, t: (tile_fn(b, t), col))

    kk = jnp.arange(GLA_SUB * GLA_DK, dtype=jnp.int32)[:, None] // GLA_DK
    cc = jnp.arange(GLA_CHUNK, dtype=jnp.int32)[None, :] % GLA_SUB
    e_mat = (kk == cc).astype(BF16)

    o_shape = jax.ShapeDtypeStruct((t_rows, W_V), F32)
    n_units = 2 * GLA_HEADS
    state = pltpu.VMEM((GLA_DK, GLA_DV), F32)
    chunk = pltpu.VMEM((GLA_CHUNK, GLA_DK), F32)
    return pl.pallas_call(
        _gla_kernel,
        grid=(batch, steps),
        in_specs=[
            spec(W_Q, fwd_tile, qb0), spec(W_Q, fwd_tile, kb0), spec(W_V, fwd_tile, vb0), spec(W_Q, fwd_tile, 0),
            spec(W_Q, bwd_tile, qb0), spec(W_Q, bwd_tile, kb0), spec(W_V, bwd_tile, vb0), spec(W_Q, bwd_tile, 1),
            pl.BlockSpec((GLA_SUB * GLA_DK, GLA_CHUNK), lambda b, t: (0, 0)),
        ],
        out_specs=[spec(W_V, fwd_tile, 0), spec(W_V, bwd_tile, 0)],
        out_shape=[o_shape, o_shape],
        scratch_shapes=[state] * n_units + [chunk] * (2 * n_units),
        compiler_params=_cparams("parallel", "arbitrary"),
        name="gla_scan",
    )(z, z, z, la, z, z, z, la, e_mat)


def _dft_tables(n):
    scale = n ** -0.5
    p = jnp.arange(n, dtype=jnp.int32)[:, None]

    def trig(cols, period):
        ang = ((p * cols[None, :]) % period).astype(F32) * (2.0 * math.pi / period)
        return jnp.cos(ang), jnp.sin(ang)

    n2 = 1
    while n2 * n2 < n:
        n2 *= 2
    if n % n2 or n2 == 1:
        c, s = trig(jnp.arange(n, dtype=jnp.int32), n)
        return (c * scale).astype(BF16), (s * scale).astype(BF16)
    n1 = n // n2
    ca, sa = trig(jnp.arange(n1, dtype=jnp.int32), n1)
    cb, sb = trig(jnp.arange(n2, dtype=jnp.int32), n)
    ca, sa, cb, sb = ca[:, :, None], sa[:, :, None], cb[:, None, :] * scale, sb[:, None, :] * scale
    return ((ca * cb - sa * sb).reshape(n, n).astype(BF16), (sa * cb + ca * sb).reshape(n, n).astype(BF16))


def _ft_chan_kernel(f_ref, cs_ref, xc_ref, xs_ref):
    for g in range(FT_GROUPS):
        cols = slice(g * FT_CH, (g + 1) * FT_CH)
        y = _dot(f_ref[:, cols], cs_ref[...])
        xc_ref[:, cols] = y[:, :FT_CH].astype(BF16)
        xs_ref[:, cols] = y[:, FT_CH:].astype(BF16)


def _ft_channel(z, cs, *, col_f, row0, n_rows):
    tm = math.gcd(math.gcd(row0, n_rows), 1024)
    shape = jax.ShapeDtypeStruct((n_rows, W_FT), BF16)
    return pl.pallas_call(
        _ft_chan_kernel,
        grid=(n_rows // tm,),
        in_specs=[pl.BlockSpec((tm, W_FT), lambda i: (row0 // tm + i, col_f // W_FT)),
                  pl.BlockSpec((FT_CH, 2 * FT_CH), lambda i: (0, 0))],
        out_specs=[pl.BlockSpec((tm, W_FT), lambda i: (i, 0))] * 2,
        out_shape=[shape, shape],
        compiler_params=_cparams("parallel"),
        name="fourier_channels",
    )(z, cs)


def _ft_pos_kernel(c_ref, s_ref, xc_ref, xs_ref, o_ref):
    o_ref[...] = _dot(c_ref[...], xc_ref[...]) - _dot(s_ref[...], xs_ref[...])


def _ft_positions(cos_t, sin_t, xc, xs, *, batch, seq):
    nt = seq // ROW_TILE
    x_spec = pl.BlockSpec((seq, W_FT), lambda b, j: (b, 0))
    t_spec = pl.BlockSpec((ROW_TILE, seq), lambda b, j: (j, 0))
    return pl.pallas_call(
        _ft_pos_kernel,
        grid=(batch, nt),
        in_specs=[t_spec, t_spec, x_spec, x_spec],
        out_specs=pl.BlockSpec((ROW_TILE, W_FT), lambda b, j: (b * nt + j, 0)),
        out_shape=jax.ShapeDtypeStruct((batch * seq, W_FT), F32),
        compiler_params=_cparams("parallel", "parallel"),
        name="fourier_positions",
    )(cos_t, sin_t, xc, xs)


FT_RADIX = 16
FT_TILE = 16


def _ft_stage1_kernel(xc_ref, xs_ref, k_ref, tc_ref, ts_ref, o_ref):
    rows = FT_RADIX * FT_TILE
    zin = jnp.concatenate([xc_ref[0].reshape(rows, W_FT), xs_ref[0].reshape(rows, W_FT)], axis=0)
    a = _dot(k_ref[...], zin)
    re, im = a[:rows], a[rows:]
    tc, ts = tc_ref[0], ts_ref[0]
    o_ref[0, 0] = (re * tc + im * ts).astype(BF16).reshape(FT_RADIX, FT_TILE, W_FT)
    o_ref[0, 1] = (im * tc - re * ts).astype(BF16).reshape(FT_RADIX, FT_TILE, W_FT)


def _ft_stage2_kernel(b_ref, t_ref, o_ref):
    rhs = jnp.concatenate([b_ref[0, 0, 0], b_ref[0, 1, 0]], axis=0)
    o_ref[0, 0] = _dot(t_ref[...], rhs)


def _ft_long_tables(seq):
    r, n1 = FT_RADIX, seq // FT_RADIX

    def trig(rows, cols, period):
        ang = ((rows[:, None] * cols[None, :]) % period).astype(F32) * (2.0 * math.pi / period)
        return jnp.cos(ang), jnp.sin(ang)

    ir = jnp.arange(r, dtype=jnp.int32)
    i1 = jnp.arange(n1, dtype=jnp.int32)
    c1, s1 = trig(ir, ir, r)
    t1 = jnp.concatenate([jnp.concatenate([c1, -s1], axis=1), jnp.concatenate([-s1, -c1], axis=1)], axis=0)
    k1 = jnp.kron(t1 * r ** -0.5, jnp.eye(FT_TILE, dtype=F32)).astype(BF16)
    tc, ts = trig(ir, i1, seq)

    def per_tile(t):
        return t.reshape(r, n1 // FT_TILE, FT_TILE).transpose(1, 0, 2).reshape(n1 // FT_TILE, r * FT_TILE, 1)

    c2, s2 = trig(i1, i1, n1)
    t2 = (jnp.concatenate([c2, s2], axis=1) * n1 ** -0.5).astype(BF16)
    return k1, per_tile(tc), per_tile(ts), t2


def _ft_positions_long(tables, xc, xs, *, batch, seq):
    k1, tc, ts, t2 = tables
    r, n1 = FT_RADIX, seq // FT_RADIX
    nt = n1 // FT_TILE
    rows = r * FT_TILE
    x4 = (batch, r, n1, W_FT)
    x_spec = pl.BlockSpec((1, r, FT_TILE, W_FT), lambda b, t: (b, 0, t, 0))
    tw_spec = pl.BlockSpec((1, rows, 1), lambda b, t: (t, 0, 0))
    stage1 = pl.pallas_call(
        _ft_stage1_kernel,
        grid=(batch, nt),
        in_specs=[x_spec, x_spec, pl.BlockSpec((2 * rows, 2 * rows), lambda b, t: (0, 0)), tw_spec, tw_spec],
        out_specs=pl.BlockSpec((1, 2, r, FT_TILE, W_FT), lambda b, t: (b, 0, 0, t, 0)),
        out_shape=jax.ShapeDtypeStruct((batch, 2, r, n1, W_FT), BF16),
        compiler_params=_cparams("parallel", "parallel"),
        name="fourier_stage1",
    )(xc.reshape(x4), xs.reshape(x4), k1, tc, ts)
    stage2 = pl.pallas_call(
        _ft_stage2_kernel,
        grid=(batch, r),
        in_specs=[pl.BlockSpec((1, 2, 1, n1, W_FT), lambda b, p: (b, 0, p, 0, 0)),
                  pl.BlockSpec((n1, 2 * n1), lambda b, p: (0, 0))],
        out_specs=pl.BlockSpec((1, 1, n1, W_FT), lambda b, p: (b, p, 0, 0)),
        out_shape=jax.ShapeDtypeStruct((batch, r, n1, W_FT), F32),
        compiler_params=_cparams("parallel", "parallel"),
        name="fourier_stage2",
    )(stage1, t2)
    return stage2.transpose(0, 2, 1, 3).reshape(batch * seq, W_FT)


def _branch_kernel(gg_ref, gs_ref, gf_ref, of_ref, ob_ref, r_ref, su_ref, sv_ref, yc_ref, yl_ref,
                   gn_ref, ws_ref, bs_ref, pg_ref, ps_ref, pf_ref, m_ref, *, n_ctx_tiles):
    o = of_ref[...] + ob_ref[...]
    heads = []
    for h in range(GLA_HEADS):
        oh = o[:, h * GLA_DV:(h + 1) * GLA_DV]
        heads.append(oh * lax.rsqrt(jnp.mean(oh * oh, axis=-1, keepdims=True) + EPS))
    r = r_ref[...].astype(F32)
    half_r = 0.5 * r
    silu_r = half_r + half_r * jnp.tanh(half_r)
    y_gla = (jnp.concatenate(heads, axis=-1) * gn_ref[...]) * silu_r
    y_gla = y_gla.astype(BF16)

    u = _gelu_tanh(su_ref[...].astype(F32))
    v = _gelu_tanh(sv_ref[...].astype(F32))
    groups = []
    for g in range(SG_GROUPS):
        vg = v[:, g * SG_CH:(g + 1) * SG_CH]
        mu = jnp.mean(vg, axis=-1, keepdims=True)
        var = jnp.mean(jnp.square(vg - mu), axis=-1, keepdims=True)
        vn = ((vg - mu) * lax.rsqrt(var + EPS)).astype(BF16)
        chunks = []
        for c in range(v.shape[0] // SG_CHUNK):
            chunks.append(_dot(ws_ref[g], vn[c * SG_CHUNK:(c + 1) * SG_CHUNK, :]) + bs_ref[g])
        groups.append(jnp.concatenate(chunks, axis=0))
    y_sg = u * jnp.concatenate(groups, axis=-1)
    y_sg = y_sg.astype(BF16)

    y_ft = jnp.where(pl.program_id(0) < n_ctx_tiles, yc_ref[...], yl_ref[...])
    y_ft = y_ft.astype(BF16)

    acc = _half_gated(gg_ref[...], _dot(y_gla, pg_ref[...]))
    acc = acc + _half_gated(gs_ref[...], _dot(y_sg, ps_ref[...]))
    acc = acc + _half_gated(gf_ref[...], _dot(y_ft, pf_ref[...]))
    m_ref[...] = (0.5 * acc).astype(BF16)


def _branches(z, o_f, o_b, y_ft, gla_norm, w_s, b_s, p_gla, p_sg, p_ft, *, cols, tm):
    t_rows = z.shape[0]
    d = p_gla.shape[1]
    n_ctx_tiles = y_ft[0].shape[0] // tm
    kern = functools.partial(_branch_kernel, n_ctx_tiles=n_ctx_tiles)

    def zspec(width, col0):
        return pl.BlockSpec((tm, width), lambda i: (i, col0 // width))

    def rows(width):
        return pl.BlockSpec((tm, width), lambda i: (i, 0))

    def const(shape):
        return pl.BlockSpec(shape, lambda i: (0,) * len(shape))

    return pl.pallas_call(
        kern,
        grid=(t_rows // tm,),
        in_specs=[
            zspec(d, cols["gates"]), zspec(d, cols["gates"] + d), zspec(d, cols["gates"] + 2 * d),
            rows(W_V), rows(W_V), zspec(W_V, cols["r"]), zspec(W_SG, cols["su"]), zspec(W_SG, cols["sv"]),
            pl.BlockSpec((tm, W_FT), lambda i: (jnp.minimum(i, n_ctx_tiles - 1), 0)),
            pl.BlockSpec((tm, W_FT), lambda i: (jnp.maximum(i - n_ctx_tiles, 0), 0)),
            const((1, W_V)), const((SG_GROUPS, SG_CHUNK, SG_CHUNK)), const((SG_GROUPS, SG_CHUNK, SG_CH)),
            const((W_V, d)), const((W_SG, d)), const((W_FT, d)),
        ],
        out_specs=rows(d),
        out_shape=jax.ShapeDtypeStruct((t_rows, d), BF16),
        compiler_params=_cparams("parallel"),
        name="branch_merge",
    )(z, z, z, o_f, o_b, z, z, z, y_ft[0], y_ft[1], gla_norm, w_s, b_s, p_gla, p_sg, p_ft)


def _outproj_kernel(m_ref, x_ref, w_ref, g1_ref, n2_ref, sh_ref, sc_ref, wr_ref, br_ref,
                    xo_ref, h_ref, lg_ref, *, n_ctx_tiles, tiles_per_batch, ctx_row):
    row = _mod_row(pl.program_id(0), n_ctx_tiles, tiles_per_batch, ctx_row)
    gate1 = g1_ref[pl.ds(row, 1), :]
    scale2 = 1.0 + sc_ref[pl.ds(row, 1), :]
    shift2 = sh_ref[pl.ds(row, 1), :]
    w = wr_ref[...]
    w_hi = w.astype(BF16)
    w_lo = (w - w_hi.astype(F32)).astype(BF16)
    w_both = jnp.concatenate([w_hi, w_lo], axis=0)
    contract_last = (((1,), (1,)), ((), ()))
    sub = OUT_SUB_ROWS
    starts = list(range(0, x_ref.shape[0], sub))
    proj = {starts[0]: _dot(m_ref[0:sub, :], w_ref[...])}
    for k, r0 in enumerate(starts):
        rows = slice(r0, r0 + sub)
        if k + 1 < len(starts):
            nxt = starts[k + 1]
            proj[nxt] = _dot(m_ref[nxt:nxt + sub, :], w_ref[...])
        x = x_ref[rows, :] + gate1 * proj.pop(r0)
        xo_ref[rows, :] = x
        y = x * lax.rsqrt(jnp.mean(x * x, axis=-1, keepdims=True) + EPS) * n2_ref[...]
        h = y * scale2 + shift2
        h_hi = h.astype(BF16)
        h_ref[rows, :] = _pack_bf16_pairs(h_hi)
        h_lo = (h - h_hi.astype(F32)).astype(BF16)
        hi_part = lax.dot_general(w_both, h_hi, contract_last, preferred_element_type=F32)
        lo_part = lax.dot_general(w_hi, h_lo, contract_last, preferred_element_type=F32)
        lg_ref[:, rows] = hi_part[:LANES] + (hi_part[LANES:] + lo_part) + br_ref[...]


def _out_project(merged, x, w_out, mod, norm2, w_router, b_router, *, tm, n_ctx_tiles, tiles_per_batch, ctx_row):
    t_rows, d = x.shape
    kern = functools.partial(_outproj_kernel, n_ctx_tiles=n_ctx_tiles,
                             tiles_per_batch=tiles_per_batch, ctx_row=ctx_row)

    def rows(width):
        return pl.BlockSpec((tm, width), lambda i: (i, 0))

    def modspec(k):
        return pl.BlockSpec((8, d), lambda i: (0, k))

    return pl.pallas_call(
        kern,
        grid=(t_rows // tm,),
        in_specs=[rows(d), rows(d), pl.BlockSpec((d, d), lambda i: (0, 0)),
                  modspec(2), pl.BlockSpec((1, d), lambda i: (0, 0)), modspec(3), modspec(4),
                  pl.BlockSpec((LANES, d), lambda i: (0, 0)), pl.BlockSpec((LANES, 1), lambda i: (0, 0))],
        out_specs=[rows(d), rows(d // 2), pl.BlockSpec((LANES, tm), lambda i: (0, i))],
        out_shape=[jax.ShapeDtypeStruct((t_rows, d), F32), jax.ShapeDtypeStruct((t_rows, d // 2), jnp.uint32),
                   jax.ShapeDtypeStruct((LANES, t_rows), F32)],
        compiler_params=_cparams("parallel"),
        name="out_project",
    )(merged, x, w_out, mod, norm2, mod, mod, w_router, b_router)


def _expert_kernel(be_ref, jb_ref, nblk_ref, ord_ref, nxt_ref, nu_ref,
                   x_ref, wgu_hbm, bgu_ref, wd_hbm, bd_ref, y_ref,
                   wgu_bf, wd_bf, stage, sems, *, layer):
    i = pl.program_id(0)
    n_gu_chunks = wgu_bf.shape[1] // W_CHUNK
    n_chunks = n_gu_chunks + wd_bf.shape[1] // W_CHUNK

    def chunk_copy(e, q):
        if q < n_gu_chunks:
            src = wgu_hbm.at[layer, e, pl.ds(q * W_CHUNK, W_CHUNK), :]
        else:
            src = wd_hbm.at[layer, e, pl.ds((q - n_gu_chunks) * W_CHUNK, W_CHUNK), :]
        return pltpu.make_async_copy(src, stage.at[q % 2], sems.at[q % 2])

    def land(e, q, slot):
        chunk_copy(e, q).wait()
        w = stage[q % 2].astype(BF16)
        if q < n_gu_chunks:
            wgu_bf[slot, pl.ds(q * W_CHUNK, W_CHUNK), :] = w
        else:
            wd_bf[slot, pl.ds((q - n_gu_chunks) * W_CHUNK, W_CHUNK), :] = w
        if q + 2 < n_chunks:
            chunk_copy(e, q + 2).start()

    active = i < nu_ref[0]
    e = be_ref[i]
    j = jb_ref[i]
    n = nblk_ref[i]
    slot = ord_ref[i] % 2
    nxt = nxt_ref[i]

    @pl.when(i == 0)
    def _():
        chunk_copy(e, 0).start()
        chunk_copy(e, 1).start()
        for q in range(n_chunks):
            land(e, q, slot)

    prefetch = jnp.logical_and(active, nxt >= 0)

    @pl.when(jnp.logical_and(prefetch, j == 0))
    def _():
        chunk_copy(nxt, 0).start()
        chunk_copy(nxt, 1).start()

    @pl.when(active)
    def _():
        gu = _dot(_unpack_bf16_pairs(x_ref[...]), wgu_bf[slot]) + bgu_ref[0, 0]
        de = gu.shape[1] // 2
        gate = jnp.minimum(gu[:, :de], SWIGLU_LIMIT)
        up = jnp.clip(gu[:, de:], -SWIGLU_LIMIT, SWIGLU_LIMIT)
        hdn = (up + 1.0) * (gate * _sigmoid(SWIGLU_ALPHA * gate))
        y = _dot(hdn.astype(BF16), wd_bf[slot]) + bd_ref[0, 0]
        y_ref[...] = _pack_bf16_pairs(y.astype(BF16))

    @pl.when(jnp.logical_not(active))
    def _():
        y_ref[...] = jnp.zeros_like(y_ref)

    lo = (n_chunks * j) // n
    hi = (n_chunks * (j + 1)) // n
    for q in range(n_chunks):
        @pl.when(jnp.logical_and(prefetch, jnp.logical_and(lo <= q, q < hi)))
        def _(q=q):
            land(nxt, q, 1 - slot)


def _experts(xs, sched, layer, w_gu, b_gu, w_down, b_down):
    rows_total = xs.shape[0]
    nb = rows_total // MOE_BLOCK
    depth, ne, d, n_gu = w_gu.shape
    assert xs.shape[1] * 2 == d and xs.dtype == jnp.uint32
    de = w_down.shape[2]
    assert n_gu == d and d % W_CHUNK == 0 and de % W_CHUNK == 0

    def blk(i, *s):
        return jnp.minimum(i, s[-1][0] - 1)

    def bspec(n):
        return pl.BlockSpec((1, 1, 1, n), lambda i, *s: (layer, s[0][blk(i, *s)], 0, 0))

    grid_spec = pltpu.PrefetchScalarGridSpec(
        num_scalar_prefetch=len(sched),
        grid=(nb,),
        in_specs=[
            pl.BlockSpec((MOE_BLOCK, d // 2), lambda i, *s: (blk(i, *s), 0)),
            pl.BlockSpec(memory_space=pl.ANY), bspec(n_gu),
            pl.BlockSpec(memory_space=pl.ANY), bspec(d),
        ],
        out_specs=pl.BlockSpec((MOE_BLOCK, d // 2), lambda i, *s: (i, 0)),
        scratch_shapes=[pltpu.VMEM((2, d, n_gu), BF16), pltpu.VMEM((2, de, d), BF16),
                        pltpu.VMEM((2, W_CHUNK, d), F32), pltpu.SemaphoreType.DMA((2,))],
    )
    return pl.pallas_call(
        functools.partial(_expert_kernel, layer=layer),
        grid_spec=grid_spec,
        out_shape=jax.ShapeDtypeStruct((rows_total, d // 2), jnp.uint32),
        compiler_params=_cparams("arbitrary"),
        name="moe_experts",
    )(*sched, xs, w_gu, b_gu.reshape(depth, ne, 1, n_gu), w_down, b_down.reshape(depth, ne, 1, d))


def _route(logits_t, n_tok):
    experts = jnp.arange(N_EXPERTS, dtype=jnp.int32)
    lg = logits_t
    e_iota = lax.broadcasted_iota(jnp.int32, lg.shape, 0)
    vals, idxs = [], []
    for _ in range(TOP_K):
        mx = jnp.max(lg, axis=0)
        ix = jnp.min(jnp.where(lg == mx[None, :], e_iota, N_EXPERTS), axis=0)
        vals.append(mx)
        idxs.append(ix)
        lg = jnp.where(e_iota == ix[None, :], -jnp.inf, lg)
    top_val = jnp.stack(vals)
    top_idx = jnp.stack(idxs)
    probs = jax.nn.softmax(top_val, axis=0)

    m = n_tok * TOP_K
    onehot = (top_idx[:, :, None] == experts[None, None, :]).astype(jnp.int32)
    per_tok = jnp.sum(onehot, axis=0)
    csum = jnp.cumsum(per_tok, axis=0)
    counts = csum[-1]
    pad_counts = (counts + MOE_BLOCK - 1) // MOE_BLOCK * MOE_BLOCK
    pad_end = jnp.cumsum(pad_counts)
    pad_start = pad_end - pad_counts
    base = csum - per_tok + pad_start[None, :]
    pos = jnp.sum(base[None, :, :] * onehot, axis=2)

    nb = (m + MOE_BLOCK - 1) // MOE_BLOCK + N_EXPERTS
    idx_bits = (m - 1).bit_length()
    assert (2 * N_EXPERTS + 2) << idx_bits < 2 ** 31
    assign = lax.broadcasted_iota(jnp.int32, (TOP_K, n_tok), 1) * TOP_K + \
        lax.broadcasted_iota(jnp.int32, (TOP_K, n_tok), 0)
    real_keys = (((2 * top_idx) << idx_bits) | assign).reshape(m)
    n_fill = nb * MOE_BLOCK - m
    assert n_fill % N_EXPERTS == 0 and n_fill // N_EXPERTS >= MOE_BLOCK - 1
    fill_i = lax.broadcasted_iota(jnp.int32, (N_EXPERTS, n_fill // N_EXPERTS), 1)
    fill_keys = jnp.where(fill_i < (pad_counts - counts)[:, None],
                          ((2 * experts + 1) << idx_bits)[:, None], (2 * N_EXPERTS + 1) << idx_bits)
    keys = lax.sort(jnp.concatenate([real_keys, fill_keys.reshape(n_fill)]))
    is_real = ((keys >> idx_bits) & 1) == 0
    row_tok = jnp.where(is_real, (keys & ((1 << idx_bits) - 1)) // TOP_K,
                        jnp.arange(nb * MOE_BLOCK, dtype=jnp.int32) % n_tok)

    blk_start = jnp.arange(nb, dtype=jnp.int32) * MOE_BLOCK
    blk_e = jnp.minimum(jnp.sum((pad_end[None, :] <= blk_start[:, None]).astype(jnp.int32), axis=1),
                        N_EXPERTS - 1)
    n_used = (pad_end[-1] // MOE_BLOCK).astype(jnp.int32).reshape(1)
    has_blocks = pad_counts > 0
    order_e = jnp.cumsum(has_blocks.astype(jnp.int32)) - 1
    later = jnp.logical_and(has_blocks[None, :], experts[None, :] > experts[:, None])
    next_e = jnp.min(jnp.where(later, experts[None, :], N_EXPERTS), axis=1)
    next_e = jnp.where(next_e == N_EXPERTS, -1, next_e)
    blk_is = blk_e[:, None] == experts[None, :]

    def per_block(table):
        return jnp.sum(jnp.where(blk_is, table[None, :], 0), axis=1)

    blk_j = (blk_start - per_block(pad_start)) // MOE_BLOCK
    blk_n = jnp.maximum(per_block(pad_counts) // MOE_BLOCK, 1)
    sched = (blk_e, blk_j, blk_n, per_block(order_e), per_block(next_e), n_used)
    return probs, pos, row_tok, sched


def _rows(x, idx):
    return x.at[idx].get(mode="promise_in_bounds")


def _moe(h2, tok0, logits_t, layer, w_gu, b_gu, w_down, b_down):
    n_tok = logits_t.shape[1]
    probs, pos, row_tok, sched = _route(logits_t, n_tok)
    xs = _rows(h2, row_tok + tok0)
    yb = _experts(xs, sched, layer, w_gu, b_gu, w_down, b_down)
    y = _rows(yb, pos.reshape(-1)).reshape(TOP_K, n_tok, -1)
    return y, probs.T


def _combine_kernel(x_ref, y0_ref, y1_ref, y2_ref, y3_ref, p_ref, g2_ref, nf_ref, o_ref,
                    *, tile0, final, n_ctx_tiles, tiles_per_batch, ctx_row):
    row = _mod_row(pl.program_id(0) + tile0, n_ctx_tiles, tiles_per_batch, ctx_row)
    p = p_ref[...]

    def expert_out(y_ref):
        u = y_ref[0]
        return jnp.concatenate([lax.bitcast_convert_type(u << 16, F32),
                                lax.bitcast_convert_type(u & jnp.uint32(0xFFFF0000), F32)], axis=1)

    y = (p[:, 0:1] * expert_out(y0_ref) + p[:, 1:2] * expert_out(y1_ref)) + \
        (p[:, 2:3] * expert_out(y2_ref) + p[:, 3:4] * expert_out(y3_ref))
    x = x_ref[...] + g2_ref[pl.ds(row, 1), :] * y
    if final:
        x = x * lax.rsqrt(jnp.mean(x * x, axis=-1, keepdims=True) + EPS) * nf_ref[...]
    o_ref[...] = x


def _combine(x, y, probs, mod, norm_f, *, tile0, final, n_ctx_tiles, tiles_per_batch, ctx_row):
    assert y.shape[0] == TOP_K == 4 and y.dtype == jnp.uint32
    n_rows = y.shape[1]
    d = x.shape[1]
    tm = ROW_TILE
    kern = functools.partial(_combine_kernel, tile0=tile0, final=final, n_ctx_tiles=n_ctx_tiles,
                             tiles_per_batch=tiles_per_batch, ctx_row=ctx_row)

    def yspec(k):
        return pl.BlockSpec((1, tm, d // 2), lambda i: (k, i, 0))

    return pl.pallas_call(
        kern,
        grid=(n_rows // tm,),
        in_specs=[pl.BlockSpec((tm, d), lambda i: (tile0 + i, 0)),
                  yspec(0), yspec(1), yspec(2), yspec(3),
                  pl.BlockSpec((tm, TOP_K), lambda i: (i, 0)),
                  pl.BlockSpec((8, d), lambda i: (0, 5)),
                  pl.BlockSpec((1, d), lambda i: (0, 0))],
        out_specs=pl.BlockSpec((tm, d), lambda i: (i, 0)),
        out_shape=jax.ShapeDtypeStruct((n_rows, d), F32),
        compiler_params=_cparams("parallel"),
        name="moe_combine",
    )(x, y, y, y, y, probs, mod, norm_f)


def _embed_kernel(ctx_ref, x_ref, pos_ref, o_ref, *, n_ctx_tiles):
    j = pl.program_id(0)
    b = pl.program_id(1)

    @pl.when(jnp.logical_and(j == 0, b < n_ctx_tiles))
    def _():
        o_ref[...] = ctx_ref[...]

    @pl.when(j > 0)
    def _():
        o_ref[...] = x_ref[...] + pos_ref[...]


def _embed(ctx2, x2, pos, *, batch):
    d = x2.shape[1]
    l = pos.shape[0]
    tm = math.gcd(math.gcd(ctx2.shape[0], l), 2 * ROW_TILE)
    n_ctx_tiles = ctx2.shape[0] // tm
    nt = l // tm
    assert n_ctx_tiles <= batch
    return pl.pallas_call(
        functools.partial(_embed_kernel, n_ctx_tiles=n_ctx_tiles),
        grid=(1 + nt, batch),
        in_specs=[pl.BlockSpec((tm, d), lambda j, b: (jnp.minimum(b, n_ctx_tiles - 1), 0)),
                  pl.BlockSpec((tm, d), lambda j, b: (b * nt + jnp.maximum(j - 1, 0), 0)),
                  pl.BlockSpec((tm, d), lambda j, b: (jnp.maximum(j - 1, 0), 0))],
        out_specs=pl.BlockSpec(
            (tm, d), lambda j, b: (jnp.where(j == 0, jnp.minimum(b, n_ctx_tiles - 1),
                                             n_ctx_tiles + b * nt + j - 1), 0)),
        out_shape=jax.ShapeDtypeStruct((ctx2.shape[0] + x2.shape[0], d), F32),
        compiler_params=_cparams("arbitrary", "arbitrary"),
        name="embed_tokens",
    )(ctx2, x2, pos)


def _grid_pos_embed(n, d):
    rows = n // GRID_W
    quarter = d // 4
    omega = 1.0 / (10000.0 ** (jnp.arange(quarter, dtype=F32) / quarter))

    def enc(count):
        a = jnp.arange(count, dtype=F32)[:, None] * omega
        return jnp.concatenate([jnp.sin(a), jnp.cos(a)], axis=-1)

    row_part = jnp.repeat(enc(rows), GRID_W, axis=0)
    col_part = jnp.tile(enc(GRID_W), (rows, 1))
    return jnp.concatenate([row_part, col_part], axis=-1)


def kernel(x, c, ctx, c_ctx, w_mod, b_mod, norm1, w_in, w_a2, b_a, gla_norm, w_s, b_s, p_gla, p_sg, p_ft,
           w_out, norm2, w_router, b_router, w_gu, b_gu, w_down, b_down, norm_f):
    batch, l, d = x.shape
    lc = ctx.shape[1]
    depth = w_mod.shape[0]
    n_ctx_rows = batch * lc
    t_rows = n_ctx_rows + batch * l
    assert lc % ROW_TILE == 0 and l % ROW_TILE == 0 and batch < 8

    tm1 = math.gcd(math.gcd(n_ctx_rows, l), 1024)
    assert tm1 % ROW_TILE == 0
    proj_tiles = dict(n_ctx_tiles=n_ctx_rows // tm1, tiles_per_batch=l // tm1, ctx_row=batch)
    row_tiles = dict(n_ctx_tiles=n_ctx_rows // ROW_TILE, tiles_per_batch=l // ROW_TILE, ctx_row=batch)
    tm2 = ROW_TILE
    tm3 = math.gcd(tm1, 2 * ROW_TILE)
    big_tiles = dict(n_ctx_tiles=n_ctx_rows // tm3, tiles_per_batch=l // tm3, ctx_row=batch)

    o_q = 0
    o_k = o_q + W_Q
    o_v = o_k + W_Q
    o_r = o_v + W_V
    o_a = o_r + W_V
    o_su = o_a + 2 * GLA_RANK
    o_sv = o_su + W_SG
    o_f = o_sv + W_SG
    o_g = o_f + W_FT
    n_gates = 3 * d
    cols = dict(gates=0, q=n_gates, r=n_gates + 2 * W_Q + W_V, su=n_gates + 2 * W_Q + 2 * W_V)
    cols["sv"] = cols["su"] + W_SG
    cols["f"] = cols["sv"] + W_SG
    n_main = cols["f"] + W_FT

    xt = _embed(ctx.reshape(n_ctx_rows, d), x.reshape(batch * l, d), _grid_pos_embed(l, d), batch=batch)

    cond = jnp.zeros((8, d), F32).at[:batch].set(c).at[batch].set(c_ctx)
    mods = _mod_vectors(cond, w_mod, b_mod)

    cc, sc = _dft_tables(FT_CH)
    chan_tab = jnp.concatenate([cc, sc], axis=1)
    assert l % (FT_RADIX * FT_TILE) == 0
    ft_tables = _ft_long_tables(l)
    cos_c, sin_c = _dft_tables(lc)

    w_main = jnp.concatenate(
        [0.5 * w_in[:, :, o_g:], w_in[:, :, o_q:o_a], w_in[:, :, o_su:o_g]], axis=2).astype(BF16)
    w_alr = jnp.pad(w_in[:, :, o_a:o_su], ((0, 0), (0, 0), (0, LANES - 2 * GLA_RANK))).astype(BF16)

    for i in range(depth):
        last = i == depth - 1
        w_a = w_alr[i]
        wa2 = jnp.zeros((LANES, 2 * W_Q), F32)
        wa2 = wa2.at[:GLA_RANK, :W_Q].set(w_a2[i, 0]).at[GLA_RANK:2 * GLA_RANK, W_Q:].set(w_a2[i, 1])
        mod = mods[i]

        z, la = _project(xt, mod, norm1[i][None], w_main, w_a, wa2.astype(BF16), b_a[i].reshape(1, 2 * W_Q),
                         layer=i, tm=tm1, tn=1536, **proj_tiles)
        assert z.shape[1] == n_main

        o_fw, o_bw = _gla(z, la, batch=batch, lc=lc, l=l, col_q=cols["q"])

        xc_c, xs_c = _ft_channel(z, chan_tab, col_f=cols["f"], row0=0, n_rows=n_ctx_rows)
        xc_l, xs_l = _ft_channel(z, chan_tab, col_f=cols["f"], row0=n_ctx_rows, n_rows=batch * l)
        y_ft = (_ft_positions(cos_c, sin_c, xc_c, xs_c, batch=batch, seq=lc),
                _ft_positions_long(ft_tables, xc_l, xs_l, batch=batch, seq=l))

        bs_b = jnp.broadcast_to(b_s[i][:, :, None], (SG_GROUPS, SG_CHUNK, SG_CH))
        merged = _branches(z, o_fw, o_bw, y_ft, gla_norm[i][None], w_s[i].astype(BF16), bs_b,
                           p_gla[i].astype(BF16), p_sg[i].astype(BF16), p_ft[i].astype(BF16), cols=cols, tm=tm2)

        w_r = jnp.pad(w_router[i].T, ((0, LANES - N_EXPERTS), (0, 0)))
        b_r = jnp.pad(b_router[i], (0, LANES - N_EXPERTS)).reshape(LANES, 1)
        xt, h2, logits_t = _out_project(merged, xt, w_out[i].astype(BF16), mod, norm2[i][None], w_r, b_r,
                                      tm=tm3, **big_tiles)

        row0 = n_ctx_rows if last else 0
        y, probs = _moe(h2, row0, logits_t[:N_EXPERTS, row0:], i, w_gu, b_gu, w_down, b_down)
        xt = _combine(xt, y, probs, mod, norm_f[None], tile0=row0 // ROW_TILE, final=last, **row_tiles)

    return xt.reshape(batch, l, d)
```

```python
import functools
import math

import jax
import jax.numpy as jnp
from jax import lax
from jax.experimental import pallas as pl
from jax.experimental.pallas import tpu as pltpu

F32 = jnp.float32
BF16 = jnp.bfloat16

EPS = 1e-6
GRID_W = 64
GLA_HEADS = 4
GLA_DK = 128
GLA_DV = 256
GLA_RANK = 16
GLA_TAU = 16.0
GLA_CHUNK = 64
GLA_SUB = 8
GLA_STAGES = 3
SG_GROUPS = 4
SG_CH = 128
SG_CHUNK = 128
FT_GROUPS = 4
FT_CH = 128
N_EXPERTS = 32
TOP_K = 4
SWIGLU_LIMIT = 7.0
SWIGLU_ALPHA = 1.702
MOE_BLOCK = 256
W_CHUNK = 512

LOG2E = math.log2(math.e)
LANES = 128
ROW_TILE = 256
OUT_SUB_ROWS = 256
VMEM_LIMIT = 56 * 1024 * 1024

W_Q = GLA_HEADS * GLA_DK
W_V = GLA_HEADS * GLA_DV
W_SG = SG_GROUPS * SG_CH
W_FT = FT_GROUPS * FT_CH


def _cparams(*sem):
    return pltpu.CompilerParams(dimension_semantics=sem, vmem_limit_bytes=VMEM_LIMIT)


def _sigmoid(x):
    return 1.0 / (1.0 + jnp.exp(-x))


def _half_gated(half_gate, proj):
    return proj + jnp.tanh(half_gate.astype(F32)) * proj


def _gelu_tanh(x):
    return 0.5 * x * (1.0 + jnp.tanh(math.sqrt(2.0 / math.pi) * (x + 0.044715 * (x * x * x))))


def _dot(a, b):
    return jnp.dot(a, b, preferred_element_type=F32)


def _pack_bf16_pairs(x):
    bits = lax.bitcast_convert_type(x.astype(F32), jnp.uint32)
    half = x.shape[1] // 2
    return (bits[:, :half] >> 16) | (bits[:, half:] & jnp.uint32(0xFFFF0000))


def _unpack_bf16_pairs(u):
    lo = lax.bitcast_convert_type(u << 16, F32)
    hi = lax.bitcast_convert_type(u & jnp.uint32(0xFFFF0000), F32)
    return jnp.concatenate([lo, hi], axis=1).astype(BF16)


def _mod_row(tile, n_ctx_tiles, tiles_per_batch, ctx_row):
    return jnp.where(tile < n_ctx_tiles, ctx_row, (tile - n_ctx_tiles) // tiles_per_batch)


def _mod_kernel(c_ref, w_ref, b_ref, o_ref):
    c = c_ref[...]
    s = (c * _sigmoid(c)).astype(BF16)
    o_ref[0] = _dot(s, w_ref[0].astype(BF16)) + b_ref[0]


def _mod_vectors(cond, w_mod, b_mod):
    depth, d, n = w_mod.shape
    tn = 1024
    return pl.pallas_call(
        _mod_kernel,
        grid=(depth, n // tn),
        in_specs=[
            pl.BlockSpec((8, d), lambda l, j: (0, 0)),
            pl.BlockSpec((1, d, tn), lambda l, j: (l, 0, j)),
            pl.BlockSpec((1, 1, tn), lambda l, j: (l, 0, j)),
        ],
        out_specs=pl.BlockSpec((1, 8, tn), lambda l, j: (l, 0, j)),
        out_shape=jax.ShapeDtypeStruct((depth, 8, n), F32),
        compiler_params=_cparams("parallel", "parallel"),
        name="mod_vectors",
    )(cond, w_mod, b_mod.reshape(depth, 1, n))


def _proj_kernel(x_ref, sh_ref, sc_ref, g_ref, w_ref, wa_ref, wa2_ref, ba_ref,
                 z_ref, la_ref, a_scr, *, n_gate_tiles, n_ctx_tiles, tiles_per_batch, ctx_row):
    i = pl.program_id(0)
    j = pl.program_id(1)

    @pl.when(j == 0)
    def _():
        row = _mod_row(i, n_ctx_tiles, tiles_per_batch, ctx_row)
        gain = g_ref[...] * (1.0 + sc_ref[pl.ds(row, 1), :])
        shift = sh_ref[pl.ds(row, 1), :]

        def body(r, carry):
            rows = pl.ds(pl.multiple_of(r * ROW_TILE, ROW_TILE), ROW_TILE)
            x = x_ref[rows, :]
            y = x * lax.rsqrt(jnp.mean(x * x, axis=-1, keepdims=True) + EPS)
            hb = (y * gain + shift).astype(BF16)
            a_scr[rows, :] = hb
            a_lr = _dot(hb, wa_ref[...])
            pre = _dot(a_lr.astype(BF16), wa2_ref[...]) + ba_ref[...]
            la_ref[rows, :] = -(jnp.maximum(-pre, 0.0) + jnp.log(1.0 + jnp.exp(-jnp.abs(pre)))) * (1.0 / GLA_TAU)
            return carry

        lax.fori_loop(0, x_ref.shape[0] // ROW_TILE, body, 0)

    col_scale = jnp.where(j < n_gate_tiles, 0.5, 1.0)
    z_ref[...] = (_dot(a_scr[...], w_ref[0]) * col_scale).astype(BF16)


def _project(x, mod, g, w_main, w_a, w_a2, b_a, *, layer, tm, tn, n_gate_cols, n_ctx_tiles, tiles_per_batch,
             ctx_row):
    t, d = x.shape
    n = w_main.shape[2]
    assert n_gate_cols % tn == 0
    kern = functools.partial(_proj_kernel, n_gate_tiles=n_gate_cols // tn, n_ctx_tiles=n_ctx_tiles,
                             tiles_per_batch=tiles_per_batch, ctx_row=ctx_row)
    return pl.pallas_call(
        kern,
        grid=(t // tm, n // tn),
        in_specs=[
            pl.BlockSpec((tm, d), lambda i, j: (i, 0)),
            pl.BlockSpec((8, d), lambda i, j: (0, 0)),
            pl.BlockSpec((8, d), lambda i, j: (0, 1)),
            pl.BlockSpec((1, d), lambda i, j: (0, 0)),
            pl.BlockSpec((1, d, tn), lambda i, j: (layer, 0, j)),
            pl.BlockSpec((d, LANES), lambda i, j: (0, 0)),
            pl.BlockSpec((LANES, 2 * W_Q), lambda i, j: (0, 0)),
            pl.BlockSpec((1, 2 * W_Q), lambda i, j: (0, 0)),
        ],
        out_specs=[
            pl.BlockSpec((tm, tn), lambda i, j: (i, j)),
            pl.BlockSpec((tm, 2 * W_Q), lambda i, j: (i, 0)),
        ],
        out_shape=[
            jax.ShapeDtypeStruct((t, n), BF16),
            jax.ShapeDtypeStruct((t, 2 * W_Q), F32),
        ],
        scratch_shapes=[pltpu.VMEM((tm, d), BF16)],
        compiler_params=_cparams("parallel", "arbitrary"),
        name="norm_mod_project",
    )(x, mod, mod, g, w_main, w_a, w_a2, b_a)


def _gla_pairs(rev):
    c = GLA_CHUNK
    pairs = []
    s = c // 2
    while s >= GLA_SUB:
        for b0 in range(0, c, 2 * s):
            if rev:
                pairs.append((b0, b0 + s, b0 + s, b0 + 2 * s, b0 + s - 1))
            else:
                pairs.append((b0 + s, b0 + 2 * s, b0, b0 + s, b0 + s))
        s //= 2
    return tuple(pairs)


def _place_rows(x, r0, n):
    parts = []
    if r0:
        parts.append(jnp.zeros((r0, x.shape[1]), x.dtype))
    parts.append(x)
    rest = n - r0 - x.shape[0]
    if rest:
        parts.append(jnp.zeros((rest, x.shape[1]), x.dtype))
    return jnp.concatenate(parts, axis=0)


def _gla_chunk(q_ref, k_ref, v_ref, la_ref, e_ref, o_ref, s_ref, cum_ref, kc_ref, r0, h, rev):
    c, sub = GLA_CHUNK, GLA_SUB
    nsub = c // sub
    rows = pl.ds(r0, c)
    kcols = slice(h * GLA_DK, (h + 1) * GLA_DK)
    vcols = slice(h * GLA_DV, (h + 1) * GLA_DV)
    q = q_ref[rows, kcols].astype(F32) * (GLA_DK ** -0.5)
    k = k_ref[rows, kcols].astype(F32)
    v = v_ref[rows, vcols]
    la = la_ref[rows, kcols]

    ri = lax.broadcasted_iota(jnp.int32, (c, c), 0)
    ci = lax.broadcasted_iota(jnp.int32, (c, c), 1)
    before = (ci >= ri) if rev else (ci <= ri)
    tri = jnp.where(before, 1.0, 0.0).astype(BF16)
    la_hi = la.astype(BF16)
    rem = la - la_hi.astype(F32)
    la_mid = rem.astype(BF16)
    la_lo = (rem - la_mid.astype(F32)).astype(BF16)
    cum = (_dot(tri, la_hi) + _dot(tri, la_mid) + _dot(tri, la_lo)) * LOG2E
    cum_ref[...] = cum
    kc_ref[...] = k
    yield
    last = 0 if rev else c - 1
    total = cum[last:last + 1, :]

    s = s_ref[...]
    o = _dot((q * jnp.exp2(cum)).astype(BF16), s.astype(BF16))

    qparts, kparts = [], []
    for l0, l1, e0, e1, ref in _gla_pairs(rev):
        cr = cum[ref:ref + 1, :]
        qparts.append(_place_rows(q[l0:l1] * jnp.exp2(cum[l0:l1] - cr), l0, c))
        kparts.append(_place_rows(k[e0:e1] * jnp.exp2(cr - cum[e0:e1]), e0, c))
    s_off = lax.dot_general(jnp.concatenate(qparts, axis=1).astype(BF16),
                            jnp.concatenate(kparts, axis=1).astype(BF16),
                            (((1,), (1,)), ((), ())), preferred_element_type=F32)

    hs = sub // 2
    skip_halves = hs % 8 == 0

    def halves(x, which):
        if len(which) == 2:
            return x
        return jnp.concatenate([x[a * sub + w * hs:a * sub + (w + 1) * hs] for a in range(nsub) for w in which],
                               axis=0)

    parts = []
    for jj in range(sub):
        if not skip_halves:
            which = (0, 1)
        elif rev:
            which = (0, 1) if jj >= hs else (0,)
        else:
            which = (0, 1) if jj < hs else (1,)
        nrow = hs * len(which)
        cj = jnp.concatenate(
            [jnp.broadcast_to(cum_ref[a * sub + jj:a * sub + jj + 1, :], (nrow, GLA_DK)) for a in range(nsub)],
            axis=0)
        kj = jnp.concatenate(
            [jnp.broadcast_to(kc_ref[a * sub + jj:a * sub + jj + 1, :], (nrow, GLA_DK)) for a in range(nsub)],
            axis=0)
        prod = halves(q, which) * jnp.exp2(jnp.minimum(halves(cum, which) - cj, 0.0)) * kj
        if len(which) == 1:
            zero = jnp.zeros((hs, GLA_DK), F32)
            pieces = []
            for a in range(nsub):
                live = prod[a * hs:(a + 1) * hs]
                pieces += [live, zero] if which[0] == 0 else [zero, live]
            prod = jnp.concatenate(pieces, axis=0)
        parts.append(prod.astype(BF16))
    diag = _dot(jnp.concatenate(parts, axis=1), e_ref[...])
    yield

    lag = (ci - ri) if rev else (ri - ci)
    room = (sub - 1 - ri % sub) if rev else (ri % sub)
    in_diag = lag.astype(jnp.uint32) <= room.astype(jnp.uint32)
    p = jnp.where(in_diag, diag, 0.0) + s_off
    o_ref[rows, vcols] = o + _dot(p.astype(BF16), v)

    k_out = (k * jnp.exp2(total - cum)).astype(BF16)
    upd = lax.dot_general(k_out, v, (((0,), (0,)), ((), ())), preferred_element_type=F32)
    dmat = jnp.broadcast_to(jnp.exp2(total), (GLA_DK, GLA_DK)).T
    for half in range(GLA_DV // GLA_DK):
        cols = slice(half * GLA_DK, (half + 1) * GLA_DK)
        s_ref[:, cols] = s[:, cols] * dmat + upd[:, cols]


def _gla_kernel(qf, kf, vf, laf, qb, kb, vb, lab, e_ref, of_ref, ob_ref, *scratch):
    n = 2 * GLA_HEADS
    states, cums, keys = scratch[:n], scratch[n:2 * n], scratch[2 * n:]

    @pl.when(pl.program_id(1) == 0)
    def _():
        for s_ref in states:
            s_ref[...] = jnp.zeros_like(s_ref)

    nchunk = ROW_TILE // GLA_CHUNK

    def body(c, carry):
        r_f = pl.multiple_of(c * GLA_CHUNK, GLA_CHUNK)
        r_b = pl.multiple_of((nchunk - 1 - c) * GLA_CHUNK, GLA_CHUNK)
        units = []
        for h in range(GLA_HEADS):
            f, b = h, GLA_HEADS + h
            units.append(_gla_chunk(qf, kf, vf, laf, e_ref, of_ref, states[f], cums[f], keys[f], r_f, h, False))
            units.append(_gla_chunk(qb, kb, vb, lab, e_ref, ob_ref, states[b], cums[b], keys[b], r_b, h, True))
        for _ in range(GLA_STAGES):
            for u in units:
                next(u, None)
        return carry

    lax.fori_loop(0, nchunk, body, 0)


def _gla(z, la, *, batch, lc, l, col_q):
    t_rows = z.shape[0]
    nctx = lc // ROW_TILE
    nlat = l // ROW_TILE
    steps = nctx + nlat
    qb0 = col_q // W_Q
    kb0 = (col_q + W_Q) // W_Q
    vb0 = (col_q + 2 * W_Q) // W_V

    def fwd_tile(b, t):
        return jnp.where(t < nctx, b * nctx + t, batch * nctx + b * nlat + (t - nctx))

    def bwd_tile(b, t):
        return jnp.where(t < nctx, b * nctx + (nctx - 1 - t), batch * nctx + b * nlat + (steps - 1 - t))

    def spec(width, tile_fn, col):
        return pl.BlockSpec((ROW_TILE, width), lambda b, t: (tile_fn(b, t), col))

    kk = jnp.arange(GLA_SUB * GLA_DK, dtype=jnp.int32)[:, None] // GLA_DK
    cc = jnp.arange(GLA_CHUNK, dtype=jnp.int32)[None, :] % GLA_SUB
    e_mat = (kk == cc).astype(BF16)

    o_shape = jax.ShapeDtypeStruct((t_rows, W_V), F32)
    n_units = 2 * GLA_HEADS
    state = pltpu.VMEM((GLA_DK, GLA_DV), F32)
    chunk = pltpu.VMEM((GLA_CHUNK, GLA_DK), F32)
    return pl.pallas_call(
        _gla_kernel,
        grid=(batch, steps),
        in_specs=[
            spec(W_Q, fwd_tile, qb0), spec(W_Q, fwd_tile, kb0), spec(W_V, fwd_tile, vb0), spec(W_Q, fwd_tile, 0),
            spec(W_Q, bwd_tile, qb0), spec(W_Q, bwd_tile, kb0), spec(W_V, bwd_tile, vb0), spec(W_Q, bwd_tile, 1),
            pl.BlockSpec((GLA_SUB * GLA_DK, GLA_CHUNK), lambda b, t: (0, 0)),
        ],
        out_specs=[spec(W_V, fwd_tile, 0), spec(W_V, bwd_tile, 0)],
        out_shape=[o_shape, o_shape],
        scratch_shapes=[state] * n_units + [chunk] * (2 * n_units),
        compiler_params=_cparams("parallel", "arbitrary"),
        name="gla_scan",
    )(z, z, z, la, z, z, z, la, e_mat)


def _dft_tables(n):
    scale = n ** -0.5
    p = jnp.arange(n, dtype=jnp.int32)[:, None]

    def trig(cols, period):
        ang = ((p * cols[None, :]) % period).astype(F32) * (2.0 * math.pi / period)
        return jnp.cos(ang), jnp.sin(ang)

    n2 = 1
    while n2 * n2 < n:
        n2 *= 2
    if n % n2 or n2 == 1:
        c, s = trig(jnp.arange(n, dtype=jnp.int32), n)
        return (c * scale).astype(BF16), (s * scale).astype(BF16)
    n1 = n // n2
    ca, sa = trig(jnp.arange(n1, dtype=jnp.int32), n1)
    cb, sb = trig(jnp.arange(n2, dtype=jnp.int32), n)
    ca, sa, cb, sb = ca[:, :, None], sa[:, :, None], cb[:, None, :] * scale, sb[:, None, :] * scale
    return ((ca * cb - sa * sb).reshape(n, n).astype(BF16), (sa * cb + ca * sb).reshape(n, n).astype(BF16))


def _ft_chan_kernel(f_ref, cs_ref, xc_ref, xs_ref):
    for g in range(FT_GROUPS):
        cols = slice(g * FT_CH, (g + 1) * FT_CH)
        y = _dot(f_ref[:, cols], cs_ref[...])
        xc_ref[:, cols] = y[:, :FT_CH].astype(BF16)
        xs_ref[:, cols] = y[:, FT_CH:].astype(BF16)


def _ft_channel(z, cs, *, col_f, row0, n_rows):
    tm = math.gcd(math.gcd(row0, n_rows), 1024)
    shape = jax.ShapeDtypeStruct((n_rows, W_FT), BF16)
    return pl.pallas_call(
        _ft_chan_kernel,
        grid=(n_rows // tm,),
        in_specs=[pl.BlockSpec((tm, W_FT), lambda i: (row0 // tm + i, col_f // W_FT)),
                  pl.BlockSpec((FT_CH, 2 * FT_CH), lambda i: (0, 0))],
        out_specs=[pl.BlockSpec((tm, W_FT), lambda i: (i, 0))] * 2,
        out_shape=[shape, shape],
        compiler_params=_cparams("parallel"),
        name="fourier_channels",
    )(z, cs)


def _ft_pos_kernel(c_ref, s_ref, xc_ref, xs_ref, o_ref):
    o_ref[...] = _dot(c_ref[...], xc_ref[...]) - _dot(s_ref[...], xs_ref[...])


def _ft_positions(cos_t, sin_t, xc, xs, *, batch, seq):
    nt = seq // ROW_TILE
    x_spec = pl.BlockSpec((seq, W_FT), lambda b, j: (b, 0))
    t_spec = pl.BlockSpec((ROW_TILE, seq), lambda b, j: (j, 0))
    return pl.pallas_call(
        _ft_pos_kernel,
        grid=(batch, nt),
        in_specs=[t_spec, t_spec, x_spec, x_spec],
        out_specs=pl.BlockSpec((ROW_TILE, W_FT), lambda b, j: (b * nt + j, 0)),
        out_shape=jax.ShapeDtypeStruct((batch * seq, W_FT), F32),
        compiler_params=_cparams("parallel", "parallel"),
        name="fourier_positions",
    )(cos_t, sin_t, xc, xs)


FT_RADIX = 16
FT_TILE = 16


def _ft_stage1_kernel(xc_ref, xs_ref, k_ref, tc_ref, ts_ref, o_ref):
    rows = FT_RADIX * FT_TILE
    zin = jnp.concatenate([xc_ref[0].reshape(rows, W_FT), xs_ref[0].reshape(rows, W_FT)], axis=0)
    a = _dot(k_ref[...], zin)
    re, im = a[:rows], a[rows:]
    tc, ts = tc_ref[0], ts_ref[0]
    o_ref[0, 0] = (re * tc + im * ts).astype(BF16).reshape(FT_RADIX, FT_TILE, W_FT)
    o_ref[0, 1] = (im * tc - re * ts).astype(BF16).reshape(FT_RADIX, FT_TILE, W_FT)


def _ft_stage2_kernel(b_ref, t_ref, o_ref):
    rhs = jnp.concatenate([b_ref[0, 0, 0], b_ref[0, 1, 0]], axis=0)
    o_ref[0, 0] = _dot(t_ref[...], rhs)


def _ft_long_tables(seq):
    r, n1 = FT_RADIX, seq // FT_RADIX

    def trig(rows, cols, period):
        ang = ((rows[:, None] * cols[None, :]) % period).astype(F32) * (2.0 * math.pi / period)
        return jnp.cos(ang), jnp.sin(ang)

    ir = jnp.arange(r, dtype=jnp.int32)
    i1 = jnp.arange(n1, dtype=jnp.int32)
    c1, s1 = trig(ir, ir, r)
    t1 = jnp.concatenate([jnp.concatenate([c1, -s1], axis=1), jnp.concatenate([-s1, -c1], axis=1)], axis=0)
    k1 = jnp.kron(t1 * r ** -0.5, jnp.eye(FT_TILE, dtype=F32)).astype(BF16)
    tc, ts = trig(ir, i1, seq)

    def per_tile(t):
        return t.reshape(r, n1 // FT_TILE, FT_TILE).transpose(1, 0, 2).reshape(n1 // FT_TILE, r * FT_TILE, 1)

    c2, s2 = trig(i1, i1, n1)
    t2 = (jnp.concatenate([c2, s2], axis=1) * n1 ** -0.5).astype(BF16)
    return k1, per_tile(tc), per_tile(ts), t2


def _ft_positions_long(tables, xc, xs, *, batch, seq):
    k1, tc, ts, t2 = tables
    r, n1 = FT_RADIX, seq // FT_RADIX
    nt = n1 // FT_TILE
    rows = r * FT_TILE
    x4 = (batch, r, n1, W_FT)
    x_spec = pl.BlockSpec((1, r, FT_TILE, W_FT), lambda b, t: (b, 0, t, 0))
    tw_spec = pl.BlockSpec((1, rows, 1), lambda b, t: (t, 0, 0))
    stage1 = pl.pallas_call(
        _ft_stage1_kernel,
        grid=(batch, nt),
        in_specs=[x_spec, x_spec, pl.BlockSpec((2 * rows, 2 * rows), lambda b, t: (0, 0)), tw_spec, tw_spec],
        out_specs=pl.BlockSpec((1, 2, r, FT_TILE, W_FT), lambda b, t: (b, 0, 0, t, 0)),
        out_shape=jax.ShapeDtypeStruct((batch, 2, r, n1, W_FT), BF16),
        compiler_params=_cparams("parallel", "parallel"),
        name="fourier_stage1",
    )(xc.reshape(x4), xs.reshape(x4), k1, tc, ts)
    stage2 = pl.pallas_call(
        _ft_stage2_kernel,
        grid=(batch, r),
        in_specs=[pl.BlockSpec((1, 2, 1, n1, W_FT), lambda b, p: (b, 0, p, 0, 0)),
                  pl.BlockSpec((n1, 2 * n1), lambda b, p: (0, 0))],
        out_specs=pl.BlockSpec((1, 1, n1, W_FT), lambda b, p: (b, p, 0, 0)),
        out_shape=jax.ShapeDtypeStruct((batch, r, n1, W_FT), F32),
        compiler_params=_cparams("parallel", "parallel"),
        name="fourier_stage2",
    )(stage1, t2)
    return stage2.transpose(0, 2, 1, 3).reshape(batch * seq, W_FT)


def _branch_kernel(gg_ref, gs_ref, gf_ref, of_ref, ob_ref, r_ref, su_ref, sv_ref, yc_ref, yl_ref,
                   gn_ref, ws_ref, bs_ref, pg_ref, ps_ref, pf_ref, m_ref, *, n_ctx_tiles):
    o = of_ref[...] + ob_ref[...]
    heads = []
    for h in range(GLA_HEADS):
        oh = o[:, h * GLA_DV:(h + 1) * GLA_DV]
        heads.append(oh * lax.rsqrt(jnp.mean(oh * oh, axis=-1, keepdims=True) + EPS))
    r = r_ref[...].astype(F32)
    half_r = 0.5 * r
    silu_r = half_r + half_r * jnp.tanh(half_r)
    y_gla = (jnp.concatenate(heads, axis=-1) * gn_ref[...]) * silu_r
    y_gla = y_gla.astype(BF16)

    u = _gelu_tanh(su_ref[...].astype(F32))
    v = _gelu_tanh(sv_ref[...].astype(F32))
    groups = []
    for g in range(SG_GROUPS):
        vg = v[:, g * SG_CH:(g + 1) * SG_CH]
        mu = jnp.mean(vg, axis=-1, keepdims=True)
        var = jnp.mean(jnp.square(vg - mu), axis=-1, keepdims=True)
        vn = ((vg - mu) * lax.rsqrt(var + EPS)).astype(BF16)
        chunks = []
        for c in range(v.shape[0] // SG_CHUNK):
            chunks.append(_dot(ws_ref[g], vn[c * SG_CHUNK:(c + 1) * SG_CHUNK, :]) + bs_ref[g])
        groups.append(jnp.concatenate(chunks, axis=0))
    y_sg = u * jnp.concatenate(groups, axis=-1)
    y_sg = y_sg.astype(BF16)

    y_ft = jnp.where(pl.program_id(0) < n_ctx_tiles, yc_ref[...], yl_ref[...])
    y_ft = y_ft.astype(BF16)

    acc = _half_gated(gg_ref[...], _dot(y_gla, pg_ref[...]))
    acc = acc + _half_gated(gs_ref[...], _dot(y_sg, ps_ref[...]))
    acc = acc + _half_gated(gf_ref[...], _dot(y_ft, pf_ref[...]))
    m_ref[...] = (0.5 * acc).astype(BF16)


def _branches(z, o_f, o_b, y_ft, gla_norm, w_s, b_s, p_gla, p_sg, p_ft, *, cols, tm):
    t_rows = z.shape[0]
    d = p_gla.shape[1]
    n_ctx_tiles = y_ft[0].shape[0] // tm
    kern = functools.partial(_branch_kernel, n_ctx_tiles=n_ctx_tiles)

    def zspec(width, col0):
        return pl.BlockSpec((tm, width), lambda i: (i, col0 // width))

    def rows(width):
        return pl.BlockSpec((tm, width), lambda i: (i, 0))

    def const(shape):
        return pl.BlockSpec(shape, lambda i: (0,) * len(shape))

    return pl.pallas_call(
        kern,
        grid=(t_rows // tm,),
        in_specs=[
            zspec(d, cols["gates"]), zspec(d, cols["gates"] + d), zspec(d, cols["gates"] + 2 * d),
            rows(W_V), rows(W_V), zspec(W_V, cols["r"]), zspec(W_SG, cols["su"]), zspec(W_SG, cols["sv"]),
            pl.BlockSpec((tm, W_FT), lambda i: (jnp.minimum(i, n_ctx_tiles - 1), 0)),
            pl.BlockSpec((tm, W_FT), lambda i: (jnp.maximum(i - n_ctx_tiles, 0), 0)),
            const((1, W_V)), const((SG_GROUPS, SG_CHUNK, SG_CHUNK)), const((SG_GROUPS, SG_CHUNK, SG_CH)),
            const((W_V, d)), const((W_SG, d)), const((W_FT, d)),
        ],
        out_specs=rows(d),
        out_shape=jax.ShapeDtypeStruct((t_rows, d), BF16),
        compiler_params=_cparams("parallel"),
        name="branch_merge",
    )(z, z, z, o_f, o_b, z, z, z, y_ft[0], y_ft[1], gla_norm, w_s, b_s, p_gla, p_sg, p_ft)


def _outproj_kernel(m_ref, x_ref, w_ref, g1_ref, n2_ref, sh_ref, sc_ref, wr_ref, br_ref,
                    xo_ref, h_ref, lg_ref, *, n_ctx_tiles, tiles_per_batch, ctx_row):
    row = _mod_row(pl.program_id(0), n_ctx_tiles, tiles_per_batch, ctx_row)
    gate1 = g1_ref[pl.ds(row, 1), :]
    scale2 = 1.0 + sc_ref[pl.ds(row, 1), :]
    shift2 = sh_ref[pl.ds(row, 1), :]
    w = wr_ref[...]
    w_hi = w.astype(BF16)
    w_lo = (w - w_hi.astype(F32)).astype(BF16)
    w_both = jnp.concatenate([w_hi, w_lo], axis=0)
    contract_last = (((1,), (1,)), ((), ()))
    sub = OUT_SUB_ROWS
    starts = list(range(0, x_ref.shape[0], sub))
    proj = {starts[0]: _dot(m_ref[0:sub, :], w_ref[...])}
    for k, r0 in enumerate(starts):
        rows = slice(r0, r0 + sub)
        if k + 1 < len(starts):
            nxt = starts[k + 1]
            proj[nxt] = _dot(m_ref[nxt:nxt + sub, :], w_ref[...])
        x = x_ref[rows, :] + gate1 * proj.pop(r0)
        xo_ref[rows, :] = x
        y = x * lax.rsqrt(jnp.mean(x * x, axis=-1, keepdims=True) + EPS) * n2_ref[...]
        h = y * scale2 + shift2
        h_hi = h.astype(BF16)
        h_ref[rows, :] = _pack_bf16_pairs(h_hi)
        h_lo = (h - h_hi.astype(F32)).astype(BF16)
        hi_part = lax.dot_general(w_both, h_hi, contract_last, preferred_element_type=F32)
        lo_part = lax.dot_general(w_hi, h_lo, contract_last, preferred_element_type=F32)
        lg_ref[:, rows] = hi_part[:LANES] + (hi_part[LANES:] + lo_part) + br_ref[...]


def _out_project(merged, x, w_out, mod, norm2, w_router, b_router, *, tm, n_ctx_tiles, tiles_per_batch, ctx_row):
    t_rows, d = x.shape
    kern = functools.partial(_outproj_kernel, n_ctx_tiles=n_ctx_tiles,
                             tiles_per_batch=tiles_per_batch, ctx_row=ctx_row)

    def rows(width):
        return pl.BlockSpec((tm, width), lambda i: (i, 0))

    def modspec(k):
        return pl.BlockSpec((8, d), lambda i: (0, k))

    return pl.pallas_call(
        kern,
        grid=(t_rows // tm,),
        in_specs=[rows(d), rows(d), pl.BlockSpec((d, d), lambda i: (0, 0)),
                  modspec(2), pl.BlockSpec((1, d), lambda i: (0, 0)), modspec(3), modspec(4),
                  pl.BlockSpec((LANES, d), lambda i: (0, 0)), pl.BlockSpec((LANES, 1), lambda i: (0, 0))],
        out_specs=[rows(d), rows(d // 2), pl.BlockSpec((LANES, tm), lambda i: (0, i))],
        out_shape=[jax.ShapeDtypeStruct((t_rows, d), F32), jax.ShapeDtypeStruct((t_rows, d // 2), jnp.uint32),
                   jax.ShapeDtypeStruct((LANES, t_rows), F32)],
        compiler_params=_cparams("parallel"),
        name="out_project",
    )(merged, x, w_out, mod, norm2, mod, mod, w_router, b_router)


def _expert_kernel(be_ref, jb_ref, nblk_ref, ord_ref, nxt_ref, nu_ref,
                   x_ref, wgu_hbm, bgu_ref, wd_hbm, bd_ref, y_ref,
                   wgu_bf, wd_bf, stage, sems, *, layer):
    i = pl.program_id(0)
    n_gu_chunks = wgu_bf.shape[1] // W_CHUNK
    n_chunks = n_gu_chunks + wd_bf.shape[1] // W_CHUNK

    def chunk_copy(e, q):
        if q < n_gu_chunks:
            src = wgu_hbm.at[layer, e, pl.ds(q * W_CHUNK, W_CHUNK), :]
        else:
            src = wd_hbm.at[layer, e, pl.ds((q - n_gu_chunks) * W_CHUNK, W_CHUNK), :]
        return pltpu.make_async_copy(src, stage.at[q % 2], sems.at[q % 2])

    def land(e, q, slot):
        chunk_copy(e, q).wait()
        w = stage[q % 2].astype(BF16)
        if q < n_gu_chunks:
            wgu_bf[slot, pl.ds(q * W_CHUNK, W_CHUNK), :] = w
        else:
            wd_bf[slot, pl.ds((q - n_gu_chunks) * W_CHUNK, W_CHUNK), :] = w
        if q + 2 < n_chunks:
            chunk_copy(e, q + 2).start()

    active = i < nu_ref[0]
    e = be_ref[i]
    j = jb_ref[i]
    n = nblk_ref[i]
    slot = ord_ref[i] % 2
    nxt = nxt_ref[i]

    @pl.when(i == 0)
    def _():
        chunk_copy(e, 0).start()
        chunk_copy(e, 1).start()
        for q in range(n_chunks):
            land(e, q, slot)

    prefetch = jnp.logical_and(active, nxt >= 0)

    @pl.when(jnp.logical_and(prefetch, j == 0))
    def _():
        chunk_copy(nxt, 0).start()
        chunk_copy(nxt, 1).start()

    @pl.when(active)
    def _():
        gu = _dot(_unpack_bf16_pairs(x_ref[...]), wgu_bf[slot]) + bgu_ref[0, 0]
        de = gu.shape[1] // 2
        gate = jnp.minimum(gu[:, :de], SWIGLU_LIMIT)
        up = jnp.clip(gu[:, de:], -SWIGLU_LIMIT, SWIGLU_LIMIT)
        hdn = (up + 1.0) * (gate * _sigmoid(SWIGLU_ALPHA * gate))
        y = _dot(hdn.astype(BF16), wd_bf[slot]) + bd_ref[0, 0]
        y_ref[...] = _pack_bf16_pairs(y.astype(BF16))

    @pl.when(jnp.logical_not(active))
    def _():
        y_ref[...] = jnp.zeros_like(y_ref)

    lo = (n_chunks * j) // n
    hi = (n_chunks * (j + 1)) // n
    for q in range(n_chunks):
        @pl.when(jnp.logical_and(prefetch, jnp.logical_and(lo <= q, q < hi)))
        def _(q=q):
            land(nxt, q, 1 - slot)


def _experts(xs, sched, layer, w_gu, b_gu, w_down, b_down):
    rows_total = xs.shape[0]
    nb = rows_total // MOE_BLOCK
    depth, ne, d, n_gu = w_gu.shape
    assert xs.shape[1] * 2 == d and xs.dtype == jnp.uint32
    de = w_down.shape[2]
    assert n_gu == d and d % W_CHUNK == 0 and de % W_CHUNK == 0

    def blk(i, *s):
        return jnp.minimum(i, s[-1][0] - 1)

    def bspec(n):
        return pl.BlockSpec((1, 1, 1, n), lambda i, *s: (layer, s[0][blk(i, *s)], 0, 0))

    grid_spec = pltpu.PrefetchScalarGridSpec(
        num_scalar_prefetch=len(sched),
        grid=(nb,),
        in_specs=[
            pl.BlockSpec((MOE_BLOCK, d // 2), lambda i, *s: (blk(i, *s), 0)),
            pl.BlockSpec(memory_space=pl.ANY), bspec(n_gu),
            pl.BlockSpec(memory_space=pl.ANY), bspec(d),
        ],
        out_specs=pl.BlockSpec((MOE_BLOCK, d // 2), lambda i, *s: (i, 0)),
        scratch_shapes=[pltpu.VMEM((2, d, n_gu), BF16), pltpu.VMEM((2, de, d), BF16),
                        pltpu.VMEM((2, W_CHUNK, d), F32), pltpu.SemaphoreType.DMA((2,))],
    )
    return pl.pallas_call(
        functools.partial(_expert_kernel, layer=layer),
        grid_spec=grid_spec,
        out_shape=jax.ShapeDtypeStruct((rows_total, d // 2), jnp.uint32),
        compiler_params=_cparams("arbitrary"),
        name="moe_experts",
    )(*sched, xs, w_gu, b_gu.reshape(depth, ne, 1, n_gu), w_down, b_down.reshape(depth, ne, 1, d))


def _route(logits_t, n_tok):
    experts = jnp.arange(N_EXPERTS, dtype=jnp.int32)
    lg = logits_t
    e_iota = lax.broadcasted_iota(jnp.int32, lg.shape, 0)
    vals, idxs = [], []
    for _ in range(TOP_K):
        mx = jnp.max(lg, axis=0)
        ix = jnp.min(jnp.where(lg == mx[None, :], e_iota, N_EXPERTS), axis=0)
        vals.append(mx)
        idxs.append(ix)
        lg = jnp.where(e_iota == ix[None, :], -jnp.inf, lg)
    top_val = jnp.stack(vals)
    top_idx = jnp.stack(idxs)
    probs = jax.nn.softmax(top_val, axis=0)

    m = n_tok * TOP_K
    onehot = (top_idx[:, :, None] == experts[None, None, :]).astype(jnp.int32)
    per_tok = jnp.sum(onehot, axis=0)
    csum = jnp.cumsum(per_tok, axis=0)
    counts = csum[-1]
    pad_counts = (counts + MOE_BLOCK - 1) // MOE_BLOCK * MOE_BLOCK
    pad_end = jnp.cumsum(pad_counts)
    pad_start = pad_end - pad_counts
    base = csum - per_tok + pad_start[None, :]
    pos = jnp.sum(base[None, :, :] * onehot, axis=2)

    nb = (m + MOE_BLOCK - 1) // MOE_BLOCK + N_EXPERTS
    idx_bits = (m - 1).bit_length()
    assert (2 * N_EXPERTS + 2) << idx_bits < 2 ** 31
    assign = lax.broadcasted_iota(jnp.int32, (TOP_K, n_tok), 1) * TOP_K + \
        lax.broadcasted_iota(jnp.int32, (TOP_K, n_tok), 0)
    real_keys = (((2 * top_idx) << idx_bits) | assign).reshape(m)
    n_fill = nb * MOE_BLOCK - m
    assert n_fill % N_EXPERTS == 0 and n_fill // N_EXPERTS >= MOE_BLOCK - 1
    fill_i = lax.broadcasted_iota(jnp.int32, (N_EXPERTS, n_fill // N_EXPERTS), 1)
    fill_keys = jnp.where(fill_i < (pad_counts - counts)[:, None],
                          ((2 * experts + 1) << idx_bits)[:, None], (2 * N_EXPERTS + 1) << idx_bits)
    keys = lax.sort(jnp.concatenate([real_keys, fill_keys.reshape(n_fill)]))
    is_real = ((keys >> idx_bits) & 1) == 0
    row_tok = jnp.where(is_real, (keys & ((1 << idx_bits) - 1)) // TOP_K,
                        jnp.arange(nb * MOE_BLOCK, dtype=jnp.int32) % n_tok)

    blk_start = jnp.arange(nb, dtype=jnp.int32) * MOE_BLOCK
    blk_e = jnp.minimum(jnp.sum((pad_end[None, :] <= blk_start[:, None]).astype(jnp.int32), axis=1),
                        N_EXPERTS - 1)
    n_used = (pad_end[-1] // MOE_BLOCK).astype(jnp.int32).reshape(1)
    has_blocks = pad_counts > 0
    order_e = jnp.cumsum(has_blocks.astype(jnp.int32)) - 1
    later = jnp.logical_and(has_blocks[None, :], experts[None, :] > experts[:, None])
    next_e = jnp.min(jnp.where(later, experts[None, :], N_EXPERTS), axis=1)
    next_e = jnp.where(next_e == N_EXPERTS, -1, next_e)
    blk_is = blk_e[:, None] == experts[None, :]

    def per_block(table):
        return jnp.sum(jnp.where(blk_is, table[None, :], 0), axis=1)

    blk_j = (blk_start - per_block(pad_start)) // MOE_BLOCK
    blk_n = jnp.maximum(per_block(pad_counts) // MOE_BLOCK, 1)
    sched = (blk_e, blk_j, blk_n, per_block(order_e), per_block(next_e), n_used)
    return probs, pos, row_tok, sched


def _rows(x, idx):
    return x.at[idx].get(mode="promise_in_bounds")


def _moe(h2, tok0, logits_t, layer, w_gu, b_gu, w_down, b_down):
    n_tok = logits_t.shape[1]
    probs, pos, row_tok, sched = _route(logits_t, n_tok)
    xs = _rows(h2, row_tok + tok0)
    yb = _experts(xs, sched, layer, w_gu, b_gu, w_down, b_down)
    y = _rows(yb, pos.reshape(-1)).reshape(TOP_K, n_tok, -1)
    return y, probs.T


def _combine_kernel(x_ref, y0_ref, y1_ref, y2_ref, y3_ref, p_ref, g2_ref, nf_ref, o_ref,
                    *, tile0, final, n_ctx_tiles, tiles_per_batch, ctx_row):
    row = _mod_row(pl.program_id(0) + tile0, n_ctx_tiles, tiles_per_batch, ctx_row)
    p = p_ref[...]

    def expert_out(y_ref):
        u = y_ref[0]
        return jnp.concatenate([lax.bitcast_convert_type(u << 16, F32),
                                lax.bitcast_convert_type(u & jnp.uint32(0xFFFF0000), F32)], axis=1)

    y = (p[:, 0:1] * expert_out(y0_ref) + p[:, 1:2] * expert_out(y1_ref)) + \
        (p[:, 2:3] * expert_out(y2_ref) + p[:, 3:4] * expert_out(y3_ref))
    x = x_ref[...] + g2_ref[pl.ds(row, 1), :] * y
    if final:
        x = x * lax.rsqrt(jnp.mean(x * x, axis=-1, keepdims=True) + EPS) * nf_ref[...]
    o_ref[...] = x


def _combine(x, y, probs, mod, norm_f, *, tile0, final, n_ctx_tiles, tiles_per_batch, ctx_row):
    assert y.shape[0] == TOP_K == 4 and y.dtype == jnp.uint32
    n_rows = y.shape[1]
    d = x.shape[1]
    tm = ROW_TILE
    kern = functools.partial(_combine_kernel, tile0=tile0, final=final, n_ctx_tiles=n_ctx_tiles,
                             tiles_per_batch=tiles_per_batch, ctx_row=ctx_row)

    def yspec(k):
        return pl.BlockSpec((1, tm, d // 2), lambda i: (k, i, 0))

    return pl.pallas_call(
        kern,
        grid=(n_rows // tm,),
        in_specs=[pl.BlockSpec((tm, d), lambda i: (tile0 + i, 0)),
                  yspec(0), yspec(1), yspec(2), yspec(3),
                  pl.BlockSpec((tm, TOP_K), lambda i: (i, 0)),
                  pl.BlockSpec((8, d), lambda i: (0, 5)),
                  pl.BlockSpec((1, d), lambda i: (0, 0))],
        out_specs=pl.BlockSpec((tm, d), lambda i: (i, 0)),
        out_shape=jax.ShapeDtypeStruct((n_rows, d), F32),
        compiler_params=_cparams("parallel"),
        name="moe_combine",
    )(x, y, y, y, y, probs, mod, norm_f)


def _embed_kernel(ctx_ref, x_ref, pos_ref, o_ref, *, n_ctx_tiles):
    j = pl.program_id(0)
    b = pl.program_id(1)

    @pl.when(jnp.logical_and(j == 0, b < n_ctx_tiles))
    def _():
        o_ref[...] = ctx_ref[...]

    @pl.when(j > 0)
    def _():
        o_ref[...] = x_ref[...] + pos_ref[...]


def _embed(ctx2, x2, pos, *, batch):
    d = x2.shape[1]
    l = pos.shape[0]
    tm = math.gcd(math.gcd(ctx2.shape[0], l), 2 * ROW_TILE)
    n_ctx_tiles = ctx2.shape[0] // tm
    nt = l // tm
    assert n_ctx_tiles <= batch
    return pl.pallas_call(
        functools.partial(_embed_kernel, n_ctx_tiles=n_ctx_tiles),
        grid=(1 + nt, batch),
        in_specs=[pl.BlockSpec((tm, d), lambda j, b: (jnp.minimum(b, n_ctx_tiles - 1), 0)),
                  pl.BlockSpec((tm, d), lambda j, b: (b * nt + jnp.maximum(j - 1, 0), 0)),
                  pl.BlockSpec((tm, d), lambda j, b: (jnp.maximum(j - 1, 0), 0))],
        out_specs=pl.BlockSpec(
            (tm, d), lambda j, b: (jnp.where(j == 0, jnp.minimum(b, n_ctx_tiles - 1),
                                             n_ctx_tiles + b * nt + j - 1), 0)),
        out_shape=jax.ShapeDtypeStruct((ctx2.shape[0] + x2.shape[0], d), F32),
        compiler_params=_cparams("arbitrary", "arbitrary"),
        name="embed_tokens",
    )(ctx2, x2, pos)


def _grid_pos_embed(n, d):
    rows = n // GRID_W
    quarter = d // 4
    omega = 1.0 / (10000.0 ** (jnp.arange(quarter, dtype=F32) / quarter))

    def enc(count):
        a = jnp.arange(count, dtype=F32)[:, None] * omega
        return jnp.concatenate([jnp.sin(a), jnp.cos(a)], axis=-1)

    row_part = jnp.repeat(enc(rows), GRID_W, axis=0)
    col_part = jnp.tile(enc(GRID_W), (rows, 1))
    return jnp.concatenate([row_part, col_part], axis=-1)


def kernel(x, c, ctx, c_ctx, w_mod, b_mod, norm1, w_in, w_a2, b_a, gla_norm, w_s, b_s, p_gla, p_sg, p_ft,
           w_out, norm2, w_router, b_router, w_gu, b_gu, w_down, b_down, norm_f):
    batch, l, d = x.shape
    lc = ctx.shape[1]
    depth = w_mod.shape[0]
    n_ctx_rows = batch * lc
    t_rows = n_ctx_rows + batch * l
    assert lc % ROW_TILE == 0 and l % ROW_TILE == 0 and batch < 8

    tm1 = math.gcd(math.gcd(n_ctx_rows, l), 1024)
    assert tm1 % ROW_TILE == 0
    proj_tiles = dict(n_ctx_tiles=n_ctx_rows // tm1, tiles_per_batch=l // tm1, ctx_row=batch)
    row_tiles = dict(n_ctx_tiles=n_ctx_rows // ROW_TILE, tiles_per_batch=l // ROW_TILE, ctx_row=batch)
    tm2 = ROW_TILE
    tm3 = math.gcd(tm1, 2 * ROW_TILE)
    big_tiles = dict(n_ctx_tiles=n_ctx_rows // tm3, tiles_per_batch=l // tm3, ctx_row=batch)

    o_q = 0
    o_k = o_q + W_Q
    o_v = o_k + W_Q
    o_r = o_v + W_V
    o_a = o_r + W_V
    o_su = o_a + 2 * GLA_RANK
    o_sv = o_su + W_SG
    o_f = o_sv + W_SG
    o_g = o_f + W_FT
    n_gates = 3 * d
    cols = dict(gates=0, q=n_gates, r=n_gates + 2 * W_Q + W_V, su=n_gates + 2 * W_Q + 2 * W_V)
    cols["sv"] = cols["su"] + W_SG
    cols["f"] = cols["sv"] + W_SG
    n_main = cols["f"] + W_FT

    xt = _embed(ctx.reshape(n_ctx_rows, d), x.reshape(batch * l, d), _grid_pos_embed(l, d), batch=batch)

    cond = jnp.zeros((8, d), F32).at[:batch].set(c).at[batch].set(c_ctx)
    mods = _mod_vectors(cond, w_mod, b_mod)

    cc, sc = _dft_tables(FT_CH)
    chan_tab = jnp.concatenate([cc, sc], axis=1)
    assert l % (FT_RADIX * FT_TILE) == 0
    ft_tables = _ft_long_tables(l)
    cos_c, sin_c = _dft_tables(lc)

    w_main = jnp.concatenate(
        [w_in[:, :, o_g:], w_in[:, :, o_q:o_a], w_in[:, :, o_su:o_g]], axis=2).astype(BF16)
    w_alr = jnp.pad(w_in[:, :, o_a:o_su], ((0, 0), (0, 0), (0, LANES - 2 * GLA_RANK))).astype(BF16)

    for i in range(depth):
        last = i == depth - 1
        w_a = w_alr[i]
        wa2 = jnp.zeros((LANES, 2 * W_Q), F32)
        wa2 = wa2.at[:GLA_RANK, :W_Q].set(w_a2[i, 0]).at[GLA_RANK:2 * GLA_RANK, W_Q:].set(w_a2[i, 1])
        mod = mods[i]

        z, la = _project(xt, mod, norm1[i][None], w_main, w_a, wa2.astype(BF16), b_a[i].reshape(1, 2 * W_Q),
                         layer=i, tm=tm1, tn=1536, n_gate_cols=n_gates, **proj_tiles)
        assert z.shape[1] == n_main

        o_fw, o_bw = _gla(z, la, batch=batch, lc=lc, l=l, col_q=cols["q"])

        xc_c, xs_c = _ft_channel(z, chan_tab, col_f=cols["f"], row0=0, n_rows=n_ctx_rows)
        xc_l, xs_l = _ft_channel(z, chan_tab, col_f=cols["f"], row0=n_ctx_rows, n_rows=batch * l)
        y_ft = (_ft_positions(cos_c, sin_c, xc_c, xs_c, batch=batch, seq=lc),
                _ft_positions_long(ft_tables, xc_l, xs_l, batch=batch, seq=l))

        bs_b = jnp.broadcast_to(b_s[i][:, :, None], (SG_GROUPS, SG_CHUNK, SG_CH))
        merged = _branches(z, o_fw, o_bw, y_ft, gla_norm[i][None], w_s[i].astype(BF16), bs_b,
                           p_gla[i].astype(BF16), p_sg[i].astype(BF16), p_ft[i].astype(BF16), cols=cols, tm=tm2)

        w_r = jnp.pad(w_router[i].T, ((0, LANES - N_EXPERTS), (0, 0)))
        b_r = jnp.pad(b_router[i], (0, LANES - N_EXPERTS)).reshape(LANES, 1)
        xt, h2, logits_t = _out_project(merged, xt, w_out[i].astype(BF16), mod, norm2[i][None], w_r, b_r,
                                      tm=tm3, **big_tiles)

        row0 = n_ctx_rows if last else 0
        y, probs = _moe(h2, row0, logits_t[:N_EXPERTS, row0:], i, w_gu, b_gu, w_down, b_down)
        xt = _combine(xt, y, probs, mod, norm_f[None], tile0=row0 // ROW_TILE, final=last, **row_tiles)

    return xt.reshape(batch, l, d)
```
